```python
import math
import jax, jax.numpy as jnp
from jax import lax
import numpy as np

D_MODEL = 1024
BATCH = 2
SEQ = 8192
DEPTH = 1

MIX_WIDTH = D_MODEL
CONV_WIDTH = MIX_WIDTH // 2
FFT_WIDTH = MIX_WIDTH - CONV_WIDTH
GROUP_DIM = 64
CONV_HEADS = CONV_WIDTH // GROUP_DIM
FFT_GROUPS = FFT_WIDTH // GROUP_DIM
IN_PROJ_WIDTH = 3 * CONV_WIDTH + FFT_WIDTH
CONV_K = 3
MEM_LEN = 256
XATTN_HEADS = 4
XATTN_HEAD_DIM = D_MODEL // XATTN_HEADS
N_EXPERTS = 32
TOP_K = 4
D_EXPERT = D_MODEL
SWIGLU_LIMIT = 7.0
SWIGLU_ALPHA = 1.702
ROW_BLOCK = 128
EPS = 1e-5

kernel_name = "hymba_conv_fnet_xmem_moe_encoder"


def rmsnorm(x, g):
    xf = x.astype(jnp.float32)
    y = xf * lax.rsqrt(jnp.mean(xf * xf, axis=-1, keepdims=True) + EPS)
    return (y * g.astype(jnp.float32)).astype(x.dtype)


def short_conv3(u, w):
    up = jnp.pad(u, ((0, 0), (1, 1), (0, 0)))
    return w[0] * up[:, :-2] + w[1] * up[:, 1:-1] + w[2] * up[:, 2:]


def hybrid_mixer(h, w_in, conv_w, g_conv_out, g_fft_out, w_out):
    bsz, s, _ = h.shape
    z = h @ w_in
    b_gate = z[..., :CONV_WIDTH]
    c_gate = z[..., CONV_WIDTH:2 * CONV_WIDTH]
    v = z[..., 2 * CONV_WIDTH:3 * CONV_WIDTH]
    u = z[..., 3 * CONV_WIDTH:]
    y_conv = b_gate * short_conv3(c_gate * v, conv_w)
    uf = u.astype(jnp.float32).reshape(bsz, s, FFT_GROUPS, GROUP_DIM)
    y_fft = jnp.fft.fftn(uf, axes=(1, 3), norm="ortho").real
    y_fft = y_fft.reshape(bsz, s, FFT_WIDTH).astype(h.dtype)
    y = jnp.concatenate([rmsnorm(y_conv, g_conv_out), rmsnorm(y_fft, g_fft_out)], axis=-1)
    return y @ w_out


def memory_cross_attention(h, mem_n, w_q, w_k, w_v, w_o):
    bsz, s, d = h.shape
    m = mem_n.shape[1]
    q = (h @ w_q).reshape(bsz, s, XATTN_HEADS, XATTN_HEAD_DIM)
    k = (mem_n @ w_k).reshape(bsz, m, XATTN_HEADS, XATTN_HEAD_DIM)
    v = (mem_n @ w_v).reshape(bsz, m, XATTN_HEADS, XATTN_HEAD_DIM)
    scores = jnp.einsum('bshd,bmhd->bhsm', q, k).astype(jnp.float32) * (XATTN_HEAD_DIM ** -0.5)
    p = jax.nn.softmax(scores, axis=-1).astype(v.dtype)
    o = jnp.einsum('bhsm,bmhd->bshd', p, v).reshape(bsz, s, d)
    return o @ w_o


def moe_ffn(h, w_router, b_router, w_gate_up, b_gate_up, w_down, b_down):
    bsz, s, d = h.shape
    t = bsz * s
    hf = h.reshape(t, d)
    logits = (hf @ w_router).astype(jnp.float32) + b_router.astype(jnp.float32)
    top_v, top_i = lax.top_k(logits, TOP_K)
    top_w = jax.nn.softmax(top_v, axis=-1)
    n_assign = t * TOP_K
    flat_e = top_i.reshape(n_assign)
    flat_tok = jnp.repeat(jnp.arange(t, dtype=jnp.int32), TOP_K)
    flat_w = top_w.reshape(n_assign)
    order = jnp.argsort(flat_e)
    sorted_e = flat_e[order]
    counts = jnp.bincount(flat_e, length=N_EXPERTS)
    padded = ((counts + ROW_BLOCK - 1) // ROW_BLOCK) * ROW_BLOCK
    start = jnp.cumsum(counts) - counts
    pad_end = jnp.cumsum(padded)
    pad_start = pad_end - padded
    dest = pad_start[sorted_e] + jnp.arange(n_assign) - start[sorted_e]
    n_blocks = -(-n_assign // ROW_BLOCK) + N_EXPERTS
    n_rows = n_blocks * ROW_BLOCK
    row_tok = jnp.full((n_rows,), t, dtype=jnp.int32).at[dest].set(flat_tok[order])
    row_w = jnp.zeros((n_rows,), jnp.float32).at[dest].set(flat_w[order])
    block_e = jnp.minimum(
        jnp.searchsorted(pad_end, jnp.arange(n_blocks) * ROW_BLOCK, side='right'), N_EXPERTS - 1)
    h_pad = jnp.concatenate([hf, jnp.zeros((1, d), hf.dtype)], axis=0)
    xs = h_pad[row_tok].reshape(n_blocks, ROW_BLOCK, d)

    def expert_block(args):
        xb, e = args
        gu = xb @ w_gate_up[e] + b_gate_up[e]
        gate = jnp.minimum(gu[:, :D_EXPERT], SWIGLU_LIMIT)
        up = jnp.clip(gu[:, D_EXPERT:], -SWIGLU_LIMIT, SWIGLU_LIMIT)
        glu = gate * jax.nn.sigmoid(SWIGLU_ALPHA * gate)
        return ((up + 1.0) * glu) @ w_down[e] + b_down[e]

    out = lax.map(expert_block, (xs, block_e)).reshape(n_rows, d)
    y = jnp.zeros((t + 1, d), out.dtype).at[row_tok].add(out * row_w[:, None].astype(out.dtype))
    return y[:t].reshape(bsz, s, d)


def setup_inputs(seed: int = 0) -> dict:
    key = jax.random.key(seed)
    ks = jax.random.split(key, 24)
    f32 = jnp.float32

    def nrm(k, shape, scale):
        return jax.random.normal(k, shape, f32) * scale

    def gain(k, shape):
        return 1.0 + 0.02 * jax.random.normal(k, shape, f32)

    L = DEPTH
    return {
        "x": nrm(ks[0], (BATCH, SEQ, D_MODEL), 1.0),
        "mem": nrm(ks[1], (BATCH, MEM_LEN, D_MODEL), 1.0),
        "norm_mix": gain(ks[2], (L, D_MODEL)),
        "w_in": nrm(ks[3], (L, D_MODEL, IN_PROJ_WIDTH), D_MODEL ** -0.5),
        "conv_w": nrm(ks[4], (L, CONV_K, CONV_WIDTH), CONV_K ** -0.5),
        "g_conv_out": gain(ks[5], (L, CONV_WIDTH)),
        "g_fft_out": gain(ks[6], (L, FFT_WIDTH)),
        "w_out": nrm(ks[7], (L, MIX_WIDTH, D_MODEL), MIX_WIDTH ** -0.5),
        "norm_xattn": gain(ks[8], (L, D_MODEL)),
        "norm_mem": gain(ks[9], (L, D_MODEL)),
        "w_q": nrm(ks[10], (L, D_MODEL, D_MODEL), D_MODEL ** -0.5),
        "w_k": nrm(ks[11], (L, D_MODEL, D_MODEL), D_MODEL ** -0.5),
        "w_v": nrm(ks[12], (L, D_MODEL, D_MODEL), D_MODEL ** -0.5),
        "w_o": nrm(ks[13], (L, D_MODEL, D_MODEL), D_MODEL ** -0.5),
        "norm_ffn": gain(ks[14], (L, D_MODEL)),
        "w_router": nrm(ks[15], (L, D_MODEL, N_EXPERTS), D_MODEL ** -0.5),
        "b_router": nrm(ks[16], (L, N_EXPERTS), 0.01),
        "w_gate_up": nrm(ks[17], (L, N_EXPERTS, D_MODEL, 2 * D_EXPERT), D_MODEL ** -0.5),
        "b_gate_up": nrm(ks[18], (L, N_EXPERTS, 2 * D_EXPERT), 0.02),
        "w_down": nrm(ks[19], (L, N_EXPERTS, D_EXPERT, D_MODEL), D_EXPERT ** -0.5),
        "b_down": nrm(ks[20], (L, N_EXPERTS, D_MODEL), 0.02),
        "norm_final": gain(ks[21], (D_MODEL,)),
    }


def reference(x, mem, norm_mix, w_in, conv_w, g_conv_out, g_fft_out, w_out,
              norm_xattn, norm_mem, w_q, w_k, w_v, w_o, norm_ffn, w_router, b_router,
              w_gate_up, b_gate_up, w_down, b_down, norm_final):
    for l in range(DEPTH):
        h = rmsnorm(x, norm_mix[l])
        x = x + hybrid_mixer(h, w_in[l], conv_w[l], g_conv_out[l], g_fft_out[l], w_out[l])
        h = rmsnorm(x, norm_xattn[l])
        mem_n = rmsnorm(mem, norm_mem[l])
        x = x + memory_cross_attention(h, mem_n, w_q[l], w_k[l], w_v[l], w_o[l])
        h = rmsnorm(x, norm_ffn[l])
        x = x + moe_ffn(h, w_router[l], b_router[l], w_gate_up[l], b_gate_up[l], w_down[l], b_down[l])
    return rmsnorm(x, norm_final)
```

```python
import functools
import math

import numpy as np
import jax
import jax.numpy as jnp
from jax import lax
from jax.experimental import pallas as pl
from jax.experimental.pallas import tpu as pltpu

F32 = jnp.float32
BF16 = jnp.bfloat16
I32 = jnp.int32

D_MODEL = 1024
CONV_WIDTH = 512
FFT_WIDTH = 512
GROUP_DIM = 64
IN_PROJ_WIDTH = 3 * CONV_WIDTH + FFT_WIDTH
MEM_LEN = 256
XATTN_HEADS = 4
XATTN_HEAD_DIM = D_MODEL // XATTN_HEADS
N_EXPERTS = 32
TOP_K = 4
D_EXPERT = D_MODEL
SWIGLU_LIMIT = 7.0
SWIGLU_ALPHA = 1.702
EPS = 1e-5

FFT_N1 = 64
FFT_N2 = 128

TOK_TILE = 512
ROW_BLOCK = 256
ROUTE_TILE = 256
FFT1_LANES = 8192
FFT2_K1 = 8
BF16_SUBLANES = 16
VMEM_LIMIT = 56 * 1024 * 1024


def _rms(x, g):
    return x * lax.rsqrt(jnp.mean(x * x, axis=-1, keepdims=True) + EPS) * g


def _dot(a, b):
    return jnp.dot(a, b, preferred_element_type=F32)


def _dot_nt(a, b):
    return lax.dot_general(a, b, (((1,), (1,)), ((), ())), preferred_element_type=F32)


def _dft_tables(seq):
    assert seq == FFT_N1 * FFT_N2
    c = np.arange(GROUP_DIM)
    ang = 2.0 * np.pi * ((c[:, None] * c[None, :]) % GROUP_DIM) / GROUP_DIM
    groups = FFT_WIDTH // GROUP_DIM
    eye = np.eye(groups)
    cs = np.concatenate([np.kron(eye, np.cos(ang)), np.kron(eye, np.sin(ang))], axis=1) / math.sqrt(GROUP_DIM)
    n1 = np.arange(FFT_N1)
    a1 = 2.0 * np.pi * ((n1[:, None] * n1[None, :]) % FFT_N1) / FFT_N1
    c1, s1 = np.cos(a1), np.sin(a1)
    m1 = np.block([[c1, -s1], [s1, c1]]) / math.sqrt(FFT_N1)
    k1 = np.arange(FFT_N1)[:, None, None]
    k2 = np.arange(FFT_N2)[None, :, None]
    n2 = np.arange(FFT_N2)[None, None, :]
    a2 = 2.0 * np.pi * ((n2 * (k1 + FFT_N1 * k2)) % seq) / seq
    m2 = np.concatenate([np.cos(a2), -np.sin(a2)], axis=2) / math.sqrt(FFT_N2)
    return (jnp.asarray(cs, F32).astype(BF16), jnp.asarray(m1, F32).astype(BF16),
            jnp.asarray(m2, F32).astype(BF16))


def _memkv_kernel(mem_ref, g_ref, wk_ref, wv_ref, k_ref, v_ref):
    m = _rms(mem_ref[0], g_ref[...]).astype(BF16)
    k_ref[0] = _dot(m, wk_ref[...]).astype(BF16)
    v_ref[0] = _dot(m, wv_ref[...]).astype(BF16)


def _memkv(mem, g, wk, wv):
    bsz = mem.shape[0]
    full = lambda shape: pl.BlockSpec(shape, lambda b: (0,) * len(shape))
    per_b = pl.BlockSpec((1, MEM_LEN, D_MODEL), lambda b: (b, 0, 0))
    return pl.pallas_call(
        _memkv_kernel,
        grid=(bsz,),
        in_specs=[per_b, full((1, D_MODEL)), full((D_MODEL, D_MODEL)), full((D_MODEL, D_MODEL))],
        out_specs=[per_b, per_b],
        out_shape=[jax.ShapeDtypeStruct((bsz, MEM_LEN, D_MODEL), BF16)] * 2,
        compiler_params=pltpu.CompilerParams(dimension_semantics=("arbitrary",), vmem_limit_bytes=VMEM_LIMIT),
        name="memkv",
    )(mem, g, wk, wv)


def _inproj_kernel(x_ref, g_ref, win_ref, cs_ref, b_ref, cv_ref, a_ref, s_ref):
    h = _rms(x_ref[...], g_ref[...]).astype(BF16)
    z = _dot(h, win_ref[...])
    b_ref[...] = z[:, :CONV_WIDTH].astype(BF16)
    cv_ref[...] = (z[:, CONV_WIDTH:2 * CONV_WIDTH] * z[:, 2 * CONV_WIDTH:3 * CONV_WIDTH]).astype(BF16)
    u = z[:, 3 * CONV_WIDTH:].astype(BF16)
    ab = _dot(u, cs_ref[...])
    a_ref[...] = ab[:, :FFT_WIDTH].astype(BF16)
    s_ref[...] = ab[:, FFT_WIDTH:].astype(BF16)


def _inproj(x2d, g, w_in, cs):
    t = x2d.shape[0]
    tile = pl.BlockSpec((TOK_TILE, D_MODEL), lambda i: (i, 0))
    half = pl.BlockSpec((TOK_TILE, CONV_WIDTH), lambda i: (i, 0))
    full = lambda shape: pl.BlockSpec(shape, lambda i: (0,) * len(shape))
    return pl.pallas_call(
        _inproj_kernel,
        grid=(t // TOK_TILE,),
        in_specs=[tile, full((1, D_MODEL)), full((D_MODEL, IN_PROJ_WIDTH)), full((FFT_WIDTH, 2 * FFT_WIDTH))],
        out_specs=[half] * 4,
        out_shape=[jax.ShapeDtypeStruct((t, CONV_WIDTH), BF16)] * 4,
        compiler_params=pltpu.CompilerParams(dimension_semantics=("parallel",), vmem_limit_bytes=VMEM_LIMIT),
        name="inproj",
    )(x2d, g, w_in, cs)


def _fft1_kernel(a_ref, s_ref, m1_ref, g_ref):
    x = jnp.concatenate([a_ref[0], s_ref[0]], axis=0)
    g_ref[0] = _dot(m1_ref[...], x).astype(BF16)


def _fft1(a3, s3, m1):
    bsz, _, cols = a3.shape
    blk = pl.BlockSpec((1, FFT_N1, FFT1_LANES), lambda b, j: (b, 0, j))
    return pl.pallas_call(
        _fft1_kernel,
        grid=(bsz, cols // FFT1_LANES),
        in_specs=[blk, blk, pl.BlockSpec((2 * FFT_N1, 2 * FFT_N1), lambda b, j: (0, 0))],
        out_specs=pl.BlockSpec((1, 2 * FFT_N1, FFT1_LANES), lambda b, j: (b, 0, j)),
        out_shape=jax.ShapeDtypeStruct((bsz, 2 * FFT_N1, cols), BF16),
        compiler_params=pltpu.CompilerParams(dimension_semantics=("parallel", "parallel"),
                                             vmem_limit_bytes=VMEM_LIMIT),
        name="fft1",
    )(a3, s3, m1)


def _fft2_kernel(g_ref, m2_ref, y_ref):
    for j in range(FFT2_K1):
        x = jnp.concatenate([g_ref[0, 0, j], g_ref[0, 1, j]], axis=0)
        y_ref[0, :, j * FFT_WIDTH:(j + 1) * FFT_WIDTH] = _dot(m2_ref[j], x).astype(BF16)


def _fft2(g5, m2):
    bsz = g5.shape[0]
    return pl.pallas_call(
        _fft2_kernel,
        grid=(bsz, FFT_N1 // FFT2_K1),
        in_specs=[pl.BlockSpec((1, 2, FFT2_K1, FFT_N2, FFT_WIDTH), lambda b, j: (b, 0, j, 0, 0)),
                  pl.BlockSpec((FFT2_K1, FFT_N2, 2 * FFT_N2), lambda b, j: (j, 0, 0))],
        out_specs=pl.BlockSpec((1, FFT_N2, FFT2_K1 * FFT_WIDTH), lambda b, j: (b, 0, j)),
        out_shape=jax.ShapeDtypeStruct((bsz, FFT_N2, FFT_N1 * FFT_WIDTH), BF16),
        compiler_params=pltpu.CompilerParams(dimension_semantics=("parallel", "parallel"),
                                             vmem_limit_bytes=VMEM_LIMIT),
        name="fft2",
    )(g5, m2)


def _post_kernel(x_ref, bg_ref, cv_ref, cvp_ref, cvn_ref, yf_ref, convw_ref, gc_ref, gf_ref,
                 wot_ref, wob_ref, nx_ref, wq_ref, k_ref, v_ref, wo_ref, nf_ref,
                 wrh_ref, wrl_ref, br_ref, tri_ref,
                 x2_ref, h3_ref, idx_ref, w_ref, rank_ref, cnt_ref, carry_ref):
    b = pl.program_id(0)
    i = pl.program_id(1)
    last = pl.num_programs(1) - 1

    @pl.when(jnp.logical_and(b == 0, i == 0))
    def _():
        carry_ref[...] = jnp.zeros_like(carry_ref)

    cv = cv_ref[0].astype(F32)
    prev_row = jnp.where(i > 0, cvp_ref[0].astype(F32)[BF16_SUBLANES - 1:BF16_SUBLANES, :], 0.0)
    next_row = jnp.where(i < last, cvn_ref[0].astype(F32)[0:1, :], 0.0)
    rows = lax.broadcasted_iota(I32, cv.shape, 0)
    cvm1 = jnp.where(rows == 0, prev_row, pltpu.roll(cv, 1, axis=0))
    cvp1 = jnp.where(rows == TOK_TILE - 1, next_row, pltpu.roll(cv, TOK_TILE - 1, axis=0))
    cw = convw_ref[...]
    y_conv = bg_ref[0].astype(F32) * (cw[0:1] * cvm1 + cw[1:2] * cv + cw[2:3] * cvp1)
    yc_n = _rms(y_conv, gc_ref[...]).astype(BF16)
    yf_n = _rms(yf_ref[0].astype(F32), gf_ref[...]).astype(BF16)
    x1 = x_ref[0] + _dot(yc_n, wot_ref[...]) + _dot(yf_n, wob_ref[...])

    h2 = _rms(x1, nx_ref[...]).astype(BF16)
    q = _dot(h2, wq_ref[...]).astype(BF16)
    heads = []
    for hd in range(XATTN_HEADS):
        sl = slice(hd * XATTN_HEAD_DIM, (hd + 1) * XATTN_HEAD_DIM)
        s = _dot_nt(q[:, sl], k_ref[0, :, sl]) * (XATTN_HEAD_DIM ** -0.5)
        e = jnp.exp(s - jnp.max(s, axis=-1, keepdims=True))
        p = e * (1.0 / jnp.sum(e, axis=-1, keepdims=True))
        heads.append(_dot(p.astype(BF16), v_ref[0, :, sl]).astype(BF16))
    x2 = x1 + _dot(jnp.concatenate(heads, axis=-1), wo_ref[...])
    x2_ref[0] = x2

    h3 = _rms(x2, nf_ref[...])
    h3_ref[0] = h3
    h3h = h3.astype(BF16)
    h3l = (h3 - h3h.astype(F32)).astype(BF16)
    logits = (_dot_nt(wrh_ref[...], h3h) + _dot_nt(wrh_ref[...], h3l) + _dot_nt(wrl_ref[...], h3h)
              + br_ref[:, 0:1])
    eidx = lax.broadcasted_iota(I32, logits.shape, 0)
    work = logits
    vals, idxs = [], []
    for _ in range(TOP_K):
        m = jnp.max(work, axis=0, keepdims=True)
        ik = jnp.min(jnp.where(work == m, eidx, N_EXPERTS), axis=0, keepdims=True)
        vals.append(m)
        idxs.append(ik)
        work = jnp.where(eidx == ik, -jnp.inf, work)
    ex = [jnp.exp(v - vals[0]) for v in vals]
    inv_den = 1.0 / (ex[0] + ex[1] + ex[2] + ex[3])
    idx_ref[...] = jnp.concatenate(idxs, axis=0)
    w_ref[...] = jnp.concatenate([e * inv_den for e in ex], axis=0)

    sel = jnp.zeros(logits.shape, F32)
    for ik in idxs:
        sel = sel + jnp.where(eidx == ik, 1.0, 0.0)
    full = carry_ref[:, 0:1] + _dot(sel.astype(BF16), tri_ref[...])
    rank_ref[...] = jnp.concatenate(
        [jnp.sum(jnp.where(eidx == ik, full, 0.0), axis=0, keepdims=True) for ik in idxs], axis=0).astype(I32)
    carry_ref[...] = carry_ref[...] + jnp.sum(sel, axis=1, keepdims=True)
    cnt_ref[...] = carry_ref[...]


def _post(x3, bg, cv, yf, conv_w, gc, gf, wo_top, wo_bot, nx, wq, kmem, vmem_, wo, nf, wrh, wrl, br, tri):
    bsz, seq, _ = x3.shape
    nt = seq // TOK_TILE
    t = bsz * seq
    halo_per_tile = TOK_TILE // BF16_SUBLANES
    n_halo = seq // BF16_SUBLANES
    full = lambda shape: pl.BlockSpec(shape, lambda b, i: (0,) * len(shape))
    tile_d = pl.BlockSpec((1, TOK_TILE, D_MODEL), lambda b, i: (b, i, 0))
    tile_h = pl.BlockSpec((1, TOK_TILE, CONV_WIDTH), lambda b, i: (b, i, 0))
    halo_prev = pl.BlockSpec((1, BF16_SUBLANES, CONV_WIDTH),
                             lambda b, i: (b, jnp.maximum(i * halo_per_tile - 1, 0), 0))
    halo_next = pl.BlockSpec((1, BF16_SUBLANES, CONV_WIDTH),
                             lambda b, i: (b, jnp.minimum((i + 1) * halo_per_tile, n_halo - 1), 0))
    mem_blk = pl.BlockSpec((1, MEM_LEN, D_MODEL), lambda b, i: (b, 0, 0))
    tok4 = pl.BlockSpec((TOP_K, TOK_TILE), lambda b, i: (0, b * nt + i))
    return pl.pallas_call(
        _post_kernel,
        grid=(bsz, nt),
        in_specs=[tile_d, tile_h, tile_h, halo_prev, halo_next, tile_h,
                  full((3, CONV_WIDTH)), full((1, CONV_WIDTH)), full((1, FFT_WIDTH)),
                  full((CONV_WIDTH, D_MODEL)), full((FFT_WIDTH, D_MODEL)), full((1, D_MODEL)),
                  full((D_MODEL, D_MODEL)), mem_blk, mem_blk, full((D_MODEL, D_MODEL)), full((1, D_MODEL)),
                  full((N_EXPERTS, D_MODEL)), full((N_EXPERTS, D_MODEL)), full((N_EXPERTS, 128)),
                  full((TOK_TILE, TOK_TILE))],
        out_specs=[tile_d, tile_d, tok4, tok4, tok4, full((N_EXPERTS, 128))],
        out_shape=[jax.ShapeDtypeStruct((bsz, seq, D_MODEL), F32),
                   jax.ShapeDtypeStruct((bsz, seq, D_MODEL), F32),
                   jax.ShapeDtypeStruct((TOP_K, t), I32),
                   jax.ShapeDtypeStruct((TOP_K, t), F32),
                   jax.ShapeDtypeStruct((TOP_K, t), I32),
                   jax.ShapeDtypeStruct((N_EXPERTS, 128), F32)],
        scratch_shapes=[pltpu.VMEM((N_EXPERTS, 128), F32)],
        compiler_params=pltpu.CompilerParams(dimension_semantics=("arbitrary", "arbitrary"),
                                             vmem_limit_bytes=VMEM_LIMIT),
        name="post",
    )(x3, bg, cv, cv, cv, yf, conv_w, gc, gf, wo_top, wo_bot, nx, wq, kmem, vmem_, wo, nf, wrh, wrl, br, tri)


def _row_copy_out(src_ref, dst_ref, sem, r, d):
    return pltpu.make_async_copy(src_ref.at[pl.ds(r, 1)], dst_ref.at[pl.ds(d, 1)], sem)


def _dispatch_kernel(dest_ref, h3_ref, xs_in_ref, xs_ref, sem):
    del xs_in_ref
    n = TOP_K * ROUTE_TILE

    def issue(j, c):
        _row_copy_out(h3_ref, xs_ref, sem, lax.rem(j, ROUTE_TILE), dest_ref[0, j]).start()
        return c

    lax.fori_loop(0, n, issue, 0)

    def drain(j, c):
        _row_copy_out(h3_ref, xs_ref, sem, 0, 0).wait()
        return c

    lax.fori_loop(0, n, drain, 0)


def _dispatch(dest_tiles, h3, xs_init):
    t = h3.shape[0]
    return pl.pallas_call(
        _dispatch_kernel,
        grid=(t // ROUTE_TILE,),
        in_specs=[pl.BlockSpec((None, 1, TOP_K * ROUTE_TILE), lambda i: (i, 0, 0), memory_space=pltpu.SMEM),
                  pl.BlockSpec((ROUTE_TILE, D_MODEL), lambda i: (i, 0)),
                  pl.BlockSpec(memory_space=pl.ANY)],
        out_specs=pl.BlockSpec(memory_space=pl.ANY),
        out_shape=jax.ShapeDtypeStruct(xs_init.shape, F32),
        scratch_shapes=[pltpu.SemaphoreType.DMA],
        input_output_aliases={2: 0},
        compiler_params=pltpu.CompilerParams(dimension_semantics=("arbitrary",), vmem_limit_bytes=VMEM_LIMIT),
        name="dispatch",
    )(dest_tiles, h3, xs_init)


def _expert_kernel(be_ref, nu_ref, xs_ref, wgu_ref, bgu_ref, wd_ref, bd_ref, o_ref):
    del be_ref
    used = pl.program_id(0) < nu_ref[0]

    @pl.when(jnp.logical_not(used))
    def _():
        o_ref[...] = jnp.zeros_like(o_ref)

    @pl.when(used)
    def _():
        x = xs_ref[...].astype(BF16)
        gu = _dot(x, wgu_ref[0]) + bgu_ref[0]
        gate = jnp.minimum(gu[:, :D_EXPERT], SWIGLU_LIMIT)
        up = jnp.clip(gu[:, D_EXPERT:], -SWIGLU_LIMIT, SWIGLU_LIMIT)
        glu = gate * (1.0 / (1.0 + jnp.exp(-SWIGLU_ALPHA * gate)))
        h = ((up + 1.0) * glu).astype(BF16)
        o_ref[...] = _dot(h, wd_ref[0]) + bd_ref[0]


def _experts(block_e, n_used, xs, wgu, bgu, wd, bd):
    n_blocks = xs.shape[0] // ROW_BLOCK
    row_map = lambda j, be, nu: (jnp.minimum(j, nu[0] - 1), 0)
    exp_map = lambda j, be, nu: (be[j], 0, 0)
    grid_spec = pltpu.PrefetchScalarGridSpec(
        num_scalar_prefetch=2,
        grid=(n_blocks,),
        in_specs=[pl.BlockSpec((ROW_BLOCK, D_MODEL), row_map),
                  pl.BlockSpec((1, D_MODEL, 2 * D_EXPERT), exp_map),
                  pl.BlockSpec((1, 1, 2 * D_EXPERT), exp_map),
                  pl.BlockSpec((1, D_EXPERT, D_MODEL), exp_map),
                  pl.BlockSpec((1, 1, D_MODEL), exp_map)],
        out_specs=pl.BlockSpec((ROW_BLOCK, D_MODEL), lambda j, be, nu: (j, 0)),
    )
    return pl.pallas_call(
        _expert_kernel,
        grid_spec=grid_spec,
        out_shape=jax.ShapeDtypeStruct(xs.shape, F32),
        compiler_params=pltpu.CompilerParams(dimension_semantics=("arbitrary",), vmem_limit_bytes=VMEM_LIMIT),
        name="experts",
    )(block_e, n_used, xs, wgu, bgu, wd, bd)


def _row_copy_in(src_ref, dst_ref, sem, s, r):
    return pltpu.make_async_copy(src_ref.at[pl.ds(s, 1)], dst_ref.at[pl.ds(r, 1)], sem)


def _combine_kernel(dest_ref, eo_ref, x2_ref, w_ref, g_ref, out_ref, rows_ref, sem):
    n = TOP_K * ROUTE_TILE

    def issue(j, c):
        _row_copy_in(eo_ref, rows_ref, sem, dest_ref[0, j], j).start()
        return c

    lax.fori_loop(0, n, issue, 0)

    def drain(j, c):
        _row_copy_in(eo_ref, rows_ref, sem, 0, 0).wait()
        return c

    lax.fori_loop(0, n, drain, 0)

    y = x2_ref[...]
    for k in range(TOP_K):
        y = y + w_ref[:, k:k + 1] * rows_ref[k * ROUTE_TILE:(k + 1) * ROUTE_TILE, :]
    out_ref[...] = _rms(y, g_ref[...])


def _combine(dest_tiles, eo, x2, w_cols, g):
    t = x2.shape[0]
    return pl.pallas_call(
        _combine_kernel,
        grid=(t // ROUTE_TILE,),
        in_specs=[pl.BlockSpec((None, 1, TOP_K * ROUTE_TILE), lambda i: (i, 0, 0), memory_space=pltpu.SMEM),
                  pl.BlockSpec(memory_space=pl.ANY),
                  pl.BlockSpec((ROUTE_TILE, D_MODEL), lambda i: (i, 0)),
                  pl.BlockSpec((ROUTE_TILE, TOP_K), lambda i: (i, 0)),
                  pl.BlockSpec((1, D_MODEL), lambda i: (0, 0))],
        out_specs=pl.BlockSpec((ROUTE_TILE, D_MODEL), lambda i: (i, 0)),
        out_shape=jax.ShapeDtypeStruct((t, D_MODEL), F32),
        scratch_shapes=[pltpu.VMEM((TOP_K * ROUTE_TILE, D_MODEL), F32), pltpu.SemaphoreType.DMA],
        compiler_params=pltpu.CompilerParams(dimension_semantics=("arbitrary",), vmem_limit_bytes=VMEM_LIMIT),
        name="combine",
    )(dest_tiles, eo, x2, w_cols, g)


def _layer(x, mem, norm_mix, w_in, conv_w, g_conv_out, g_fft_out, w_out, norm_xattn, norm_mem,
           w_q, w_k, w_v, w_o, norm_ffn, w_router, b_router, w_gate_up, b_gate_up, w_down, b_down, tables):
    bsz, seq, d = x.shape
    t = bsz * seq
    cs, m1, m2 = tables
    row = lambda v: v.reshape(1, -1)

    kmem, vmem_ = _memkv(mem, row(norm_mem), w_k.astype(BF16), w_v.astype(BF16))

    bg, cv, a, s = _inproj(x.reshape(t, d), row(norm_mix), w_in.astype(BF16), cs)
    cols = FFT_N2 * FFT_WIDTH
    g = _fft1(a.reshape(bsz, FFT_N1, cols), s.reshape(bsz, FFT_N1, cols), m1)
    yf = _fft2(g.reshape(bsz, 2, FFT_N1, FFT_N2, FFT_WIDTH), m2).reshape(bsz, seq, FFT_WIDTH)

    w_out_b = w_out.astype(BF16)
    wr_t = w_router.T
    wr_hi = wr_t.astype(BF16)
    wr_lo = (wr_t - wr_hi.astype(F32)).astype(BF16)
    tri = (jnp.arange(TOK_TILE)[:, None] < jnp.arange(TOK_TILE)[None, :]).astype(BF16)
    x2, h3, idx_t, w_t, rank_t, cnt = _post(
        x, bg.reshape(bsz, seq, CONV_WIDTH), cv.reshape(bsz, seq, CONV_WIDTH), yf,
        conv_w, row(g_conv_out), row(g_fft_out), w_out_b[:CONV_WIDTH], w_out_b[CONV_WIDTH:],
        row(norm_xattn), w_q.astype(BF16), kmem, vmem_, w_o.astype(BF16), row(norm_ffn),
        wr_hi, wr_lo, jnp.broadcast_to(b_router[:, None], (N_EXPERTS, 128)), tri)

    counts = cnt[:, 0].astype(I32)
    padded = ((counts + ROW_BLOCK - 1) // ROW_BLOCK) * ROW_BLOCK
    pad_end = jnp.cumsum(padded)
    pad_start = pad_end - padded
    n_blocks = (t * TOP_K) // ROW_BLOCK + N_EXPERTS
    n_used = (pad_end[-1] // ROW_BLOCK).astype(I32)
    blk = jnp.minimum(jnp.arange(n_blocks, dtype=I32), n_used - 1)
    block_e = jnp.minimum(jnp.searchsorted(pad_end, blk * ROW_BLOCK, side="right"), N_EXPERTS - 1).astype(I32)
    dest = pad_start[idx_t] + rank_t
    n_tiles = t // ROUTE_TILE
    dest_tiles = dest.reshape(TOP_K, n_tiles, ROUTE_TILE).transpose(1, 0, 2).reshape(n_tiles, 1, TOP_K * ROUTE_TILE)

    xs = _dispatch(dest_tiles, h3.reshape(t, d), jnp.zeros((n_blocks * ROW_BLOCK, d), F32))
    eo = _experts(block_e, n_used.reshape(1), xs, w_gate_up.astype(BF16), b_gate_up[:, None, :],
                  w_down.astype(BF16), b_down[:, None, :])
    return x2.reshape(t, d), eo, dest_tiles, w_t.T


def kernel(x, mem, norm_mix, w_in, conv_w, g_conv_out, g_fft_out, w_out, norm_xattn, norm_mem, w_q, w_k, w_v, w_o,
           norm_ffn, w_router, b_router, w_gate_up, b_gate_up, w_down, b_down, norm_final):
    bsz, seq, d = x.shape
    depth = norm_mix.shape[0]
    assert depth == 1, "final norm is fused into the combine step of the single layer"
    tables = _dft_tables(seq)
    x2, eo, dest_tiles, w_cols = _layer(
        x, mem, norm_mix[0], w_in[0], conv_w[0], g_conv_out[0], g_fft_out[0], w_out[0], norm_xattn[0],
        norm_mem[0], w_q[0], w_k[0], w_v[0], w_o[0], norm_ffn[0], w_router[0], b_router[0],
        w_gate_up[0], b_gate_up[0], w_down[0], b_down[0], tables)
    out = _combine(dest_tiles, eo, x2, w_cols, norm_final.reshape(1, -1))
    return out.reshape(bsz, seq, d)
```

```python
import functools
import math

import numpy as np
import jax
import jax.numpy as jnp
from jax import lax
from jax.experimental import pallas as pl
from jax.experimental.pallas import tpu as pltpu

F32 = jnp.float32
BF16 = jnp.bfloat16
I32 = jnp.int32

D_MODEL = 1024
CONV_WIDTH = 512
FFT_WIDTH = 512
GROUP_DIM = 64
IN_PROJ_WIDTH = 3 * CONV_WIDTH + FFT_WIDTH
MEM_LEN = 256
XATTN_HEADS = 4
XATTN_HEAD_DIM = D_MODEL // XATTN_HEADS
N_EXPERTS = 32
TOP_K = 4
D_EXPERT = D_MODEL
SWIGLU_LIMIT = 7.0
SWIGLU_ALPHA = 1.702
EPS = 1e-5

FFT_N1 = 64
FFT_N2 = 128

TOK_TILE = 512
ROW_BLOCK = 256
ROUTE_TILE = 256
FFT1_LANES = 8192
FFT2_K1 = 8
BF16_SUBLANES = 16
TILE_SUBLANES = 8
LANES = 128
assert D_MODEL == TILE_SUBLANES * LANES
PLAN_TOK = 2048
DMA_UNROLL = 16
VMEM_LIMIT = 56 * 1024 * 1024


def _rms(x, g):
    return x * lax.rsqrt(jnp.mean(x * x, axis=-1, keepdims=True) + EPS) * g


def _dot(a, b):
    return jnp.dot(a, b, preferred_element_type=F32)


def _dot_nt(a, b):
    return lax.dot_general(a, b, (((1,), (1,)), ((), ())), preferred_element_type=F32)


def _load_token_tiles(ref, rows, base=0):
    return jnp.concatenate(
        [ref[pl.ds(base + s, rows, stride=TILE_SUBLANES), :] for s in range(TILE_SUBLANES)], axis=-1)


def _store_token_tiles(ref, val, rows):
    for s in range(TILE_SUBLANES):
        ref[pl.ds(s, rows, stride=TILE_SUBLANES), :] = val[:, s * LANES:(s + 1) * LANES]


def _dft_tables(seq):
    assert seq == FFT_N1 * FFT_N2
    c = np.arange(GROUP_DIM)
    ang = 2.0 * np.pi * ((c[:, None] * c[None, :]) % GROUP_DIM) / GROUP_DIM
    groups = FFT_WIDTH // GROUP_DIM
    eye = np.eye(groups)
    cs = np.concatenate([np.kron(eye, np.cos(ang)), np.kron(eye, np.sin(ang))], axis=1) / math.sqrt(GROUP_DIM)
    n1 = np.arange(FFT_N1)
    a1 = 2.0 * np.pi * ((n1[:, None] * n1[None, :]) % FFT_N1) / FFT_N1
    c1, s1 = np.cos(a1), np.sin(a1)
    m1 = np.block([[c1, -s1], [s1, c1]]) / math.sqrt(FFT_N1)
    k1 = np.arange(FFT_N1)[:, None, None]
    k2 = np.arange(FFT_N2)[None, :, None]
    n2 = np.arange(FFT_N2)[None, None, :]
    a2 = 2.0 * np.pi * ((n2 * (k1 + FFT_N1 * k2)) % seq) / seq
    m2 = np.concatenate([np.cos(a2), -np.sin(a2)], axis=2) / math.sqrt(FFT_N2)
    return (jnp.asarray(cs, F32).astype(BF16), jnp.asarray(m1, F32).astype(BF16),
            jnp.asarray(m2, F32).astype(BF16))


def _memkv_kernel(mem_ref, g_ref, wk_ref, wv_ref, k_ref, v_ref):
    m = _rms(mem_ref[0], g_ref[...]).astype(BF16)
    k_ref[0] = _dot(m, wk_ref[...]).astype(BF16)
    v_ref[0] = _dot(m, wv_ref[...]).astype(BF16)


def _memkv(mem, g, wk, wv):
    bsz = mem.shape[0]
    full = lambda shape: pl.BlockSpec(shape, lambda b: (0,) * len(shape))
    per_b = pl.BlockSpec((1, MEM_LEN, D_MODEL), lambda b: (b, 0, 0))
    return pl.pallas_call(
        _memkv_kernel,
        grid=(bsz,),
        in_specs=[per_b, full((1, D_MODEL)), full((D_MODEL, D_MODEL)), full((D_MODEL, D_MODEL))],
        out_specs=[per_b, per_b],
        out_shape=[jax.ShapeDtypeStruct((bsz, MEM_LEN, D_MODEL), BF16)] * 2,
        compiler_params=pltpu.CompilerParams(dimension_semantics=("arbitrary",), vmem_limit_bytes=VMEM_LIMIT),
        name="memkv",
    )(mem, g, wk, wv)


def _inproj_kernel(x_ref, g_ref, win_ref, cs_ref, b_ref, cv_ref, a_ref, s_ref):
    h = _rms(x_ref[...], g_ref[...]).astype(BF16)
    z = _dot(h, win_ref[...])
    b_ref[...] = z[:, :CONV_WIDTH].astype(BF16)
    cv_ref[...] = (z[:, CONV_WIDTH:2 * CONV_WIDTH] * z[:, 2 * CONV_WIDTH:3 * CONV_WIDTH]).astype(BF16)
    u = z[:, 3 * CONV_WIDTH:].astype(BF16)
    ab = _dot(u, cs_ref[...])
    a_ref[...] = ab[:, :FFT_WIDTH].astype(BF16)
    s_ref[...] = ab[:, FFT_WIDTH:].astype(BF16)


def _inproj(x2d, g, w_in, cs):
    t = x2d.shape[0]
    tile = pl.BlockSpec((TOK_TILE, D_MODEL), lambda i: (i, 0))
    half = pl.BlockSpec((TOK_TILE, CONV_WIDTH), lambda i: (i, 0))
    full = lambda shape: pl.BlockSpec(shape, lambda i: (0,) * len(shape))
    return pl.pallas_call(
        _inproj_kernel,
        grid=(t // TOK_TILE,),
        in_specs=[tile, full((1, D_MODEL)), full((D_MODEL, IN_PROJ_WIDTH)), full((FFT_WIDTH, 2 * FFT_WIDTH))],
        out_specs=[half] * 4,
        out_shape=[jax.ShapeDtypeStruct((t, CONV_WIDTH), BF16)] * 4,
        compiler_params=pltpu.CompilerParams(dimension_semantics=("parallel",), vmem_limit_bytes=VMEM_LIMIT),
        name="inproj",
    )(x2d, g, w_in, cs)


def _fft1_kernel(a_ref, s_ref, m1_ref, g_ref):
    x = jnp.concatenate([a_ref[0], s_ref[0]], axis=0)
    g_ref[0] = _dot(m1_ref[...], x).astype(BF16)


def _fft1(a3, s3, m1):
    bsz, _, cols = a3.shape
    blk = pl.BlockSpec((1, FFT_N1, FFT1_LANES), lambda b, j: (b, 0, j))
    return pl.pallas_call(
        _fft1_kernel,
        grid=(bsz, cols // FFT1_LANES),
        in_specs=[blk, blk, pl.BlockSpec((2 * FFT_N1, 2 * FFT_N1), lambda b, j: (0, 0))],
        out_specs=pl.BlockSpec((1, 2 * FFT_N1, FFT1_LANES), lambda b, j: (b, 0, j)),
        out_shape=jax.ShapeDtypeStruct((bsz, 2 * FFT_N1, cols), BF16),
        compiler_params=pltpu.CompilerParams(dimension_semantics=("parallel", "parallel"),
                                             vmem_limit_bytes=VMEM_LIMIT),
        name="fft1",
    )(a3, s3, m1)


def _fft2_kernel(g_ref, m2_ref, y_ref):
    for j in range(FFT2_K1):
        x = jnp.concatenate([g_ref[0, 0, j], g_ref[0, 1, j]], axis=0)
        y_ref[0, :, j * FFT_WIDTH:(j + 1) * FFT_WIDTH] = _dot(m2_ref[j], x).astype(BF16)


def _fft2(g5, m2):
    bsz = g5.shape[0]
    return pl.pallas_call(
        _fft2_kernel,
        grid=(bsz, FFT_N1 // FFT2_K1),
        in_specs=[pl.BlockSpec((1, 2, FFT2_K1, FFT_N2, FFT_WIDTH), lambda b, j: (b, 0, j, 0, 0)),
                  pl.BlockSpec((FFT2_K1, FFT_N2, 2 * FFT_N2), lambda b, j: (j, 0, 0))],
        out_specs=pl.BlockSpec((1, FFT_N2, FFT2_K1 * FFT_WIDTH), lambda b, j: (b, 0, j)),
        out_shape=jax.ShapeDtypeStruct((bsz, FFT_N2, FFT_N1 * FFT_WIDTH), BF16),
        compiler_params=pltpu.CompilerParams(dimension_semantics=("parallel", "parallel"),
                                             vmem_limit_bytes=VMEM_LIMIT),
        name="fft2",
    )(g5, m2)


def _post_kernel(x_ref, bg_ref, cv_ref, cvp_ref, cvn_ref, yf_ref, convw_ref, gc_ref, gf_ref,
                 wot_ref, wob_ref, nx_ref, wq_ref, k_ref, v_ref, wo_ref, nf_ref,
                 wrh_ref, wrl_ref, br_ref, tri_ref,
                 x2_ref, h3_ref, idx_ref, w_ref, rank_ref, cnt_ref, carry_ref):
    b = pl.program_id(0)
    i = pl.program_id(1)
    last = pl.num_programs(1) - 1

    @pl.when(jnp.logical_and(b == 0, i == 0))
    def _():
        carry_ref[...] = jnp.zeros_like(carry_ref)

    cv = cv_ref[0].astype(F32)
    prev_row = jnp.where(i > 0, cvp_ref[0].astype(F32)[BF16_SUBLANES - 1:BF16_SUBLANES, :], 0.0)
    next_row = jnp.where(i < last, cvn_ref[0].astype(F32)[0:1, :], 0.0)
    rows = lax.broadcasted_iota(I32, cv.shape, 0)
    cvm1 = jnp.where(rows == 0, prev_row, pltpu.roll(cv, 1, axis=0))
    cvp1 = jnp.where(rows == TOK_TILE - 1, next_row, pltpu.roll(cv, TOK_TILE - 1, axis=0))
    cw = convw_ref[...]
    y_conv = bg_ref[0].astype(F32) * (cw[0:1] * cvm1 + cw[1:2] * cv + cw[2:3] * cvp1)
    yc_n = _rms(y_conv, gc_ref[...]).astype(BF16)
    yf_n = _rms(yf_ref[0].astype(F32), gf_ref[...]).astype(BF16)
    x1 = x_ref[0] + _dot(yc_n, wot_ref[...]) + _dot(yf_n, wob_ref[...])

    h2 = _rms(x1, nx_ref[...]).astype(BF16)
    q = _dot(h2, wq_ref[...]).astype(BF16)
    heads = []
    for hd in range(XATTN_HEADS):
        sl = slice(hd * XATTN_HEAD_DIM, (hd + 1) * XATTN_HEAD_DIM)
        s = _dot_nt(q[:, sl], k_ref[0, :, sl]) * (XATTN_HEAD_DIM ** -0.5)
        e = jnp.exp(s - jnp.max(s, axis=-1, keepdims=True))
        p = e * (1.0 / jnp.sum(e, axis=-1, keepdims=True))
        heads.append(_dot(p.astype(BF16), v_ref[0, :, sl]).astype(BF16))
    x2 = x1 + _dot(jnp.concatenate(heads, axis=-1), wo_ref[...])
    x2_ref[0] = x2

    h3 = _rms(x2, nf_ref[...])
    _store_token_tiles(h3_ref, h3, TOK_TILE)
    h3h = h3.astype(BF16)
    h3l = (h3 - h3h.astype(F32)).astype(BF16)
    logits = (_dot_nt(wrh_ref[...], h3h) + _dot_nt(wrh_ref[...], h3l) + _dot_nt(wrl_ref[...], h3h)
              + br_ref[:, 0:1])
    eidx = lax.broadcasted_iota(I32, logits.shape, 0)
    work = logits
    vals, idxs = [], []
    for _ in range(TOP_K):
        m = jnp.max(work, axis=0, keepdims=True)
        ik = jnp.min(jnp.where(work == m, eidx, N_EXPERTS), axis=0, keepdims=True)
        vals.append(m)
        idxs.append(ik)
        work = jnp.where(eidx == ik, -jnp.inf, work)
    ex = [jnp.exp(v - vals[0]) for v in vals]
    inv_den = 1.0 / (ex[0] + ex[1] + ex[2] + ex[3])
    idx_ref[...] = jnp.concatenate(idxs, axis=0)
    w_ref[...] = jnp.concatenate([e * inv_den for e in ex], axis=0)

    sel = jnp.zeros(logits.shape, F32)
    for ik in idxs:
        sel = sel + jnp.where(eidx == ik, 1.0, 0.0)
    full = carry_ref[:, 0:1] + _dot(sel.astype(BF16), tri_ref[...])
    rank_ref[...] = jnp.concatenate(
        [jnp.sum(jnp.where(eidx == ik, full, 0.0), axis=0, keepdims=True) for ik in idxs], axis=0).astype(I32)
    carry_ref[...] = carry_ref[...] + jnp.sum(sel, axis=1, keepdims=True)
    cnt_ref[...] = carry_ref[...]


def _post(x3, bg, cv, yf, conv_w, gc, gf, wo_top, wo_bot, nx, wq, kmem, vmem_, wo, nf, wrh, wrl, br, tri):
    bsz, seq, _ = x3.shape
    nt = seq // TOK_TILE
    t = bsz * seq
    halo_per_tile = TOK_TILE // BF16_SUBLANES
    n_halo = seq // BF16_SUBLANES
    full = lambda shape: pl.BlockSpec(shape, lambda b, i: (0,) * len(shape))
    tile_d = pl.BlockSpec((1, TOK_TILE, D_MODEL), lambda b, i: (b, i, 0))
    tile_h = pl.BlockSpec((1, TOK_TILE, CONV_WIDTH), lambda b, i: (b, i, 0))
    halo_prev = pl.BlockSpec((1, BF16_SUBLANES, CONV_WIDTH),
                             lambda b, i: (b, jnp.maximum(i * halo_per_tile - 1, 0), 0))
    halo_next = pl.BlockSpec((1, BF16_SUBLANES, CONV_WIDTH),
                             lambda b, i: (b, jnp.minimum((i + 1) * halo_per_tile, n_halo - 1), 0))
    mem_blk = pl.BlockSpec((1, MEM_LEN, D_MODEL), lambda b, i: (b, 0, 0))
    tok4 = pl.BlockSpec((TOP_K, TOK_TILE), lambda b, i: (0, b * nt + i))
    tok_tiles = pl.BlockSpec((TOK_TILE * TILE_SUBLANES, LANES), lambda b, i: (b * nt + i, 0))
    return pl.pallas_call(
        _post_kernel,
        grid=(bsz, nt),
        in_specs=[tile_d, tile_h, tile_h, halo_prev, halo_next, tile_h,
                  full((3, CONV_WIDTH)), full((1, CONV_WIDTH)), full((1, FFT_WIDTH)),
                  full((CONV_WIDTH, D_MODEL)), full((FFT_WIDTH, D_MODEL)), full((1, D_MODEL)),
                  full((D_MODEL, D_MODEL)), mem_blk, mem_blk, full((D_MODEL, D_MODEL)), full((1, D_MODEL)),
                  full((N_EXPERTS, D_MODEL)), full((N_EXPERTS, D_MODEL)), full((N_EXPERTS, 128)),
                  full((TOK_TILE, TOK_TILE))],
        out_specs=[tile_d, tok_tiles, tok4, tok4, tok4, full((N_EXPERTS, 128))],
        out_shape=[jax.ShapeDtypeStruct((bsz, seq, D_MODEL), F32),
                   jax.ShapeDtypeStruct((t * TILE_SUBLANES, LANES), F32),
                   jax.ShapeDtypeStruct((TOP_K, t), I32),
                   jax.ShapeDtypeStruct((TOP_K, t), F32),
                   jax.ShapeDtypeStruct((TOP_K, t), I32),
                   jax.ShapeDtypeStruct((N_EXPERTS, 128), F32)],
        scratch_shapes=[pltpu.VMEM((N_EXPERTS, 128), F32)],
        compiler_params=pltpu.CompilerParams(dimension_semantics=("arbitrary", "arbitrary"),
                                             vmem_limit_bytes=VMEM_LIMIT),
        name="post",
    )(x3, bg, cv, cv, cv, yf, conv_w, gc, gf, wo_top, wo_bot, nx, wq, kmem, vmem_, wo, nf, wrh, wrl, br, tri)


def _plan_kernel(cnt_ref, idx_ref, rank_ref, dest_ref, be_ref, nu_ref, last_ref, start_ref):
    n_blocks = be_ref.shape[0]

    @pl.when(pl.program_id(0) == 0)
    def _():
        def per_expert(e, carry):
            row0, blk0 = carry
            nb = lax.shift_right_logical(cnt_ref[e] + (ROW_BLOCK - 1), ROW_BLOCK.bit_length() - 1)
            start_ref[e] = row0
            row1 = row0 + nb * ROW_BLOCK
            last_ref[e] = jnp.where(nb > 0, row1 - ROW_BLOCK, -1)

            def fill(j, c):
                be_ref[blk0 + j] = e
                return c

            lax.fori_loop(0, nb, fill, 0)
            return row1, blk0 + nb

        _, n_used = lax.fori_loop(0, N_EXPERTS, per_expert, (jnp.int32(0), jnp.int32(0)))
        nu_ref[0] = n_used
        tail_e = be_ref[n_used - 1]

        def tail(j, c):
            be_ref[j] = tail_e
            return c

        lax.fori_loop(n_used, n_blocks, tail, 0)

    idx = idx_ref[...]
    dest = rank_ref[...]
    for e in range(N_EXPERTS):
        dest = dest + jnp.where(idx == e, start_ref[e], 0)
    for j in range(PLAN_TOK // ROUTE_TILE):
        dest_ref[j] = dest[:, j * ROUTE_TILE:(j + 1) * ROUTE_TILE]


def _plan(counts, idx_t, rank_t, n_blocks):
    t = idx_t.shape[1]
    smem = pl.BlockSpec(memory_space=pltpu.SMEM)
    tok = pl.BlockSpec((TOP_K, PLAN_TOK), lambda i, cnt: (0, i))
    grid_spec = pltpu.PrefetchScalarGridSpec(
        num_scalar_prefetch=1,
        grid=(t // PLAN_TOK,),
        in_specs=[tok, tok],
        out_specs=[pl.BlockSpec((PLAN_TOK // ROUTE_TILE, TOP_K, ROUTE_TILE), lambda i, cnt: (i, 0, 0)),
                   smem, smem, smem],
        scratch_shapes=[pltpu.SMEM((N_EXPERTS,), I32)],
    )
    return pl.pallas_call(
        _plan_kernel,
        grid_spec=grid_spec,
        out_shape=[jax.ShapeDtypeStruct((t // ROUTE_TILE, TOP_K, ROUTE_TILE), I32),
                   jax.ShapeDtypeStruct((n_blocks,), I32),
                   jax.ShapeDtypeStruct((1,), I32),
                   jax.ShapeDtypeStruct((N_EXPERTS,), I32)],
        compiler_params=pltpu.CompilerParams(dimension_semantics=("arbitrary",), vmem_limit_bytes=VMEM_LIMIT),
        name="plan",
    )(counts, idx_t, rank_t)


def _tile_rows(ref, row, n_rows=1):
    return ref.at[pl.ds(pl.multiple_of(row * TILE_SUBLANES, TILE_SUBLANES), n_rows * TILE_SUBLANES), :]


def _row_copies(issue_one, drain_one):
    for k in range(TOP_K):
        def issue(grp, c, k=k):
            for u in range(DMA_UNROLL):
                issue_one(k, grp * DMA_UNROLL + u).start(priority=u % 2)
            return c

        lax.fori_loop(0, ROUTE_TILE // DMA_UNROLL, issue, 0)

    def drain(grp, c):
        for _ in range(DMA_UNROLL):
            drain_one().wait()
        return c

    lax.fori_loop(0, TOP_K * ROUTE_TILE // DMA_UNROLL, drain, 0)


def _dispatch_kernel(last_ref, nu_ref, dest_ref, h3_ref, xs_ref, zero_ref, sem, zsem):
    @pl.when(pl.program_id(0) == 0)
    def _():
        zero_ref[...] = jnp.zeros_like(zero_ref)
        zero_copy = lambda row: pltpu.make_async_copy(zero_ref, _tile_rows(xs_ref, row, ROW_BLOCK), zsem)
        for e in range(N_EXPERTS):
            @pl.when(last_ref[e] >= 0)
            def _():
                zero_copy(last_ref[e]).start()
        for e in range(N_EXPERTS):
            @pl.when(last_ref[e] >= 0)
            def _():
                zero_copy(last_ref[e]).wait()
        n_blocks = xs_ref.shape[0] // (ROW_BLOCK * TILE_SUBLANES)

        def start_tail(j, c):
            zero_copy(j * ROW_BLOCK).start()
            return c

        def wait_tail(j, c):
            zero_copy(j * ROW_BLOCK).wait()
            return c

        lax.fori_loop(nu_ref[0], n_blocks, start_tail, 0)
        lax.fori_loop(nu_ref[0], n_blocks, wait_tail, 0)

    _row_copies(
        lambda k, r: pltpu.make_async_copy(_tile_rows(h3_ref, r), _tile_rows(xs_ref, dest_ref[k, r]), sem),
        lambda: pltpu.make_async_copy(_tile_rows(h3_ref, 0), _tile_rows(xs_ref, 0), sem))


def _dispatch(last_rows, n_used, dest_tiles, h3t, n_rows):
    t = h3t.shape[0] // TILE_SUBLANES
    grid_spec = pltpu.PrefetchScalarGridSpec(
        num_scalar_prefetch=2,
        grid=(t // ROUTE_TILE,),
        in_specs=[pl.BlockSpec((None, TOP_K, ROUTE_TILE), lambda i, last, nu: (i, 0, 0), memory_space=pltpu.SMEM),
                  pl.BlockSpec((ROUTE_TILE * TILE_SUBLANES, LANES), lambda i, last, nu: (i, 0))],
        out_specs=pl.BlockSpec(memory_space=pl.ANY),
        scratch_shapes=[pltpu.VMEM((ROW_BLOCK * TILE_SUBLANES, LANES), F32),
                        pltpu.SemaphoreType.DMA, pltpu.SemaphoreType.DMA],
    )
    return pl.pallas_call(
        _dispatch_kernel,
        grid_spec=grid_spec,
        out_shape=jax.ShapeDtypeStruct((n_rows * TILE_SUBLANES, LANES), F32),
        compiler_params=pltpu.CompilerParams(dimension_semantics=("arbitrary",), vmem_limit_bytes=VMEM_LIMIT),
        name="dispatch",
    )(last_rows, n_used, dest_tiles, h3t)


def _expert_kernel(be_ref, nu_ref, xs_ref, wgu_ref, bgu_ref, wd_ref, bd_ref, o_ref, wgu_bf, wd_bf):
    j = pl.program_id(0)
    used = j < nu_ref[0]
    first_of_expert = jnp.logical_or(j == 0, be_ref[j] != be_ref[jnp.maximum(j - 1, 0)])

    @pl.when(jnp.logical_and(used, first_of_expert))
    def _():
        wgu_bf[...] = wgu_ref[0].astype(BF16)
        wd_bf[...] = wd_ref[0].astype(BF16)

    @pl.when(jnp.logical_not(used))
    def _():
        o_ref[...] = jnp.zeros_like(o_ref)

    @pl.when(used)
    def _():
        x = _load_token_tiles(xs_ref, ROW_BLOCK).astype(BF16)
        gu = _dot(x, wgu_bf[...]) + bgu_ref[0]
        gate = jnp.minimum(gu[:, :D_EXPERT], SWIGLU_LIMIT)
        up = jnp.clip(gu[:, D_EXPERT:], -SWIGLU_LIMIT, SWIGLU_LIMIT)
        glu = gate * (1.0 / (1.0 + jnp.exp(-SWIGLU_ALPHA * gate)))
        h = ((up + 1.0) * glu).astype(BF16)
        _store_token_tiles(o_ref, _dot(h, wd_bf[...]) + bd_ref[0], ROW_BLOCK)


def _experts(block_e, n_used, xs, wgu, bgu, wd, bd):
    blk_rows = ROW_BLOCK * TILE_SUBLANES
    n_blocks = xs.shape[0] // blk_rows
    exp_map = lambda j, be, nu: (be[j], 0, 0)
    grid_spec = pltpu.PrefetchScalarGridSpec(
        num_scalar_prefetch=2,
        grid=(n_blocks,),
        in_specs=[pl.BlockSpec((blk_rows, LANES), lambda j, be, nu: (jnp.minimum(j, nu[0] - 1), 0)),
                  pl.BlockSpec((1, D_MODEL, 2 * D_EXPERT), exp_map),
                  pl.BlockSpec((1, 1, 2 * D_EXPERT), exp_map),
                  pl.BlockSpec((1, D_EXPERT, D_MODEL), exp_map),
                  pl.BlockSpec((1, 1, D_MODEL), exp_map)],
        out_specs=pl.BlockSpec((blk_rows, LANES), lambda j, be, nu: (j, 0)),
        scratch_shapes=[pltpu.VMEM((D_MODEL, 2 * D_EXPERT), BF16), pltpu.VMEM((D_EXPERT, D_MODEL), BF16)],
    )
    return pl.pallas_call(
        _expert_kernel,
        grid_spec=grid_spec,
        out_shape=jax.ShapeDtypeStruct(xs.shape, F32),
        compiler_params=pltpu.CompilerParams(dimension_semantics=("arbitrary",), vmem_limit_bytes=VMEM_LIMIT),
        name="experts",
    )(block_e, n_used, xs, wgu, bgu, wd, bd)


def _combine_kernel(dest_ref, eo_ref, x2_ref, w_ref, g_ref, out_ref, rows_ref, sem):
    _row_copies(
        lambda k, r: pltpu.make_async_copy(_tile_rows(eo_ref, dest_ref[k, r]),
                                           _tile_rows(rows_ref, k * ROUTE_TILE + r), sem),
        lambda: pltpu.make_async_copy(_tile_rows(eo_ref, 0), _tile_rows(rows_ref, 0), sem))

    y = x2_ref[...]
    for k in range(TOP_K):
        y = y + w_ref[:, k:k + 1] * _load_token_tiles(rows_ref, ROUTE_TILE, k * ROUTE_TILE * TILE_SUBLANES)
    out_ref[...] = _rms(y, g_ref[...])


def _combine(dest_tiles, eo, x2, w_cols, g):
    t = x2.shape[0]
    return pl.pallas_call(
        _combine_kernel,
        grid=(t // ROUTE_TILE,),
        in_specs=[pl.BlockSpec((None, TOP_K, ROUTE_TILE), lambda i: (i, 0, 0), memory_space=pltpu.SMEM),
                  pl.BlockSpec(memory_space=pl.ANY),
                  pl.BlockSpec((ROUTE_TILE, D_MODEL), lambda i: (i, 0)),
                  pl.BlockSpec((ROUTE_TILE, TOP_K), lambda i: (i, 0)),
                  pl.BlockSpec((1, D_MODEL), lambda i: (0, 0))],
        out_specs=pl.BlockSpec((ROUTE_TILE, D_MODEL), lambda i: (i, 0)),
        out_shape=jax.ShapeDtypeStruct((t, D_MODEL), F32),
        scratch_shapes=[pltpu.VMEM((TOP_K * ROUTE_TILE * TILE_SUBLANES, LANES), F32), pltpu.SemaphoreType.DMA],
        compiler_params=pltpu.CompilerParams(dimension_semantics=("arbitrary",), vmem_limit_bytes=VMEM_LIMIT),
        name="combine",
    )(dest_tiles, eo, x2, w_cols, g)


def _layer(x, mem, norm_mix, w_in, conv_w, g_conv_out, g_fft_out, w_out, norm_xattn, norm_mem,
           w_q, w_k, w_v, w_o, norm_ffn, w_router, b_router, w_gate_up, b_gate_up, w_down, b_down, tables):
    bsz, seq, d = x.shape
    t = bsz * seq
    cs, m1, m2 = tables
    row = lambda v: v.reshape(1, -1)

    kmem, vmem_ = _memkv(mem, row(norm_mem), w_k.astype(BF16), w_v.astype(BF16))

    bg, cv, a, s = _inproj(x.reshape(t, d), row(norm_mix), w_in.astype(BF16), cs)
    cols = FFT_N2 * FFT_WIDTH
    g = _fft1(a.reshape(bsz, FFT_N1, cols), s.reshape(bsz, FFT_N1, cols), m1)
    yf = _fft2(g.reshape(bsz, 2, FFT_N1, FFT_N2, FFT_WIDTH), m2).reshape(bsz, seq, FFT_WIDTH)

    w_out_b = w_out.astype(BF16)
    wr_t = w_router.T
    wr_hi = wr_t.astype(BF16)
    wr_lo = (wr_t - wr_hi.astype(F32)).astype(BF16)
    tri = (jnp.arange(TOK_TILE)[:, None] < jnp.arange(TOK_TILE)[None, :]).astype(BF16)
    x2, h3, idx_t, w_t, rank_t, cnt = _post(
        x, bg.reshape(bsz, seq, CONV_WIDTH), cv.reshape(bsz, seq, CONV_WIDTH), yf,
        conv_w, row(g_conv_out), row(g_fft_out), w_out_b[:CONV_WIDTH], w_out_b[CONV_WIDTH:],
        row(norm_xattn), w_q.astype(BF16), kmem, vmem_, w_o.astype(BF16), row(norm_ffn),
        wr_hi, wr_lo, jnp.broadcast_to(b_router[:, None], (N_EXPERTS, 128)), tri)

    n_blocks = (t * TOP_K) // ROW_BLOCK + N_EXPERTS
    dest_tiles, block_e, n_used, last_rows = _plan(cnt[:, 0].astype(I32), idx_t, rank_t, n_blocks)
    xs = _dispatch(last_rows, n_used, dest_tiles, h3, n_blocks * ROW_BLOCK)
    eo = _experts(block_e, n_used, xs, w_gate_up, b_gate_up[:, None, :], w_down, b_down[:, None, :])
    return x2.reshape(t, d), eo, dest_tiles, w_t.T


def kernel(x, mem, norm_mix, w_in, conv_w, g_conv_out, g_fft_out, w_out, norm_xattn, norm_mem, w_q, w_k, w_v, w_o,
           norm_ffn, w_router, b_router, w_gate_up, b_gate_up, w_down, b_down, norm_final):
    bsz, seq, d = x.shape
    depth = norm_mix.shape[0]
    assert depth == 1, "final norm is fused into the combine step of the single layer"
    tables = _dft_tables(seq)
    x2, eo, dest_tiles, w_cols = _layer(
        x, mem, norm_mix[0], w_in[0], conv_w[0], g_conv_out[0], g_fft_out[0], w_out[0], norm_xattn[0],
        norm_mem[0], w_q[0], w_k[0], w_v[0], w_o[0], norm_ffn[0], w_router[0], b_router[0],
        w_gate_up[0], b_gate_up[0], w_down[0], b_down[0], tables)
    out = _combine(dest_tiles, eo, x2, w_cols, norm_final.reshape(1, -1))
    return out.reshape(bsz, seq, d)
```

```python
import functools
import math

import numpy as np
import jax
import jax.numpy as jnp
from jax import lax
from jax.experimental import pallas as pl
from jax.experimental.pallas import tpu as pltpu

F32 = jnp.float32
BF16 = jnp.bfloat16
I32 = jnp.int32

D_MODEL = 1024
CONV_WIDTH = 512
FFT_WIDTH = 512
GROUP_DIM = 64
IN_PROJ_WIDTH = 3 * CONV_WIDTH + FFT_WIDTH
MEM_LEN = 256
XATTN_HEADS = 4
XATTN_HEAD_DIM = D_MODEL // XATTN_HEADS
N_EXPERTS = 32
TOP_K = 4
D_EXPERT = D_MODEL
SWIGLU_LIMIT = 7.0
SWIGLU_ALPHA = 1.702
EPS = 1e-5

FFT_N1 = 64
FFT_N2 = 128

TOK_TILE = 512
ROW_BLOCK = 256
ROUTE_TILE = 256
FFT1_LANES = 8192
FFT2_K1 = 8
BF16_SUBLANES = 16
TILE_SUBLANES = 8
LANES = 128
assert D_MODEL == TILE_SUBLANES * LANES
PLAN_TOK = 2048
DMA_UNROLL = 16
VMEM_LIMIT = 56 * 1024 * 1024


def _rms(x, g):
    return x * lax.rsqrt(jnp.mean(x * x, axis=-1, keepdims=True) + EPS) * g


def _dot(a, b):
    return jnp.dot(a, b, preferred_element_type=F32)


def _dot_nt(a, b):
    return lax.dot_general(a, b, (((1,), (1,)), ((), ())), preferred_element_type=F32)


def _load_token_tiles(ref, rows, base=0):
    return jnp.concatenate(
        [ref[pl.ds(base + s, rows, stride=TILE_SUBLANES), :] for s in range(TILE_SUBLANES)], axis=-1)


def _store_token_tiles(ref, val, rows):
    for s in range(TILE_SUBLANES):
        ref[pl.ds(s, rows, stride=TILE_SUBLANES), :] = val[:, s * LANES:(s + 1) * LANES]


def _dft_tables(seq):
    assert seq == FFT_N1 * FFT_N2
    c = np.arange(GROUP_DIM)
    ang = 2.0 * np.pi * ((c[:, None] * c[None, :]) % GROUP_DIM) / GROUP_DIM
    groups = FFT_WIDTH // GROUP_DIM
    eye = np.eye(groups)
    cs = np.concatenate([np.kron(eye, np.cos(ang)), np.kron(eye, np.sin(ang))], axis=1) / math.sqrt(GROUP_DIM)
    n1 = np.arange(FFT_N1)
    a1 = 2.0 * np.pi * ((n1[:, None] * n1[None, :]) % FFT_N1) / FFT_N1
    c1, s1 = np.cos(a1), np.sin(a1)
    m1 = np.block([[c1, -s1], [s1, c1]]) / math.sqrt(FFT_N1)
    k1 = np.arange(FFT_N1)[:, None, None]
    k2 = np.arange(FFT_N2)[None, :, None]
    n2 = np.arange(FFT_N2)[None, None, :]
    a2 = 2.0 * np.pi * ((n2 * (k1 + FFT_N1 * k2)) % seq) / seq
    m2 = np.concatenate([np.cos(a2), -np.sin(a2)], axis=2) / math.sqrt(FFT_N2)
    return (jnp.asarray(cs, F32).astype(BF16), jnp.asarray(m1, F32).astype(BF16),
            jnp.asarray(m2, F32).astype(BF16))


def _memkv_kernel(mem_ref, g_ref, wk_ref, wv_ref, k_ref, v_ref):
    m = _rms(mem_ref[0], g_ref[...]).astype(BF16)
    k_ref[0] = _dot(m, wk_ref[...]).astype(BF16)
    v_ref[0] = _dot(m, wv_ref[...]).astype(BF16)


def _memkv(mem, g, wk, wv):
    bsz = mem.shape[0]
    full = lambda shape: pl.BlockSpec(shape, lambda b: (0,) * len(shape))
    per_b = pl.BlockSpec((1, MEM_LEN, D_MODEL), lambda b: (b, 0, 0))
    return pl.pallas_call(
        _memkv_kernel,
        grid=(bsz,),
        in_specs=[per_b, full((1, D_MODEL)), full((D_MODEL, D_MODEL)), full((D_MODEL, D_MODEL))],
        out_specs=[per_b, per_b],
        out_shape=[jax.ShapeDtypeStruct((bsz, MEM_LEN, D_MODEL), BF16)] * 2,
        compiler_params=pltpu.CompilerParams(dimension_semantics=("arbitrary",), vmem_limit_bytes=VMEM_LIMIT),
        name="memkv",
    )(mem, g, wk, wv)


def _inproj_kernel(x_ref, g_ref, win_ref, cs_ref, b_ref, cv_ref, a_ref, s_ref):
    h = _rms(x_ref[...], g_ref[...]).astype(BF16)
    z = _dot(h, win_ref[...])
    b_ref[...] = z[:, :CONV_WIDTH].astype(BF16)
    cv_ref[...] = (z[:, CONV_WIDTH:2 * CONV_WIDTH] * z[:, 2 * CONV_WIDTH:3 * CONV_WIDTH]).astype(BF16)
    u = z[:, 3 * CONV_WIDTH:].astype(BF16)
    ab = _dot(u, cs_ref[...])
    a_ref[...] = ab[:, :FFT_WIDTH].astype(BF16)
    s_ref[...] = ab[:, FFT_WIDTH:].astype(BF16)


def _inproj(x2d, g, w_in, cs):
    t = x2d.shape[0]
    tile = pl.BlockSpec((TOK_TILE, D_MODEL), lambda i: (i, 0))
    half = pl.BlockSpec((TOK_TILE, CONV_WIDTH), lambda i: (i, 0))
    full = lambda shape: pl.BlockSpec(shape, lambda i: (0,) * len(shape))
    return pl.pallas_call(
        _inproj_kernel,
        grid=(t // TOK_TILE,),
        in_specs=[tile, full((1, D_MODEL)), full((D_MODEL, IN_PROJ_WIDTH)), full((FFT_WIDTH, 2 * FFT_WIDTH))],
        out_specs=[half] * 4,
        out_shape=[jax.ShapeDtypeStruct((t, CONV_WIDTH), BF16)] * 4,
        compiler_params=pltpu.CompilerParams(dimension_semantics=("parallel",), vmem_limit_bytes=VMEM_LIMIT),
        name="inproj",
    )(x2d, g, w_in, cs)


def _fft1_kernel(a_ref, s_ref, m1_ref, g_ref):
    x = jnp.concatenate([a_ref[0], s_ref[0]], axis=0)
    g_ref[0] = _dot(m1_ref[...], x).astype(BF16)


def _fft1(a3, s3, m1):
    bsz, _, cols = a3.shape
    blk = pl.BlockSpec((1, FFT_N1, FFT1_LANES), lambda b, j: (b, 0, j))
    return pl.pallas_call(
        _fft1_kernel,
        grid=(bsz, cols // FFT1_LANES),
        in_specs=[blk, blk, pl.BlockSpec((2 * FFT_N1, 2 * FFT_N1), lambda b, j: (0, 0))],
        out_specs=pl.BlockSpec((1, 2 * FFT_N1, FFT1_LANES), lambda b, j: (b, 0, j)),
        out_shape=jax.ShapeDtypeStruct((bsz, 2 * FFT_N1, cols), BF16),
        compiler_params=pltpu.CompilerParams(dimension_semantics=("parallel", "parallel"),
                                             vmem_limit_bytes=VMEM_LIMIT),
        name="fft1",
    )(a3, s3, m1)


def _fft2_kernel(g_ref, m2_ref, y_ref):
    for j in range(FFT2_K1):
        x = jnp.concatenate([g_ref[0, 0, j], g_ref[0, 1, j]], axis=0)
        y_ref[0, :, j * FFT_WIDTH:(j + 1) * FFT_WIDTH] = _dot(m2_ref[j], x).astype(BF16)


def _fft2(g5, m2):
    bsz = g5.shape[0]
    return pl.pallas_call(
        _fft2_kernel,
        grid=(bsz, FFT_N1 // FFT2_K1),
        in_specs=[pl.BlockSpec((1, 2, FFT2_K1, FFT_N2, FFT_WIDTH), lambda b, j: (b, 0, j, 0, 0)),
                  pl.BlockSpec((FFT2_K1, FFT_N2, 2 * FFT_N2), lambda b, j: (j, 0, 0))],
        out_specs=pl.BlockSpec((1, FFT_N2, FFT2_K1 * FFT_WIDTH), lambda b, j: (b, 0, j)),
        out_shape=jax.ShapeDtypeStruct((bsz, FFT_N2, FFT_N1 * FFT_WIDTH), BF16),
        compiler_params=pltpu.CompilerParams(dimension_semantics=("parallel", "parallel"),
                                             vmem_limit_bytes=VMEM_LIMIT),
        name="fft2",
    )(g5, m2)


def _post_kernel(x_ref, bg_ref, cv_ref, cvp_ref, cvn_ref, yf_ref, convw_ref, gc_ref, gf_ref,
                 wot_ref, wob_ref, nx_ref, wq_ref, k_ref, v_ref, wo_ref, nf_ref,
                 wrh_ref, wrl_ref, br_ref, tri_ref,
                 x2_ref, h3_ref, idx_ref, w_ref, rank_ref, cnt_ref, carry_ref):
    b = pl.program_id(0)
    i = pl.program_id(1)
    last = pl.num_programs(1) - 1

    @pl.when(jnp.logical_and(b == 0, i == 0))
    def _():
        carry_ref[...] = jnp.zeros_like(carry_ref)

    cv = cv_ref[0].astype(F32)
    prev_row = jnp.where(i > 0, cvp_ref[0].astype(F32)[BF16_SUBLANES - 1:BF16_SUBLANES, :], 0.0)
    next_row = jnp.where(i < last, cvn_ref[0].astype(F32)[0:1, :], 0.0)
    rows = lax.broadcasted_iota(I32, cv.shape, 0)
    cvm1 = jnp.where(rows == 0, prev_row, pltpu.roll(cv, 1, axis=0))
    cvp1 = jnp.where(rows == TOK_TILE - 1, next_row, pltpu.roll(cv, TOK_TILE - 1, axis=0))
    cw = convw_ref[...]
    y_conv = bg_ref[0].astype(F32) * (cw[0:1] * cvm1 + cw[1:2] * cv + cw[2:3] * cvp1)
    yc_n = _rms(y_conv, gc_ref[...]).astype(BF16)
    yf_n = _rms(yf_ref[0].astype(F32), gf_ref[...]).astype(BF16)
    x1 = x_ref[0] + _dot(yc_n, wot_ref[...]) + _dot(yf_n, wob_ref[...])

    h2 = _rms(x1, nx_ref[...]).astype(BF16)
    q = _dot(h2, wq_ref[...]).astype(BF16)
    heads = []
    for hd in range(XATTN_HEADS):
        sl = slice(hd * XATTN_HEAD_DIM, (hd + 1) * XATTN_HEAD_DIM)
        s = _dot_nt(q[:, sl], k_ref[0, :, sl]) * (XATTN_HEAD_DIM ** -0.5)
        e = jnp.exp(s - jnp.max(s, axis=-1, keepdims=True))
        p = e * (1.0 / jnp.sum(e, axis=-1, keepdims=True))
        heads.append(_dot(p.astype(BF16), v_ref[0, :, sl]).astype(BF16))
    x2 = x1 + _dot(jnp.concatenate(heads, axis=-1), wo_ref[...])
    x2_ref[0] = x2

    h3 = _rms(x2, nf_ref[...])
    _store_token_tiles(h3_ref, h3, TOK_TILE)
    h3h = h3.astype(BF16)
    h3l = (h3 - h3h.astype(F32)).astype(BF16)
    logits = (_dot_nt(wrh_ref[...], h3h) + _dot_nt(wrh_ref[...], h3l) + _dot_nt(wrl_ref[...], h3h)
              + br_ref[:, 0:1])
    eidx = lax.broadcasted_iota(I32, logits.shape, 0)
    work = logits
    vals, idxs = [], []
    for _ in range(TOP_K):
        m = jnp.max(work, axis=0, keepdims=True)
        ik = jnp.min(jnp.where(work == m, eidx, N_EXPERTS), axis=0, keepdims=True)
        vals.append(m)
        idxs.append(ik)
        work = jnp.where(eidx == ik, -jnp.inf, work)
    ex = [jnp.exp(v - vals[0]) for v in vals]
    inv_den = 1.0 / (ex[0] + ex[1] + ex[2] + ex[3])
    idx_ref[...] = jnp.concatenate(idxs, axis=0)
    w_ref[...] = jnp.concatenate([e * inv_den for e in ex], axis=0)

    sel = jnp.zeros(logits.shape, F32)
    for ik in idxs:
        sel = sel + jnp.where(eidx == ik, 1.0, 0.0)
    full = carry_ref[:, 0:1] + _dot(sel.astype(BF16), tri_ref[...])
    rank_ref[...] = jnp.concatenate(
        [jnp.sum(jnp.where(eidx == ik, full, 0.0), axis=0, keepdims=True) for ik in idxs], axis=0).astype(I32)
    carry_ref[...] = carry_ref[...] + jnp.sum(sel, axis=1, keepdims=True)
    cnt_ref[...] = carry_ref[...]


def _post(x3, bg, cv, yf, conv_w, gc, gf, wo_top, wo_bot, nx, wq, kmem, vmem_, wo, nf, wrh, wrl, br, tri):
    bsz, seq, _ = x3.shape
    nt = seq // TOK_TILE
    t = bsz * seq
    halo_per_tile = TOK_TILE // BF16_SUBLANES
    n_halo = seq // BF16_SUBLANES
    full = lambda shape: pl.BlockSpec(shape, lambda b, i: (0,) * len(shape))
    tile_d = pl.BlockSpec((1, TOK_TILE, D_MODEL), lambda b, i: (b, i, 0))
    tile_h = pl.BlockSpec((1, TOK_TILE, CONV_WIDTH), lambda b, i: (b, i, 0))
    halo_prev = pl.BlockSpec((1, BF16_SUBLANES, CONV_WIDTH),
                             lambda b, i: (b, jnp.maximum(i * halo_per_tile - 1, 0), 0))
    halo_next = pl.BlockSpec((1, BF16_SUBLANES, CONV_WIDTH),
                             lambda b, i: (b, jnp.minimum((i + 1) * halo_per_tile, n_halo - 1), 0))
    mem_blk = pl.BlockSpec((1, MEM_LEN, D_MODEL), lambda b, i: (b, 0, 0))
    tok4 = pl.BlockSpec((TOP_K, TOK_TILE), lambda b, i: (0, b * nt + i))
    tok_tiles = pl.BlockSpec((TOK_TILE * TILE_SUBLANES, LANES), lambda b, i: (b * nt + i, 0))
    return pl.pallas_call(
        _post_kernel,
        grid=(bsz, nt),
        in_specs=[tile_d, tile_h, tile_h, halo_prev, halo_next, tile_h,
                  full((3, CONV_WIDTH)), full((1, CONV_WIDTH)), full((1, FFT_WIDTH)),
                  full((CONV_WIDTH, D_MODEL)), full((FFT_WIDTH, D_MODEL)), full((1, D_MODEL)),
                  full((D_MODEL, D_MODEL)), mem_blk, mem_blk, full((D_MODEL, D_MODEL)), full((1, D_MODEL)),
                  full((N_EXPERTS, D_MODEL)), full((N_EXPERTS, D_MODEL)), full((N_EXPERTS, 128)),
                  full((TOK_TILE, TOK_TILE))],
        out_specs=[tile_d, tok_tiles, tok4, tok4, tok4, full((N_EXPERTS, 128))],
        out_shape=[jax.ShapeDtypeStruct((bsz, seq, D_MODEL), F32),
                   jax.ShapeDtypeStruct((t * TILE_SUBLANES, LANES), F32),
                   jax.ShapeDtypeStruct((TOP_K, t), I32),
                   jax.ShapeDtypeStruct((TOP_K, t), F32),
                   jax.ShapeDtypeStruct((TOP_K, t), I32),
                   jax.ShapeDtypeStruct((N_EXPERTS, 128), F32)],
        scratch_shapes=[pltpu.VMEM((N_EXPERTS, 128), F32)],
        compiler_params=pltpu.CompilerParams(dimension_semantics=("arbitrary", "arbitrary"),
                                             vmem_limit_bytes=VMEM_LIMIT),
        name="post",
    )(x3, bg, cv, cv, cv, yf, conv_w, gc, gf, wo_top, wo_bot, nx, wq, kmem, vmem_, wo, nf, wrh, wrl, br, tri)


def _blocks_of(count):
    return lax.shift_right_logical(count + (ROW_BLOCK - 1), ROW_BLOCK.bit_length() - 1)


def _dest_kernel(cnt_ref, idx_ref, rank_ref, dest_ref, start_ref):
    @pl.when(pl.program_id(0) == 0)
    def _():
        def per_expert(e, row0):
            start_ref[e] = row0
            return row0 + _blocks_of(cnt_ref[e]) * ROW_BLOCK

        lax.fori_loop(0, N_EXPERTS, per_expert, jnp.int32(0))

    idx = idx_ref[...]
    dest = rank_ref[...]
    for e in range(N_EXPERTS):
        dest = dest + jnp.where(idx == e, start_ref[e], 0)
    dest_ref[...] = dest


def _dest(counts, idx_t, rank_t):
    t = idx_t.shape[1]
    tok = pl.BlockSpec((TOP_K, PLAN_TOK), lambda i, cnt: (0, i))
    grid_spec = pltpu.PrefetchScalarGridSpec(
        num_scalar_prefetch=1,
        grid=(t // PLAN_TOK,),
        in_specs=[tok, tok],
        out_specs=tok,
        scratch_shapes=[pltpu.SMEM((N_EXPERTS,), I32)],
    )
    return pl.pallas_call(
        _dest_kernel,
        grid_spec=grid_spec,
        out_shape=jax.ShapeDtypeStruct((TOP_K, t), I32),
        compiler_params=pltpu.CompilerParams(dimension_semantics=("arbitrary",), vmem_limit_bytes=VMEM_LIMIT),
        name="dest",
    )(counts, idx_t, rank_t)


def _plan_kernel(cnt_ref, dest_ref, inv_ref, be_ref, nu_ref, wnext_ref, wpar_ref, nxt_ref, *, n_slots):
    n_sched = be_ref.shape[0]
    step = pl.program_id(0)
    chunk = dest_ref.shape[0]
    dump = lambda r: n_slots + (lax.shift_right_logical(r, ROW_BLOCK.bit_length() - 1) & 1) * ROW_BLOCK \
        + (r & (ROW_BLOCK - 1))

    @pl.when(step == 0)
    def _():
        def fill_dump(lo, hi):
            def body(p, c):
                inv_ref[p] = dump(p + ROW_BLOCK)
                return c
            lax.fori_loop(lo, hi, body, 0)

        def find_next(i, nxt):
            e = N_EXPERTS - 1 - i
            nxt_ref[e] = nxt
            return jnp.where(cnt_ref[e] > 0, e, nxt)

        lax.fori_loop(0, N_EXPERTS, find_next, jnp.int32(-1))
        fill_dump(0, ROW_BLOCK)

        def per_expert(e, carry):
            blk0, group = carry
            nb = _blocks_of(cnt_ref[e])

            def fill(j, c):
                be_ref[blk0 + j] = e
                wnext_ref[blk0 + j] = nxt_ref[e]
                wpar_ref[blk0 + j] = group & 1
                return c

            lax.fori_loop(0, nb, fill, 0)
            fill_dump((blk0 + 1) * ROW_BLOCK + cnt_ref[e], (blk0 + nb + 1) * ROW_BLOCK)
            return blk0 + nb, group + jnp.where(nb > 0, 1, 0)

        n_used, _ = lax.fori_loop(0, N_EXPERTS, per_expert, (jnp.int32(0), jnp.int32(0)))
        nu_ref[0] = n_used
        tail_e = be_ref[n_used - 1]

        def tail(j, c):
            be_ref[j] = tail_e
            wnext_ref[j] = -1
            wpar_ref[j] = 0
            return c

        lax.fori_loop(n_used, n_sched, tail, 0)
        fill_dump((n_used + 1) * ROW_BLOCK, inv_ref.shape[0])

    def scatter(grp, c):
        for u in range(DMA_UNROLL):
            i = grp * DMA_UNROLL + u
            inv_ref[ROW_BLOCK + dest_ref[i]] = step * chunk + i
        return c

    lax.fori_loop(0, chunk // DMA_UNROLL, scatter, 0)


def _plan(counts, dest_flat, n_blocks):
    n = dest_flat.shape[0]
    chunk = 8192
    smem = pl.BlockSpec(memory_space=pltpu.SMEM)
    grid_spec = pltpu.PrefetchScalarGridSpec(
        num_scalar_prefetch=1,
        grid=(n // chunk,),
        in_specs=[pl.BlockSpec((chunk,), lambda i, cnt: (i,), memory_space=pltpu.SMEM)],
        out_specs=[smem] * 5,
        scratch_shapes=[pltpu.SMEM((N_EXPERTS,), I32)],
    )
    sched = jax.ShapeDtypeStruct((n_blocks + 1,), I32)
    return pl.pallas_call(
        functools.partial(_plan_kernel, n_slots=n),
        grid_spec=grid_spec,
        out_shape=[jax.ShapeDtypeStruct(((n_blocks + 2) * ROW_BLOCK,), I32), sched,
                   jax.ShapeDtypeStruct((1,), I32), sched, sched],
        compiler_params=pltpu.CompilerParams(dimension_semantics=("arbitrary",), vmem_limit_bytes=VMEM_LIMIT),
        name="plan",
    )(counts, dest_flat)


def _tile_rows(ref, row, n_rows=1):
    return ref.at[pl.ds(pl.multiple_of(row * TILE_SUBLANES, TILE_SUBLANES), n_rows * TILE_SUBLANES), :]


def _expert_kernel(be_ref, nu_ref, wnext_ref, wpar_ref,
                   inv_cur_ref, inv_next_ref, inv_prev_ref, bgu_ref, bd_ref, h3_ref, wgu_hbm, wd_hbm,
                   y_ref,
                   xbuf0, xbuf1, obuf0, obuf1, wgu_f32, wd_f32, wgu_bf, wd_bf, gsem, ssem, wsem, zsem,
                   *, n_slots):
    j = pl.program_id(0)
    nu = nu_ref[0]
    xbuf = (xbuf0, xbuf1)
    obuf = (obuf0, obuf1)
    tokens = n_slots // TOP_K
    assert tokens & (tokens - 1) == 0, "slot -> token uses a mask"

    def gather_row(inv_ref, r, dst):
        tok = inv_ref[0, r] & (tokens - 1)
        return pltpu.make_async_copy(_tile_rows(h3_ref, tok), _tile_rows(dst, r), gsem)

    def scatter_row(inv_ref, r, src):
        return pltpu.make_async_copy(_tile_rows(src, r), _tile_rows(y_ref, inv_ref[0, r]), ssem)

    def weight_copies(e, p):
        return (pltpu.make_async_copy(wgu_hbm.at[e], wgu_f32.at[p], wsem.at[p, 0]),
                pltpu.make_async_copy(wd_hbm.at[e], wd_f32.at[p], wsem.at[p, 1]))

    def wait_gather(dst):
        for _ in range(ROW_BLOCK):
            pltpu.make_async_copy(_tile_rows(h3_ref, 0), _tile_rows(dst, 0), gsem).wait()

    def wait_scatter(src):
        for _ in range(ROW_BLOCK):
            pltpu.make_async_copy(_tile_rows(src, 0), _tile_rows(y_ref, 0), ssem).wait()

    @pl.when(j == 0)
    def _():
        obuf0[...] = jnp.zeros_like(obuf0)
        obuf1[...] = jnp.zeros_like(obuf1)
        zero = [pltpu.make_async_copy(obuf[s], _tile_rows(y_ref, n_slots + s * ROW_BLOCK, ROW_BLOCK), zsem)
                for s in range(2)]
        for z in zero:
            z.start()
        for z in zero:
            z.wait()
        for c in weight_copies(be_ref[0], 0):
            c.start()
        for r in range(ROW_BLOCK):
            gather_row(inv_cur_ref, r, xbuf0).start(priority=r % 2)

    first_of_group = jnp.logical_or(j == 0, be_ref[j] != be_ref[jnp.maximum(j - 1, 0)])

    @pl.when(jnp.logical_and(j < nu, first_of_group))
    def _():
        p = wpar_ref[j]
        for c in weight_copies(be_ref[j], p):
            c.wait()
        wgu_bf[...] = wgu_f32[p].astype(BF16)
        wd_bf[...] = wd_f32[p].astype(BF16)

        @pl.when(wnext_ref[j] >= 0)
        def _():
            for c in weight_copies(wnext_ref[j], 1 - p):
                c.start()

    def step(s, compute):
        wait_gather(xbuf[s])

        @pl.when(j >= 1)
        def _():
            wait_scatter(obuf[s])

        if compute:
            for r in range(ROW_BLOCK):
                gather_row(inv_next_ref, r, xbuf[1 - s]).start(priority=r % 2)
        for r in range(ROW_BLOCK):
            scatter_row(inv_prev_ref, r, obuf[1 - s]).start(priority=(r + 1) % 2)
        if compute:
            x = _load_token_tiles(xbuf[s], ROW_BLOCK).astype(BF16)
            gu = _dot(x, wgu_bf[...]) + bgu_ref[0]
            gate = jnp.minimum(gu[:, :D_EXPERT], SWIGLU_LIMIT)
            up = jnp.clip(gu[:, D_EXPERT:], -SWIGLU_LIMIT, SWIGLU_LIMIT)
            glu = gate * (1.0 / (1.0 + jnp.exp(-SWIGLU_ALPHA * gate)))
            h = ((up + 1.0) * glu).astype(BF16)
            _store_token_tiles(obuf[s], _dot(h, wd_bf[...]) + bd_ref[0], ROW_BLOCK)
        else:
            wait_scatter(obuf[1 - s])

    for s in range(2):
        parity = (j & 1) == s
        pl.when(jnp.logical_and(j < nu, parity))(functools.partial(step, s, True))
        pl.when(jnp.logical_and(j == nu, parity))(functools.partial(step, s, False))


def _experts(block_e, n_used, w_next, w_par, inv3, h3t, wgu, bgu, wd, bd, n_slots):
    n_steps = block_e.shape[0]
    blk_rows = ROW_BLOCK * TILE_SUBLANES
    last_inv = inv3.shape[0] - 1
    exp_map = lambda j, be, nu, wn, wp: (be[j], 0, 0)
    inv_spec = lambda off: pl.BlockSpec((None, 1, ROW_BLOCK),
                                        lambda j, be, nu, wn, wp: (jnp.minimum(j + off, last_inv), 0, 0),
                                        memory_space=pltpu.SMEM)
    hbm = pl.BlockSpec(memory_space=pl.ANY)
    grid_spec = pltpu.PrefetchScalarGridSpec(
        num_scalar_prefetch=4,
        grid=(n_steps,),
        in_specs=[inv_spec(1), inv_spec(2), inv_spec(0),
                  pl.BlockSpec((1, 1, 2 * D_EXPERT), exp_map), pl.BlockSpec((1, 1, D_MODEL), exp_map),
                  hbm, hbm, hbm],
        out_specs=hbm,
        scratch_shapes=[pltpu.VMEM((blk_rows, LANES), F32)] * 4
        + [pltpu.VMEM((2, D_MODEL, 2 * D_EXPERT), F32), pltpu.VMEM((2, D_EXPERT, D_MODEL), F32),
           pltpu.VMEM((D_MODEL, 2 * D_EXPERT), BF16), pltpu.VMEM((D_EXPERT, D_MODEL), BF16),
           pltpu.SemaphoreType.DMA, pltpu.SemaphoreType.DMA, pltpu.SemaphoreType.DMA((2, 2)),
           pltpu.SemaphoreType.DMA],
    )
    return pl.pallas_call(
        functools.partial(_expert_kernel, n_slots=n_slots),
        grid_spec=grid_spec,
        out_shape=jax.ShapeDtypeStruct(((n_slots + 2 * ROW_BLOCK) * TILE_SUBLANES, LANES), F32),
        compiler_params=pltpu.CompilerParams(dimension_semantics=("arbitrary",), vmem_limit_bytes=VMEM_LIMIT),
        name="experts",
    )(block_e, n_used, w_next, w_par, inv3, inv3, inv3, bgu, bd, h3t, wgu, wd)


def _combine_kernel(*refs):
    y_refs, (x2_ref, w_ref, g_ref, out_ref) = refs[:TOP_K], refs[TOP_K:]
    y = x2_ref[...]
    for k in range(TOP_K):
        y = y + w_ref[:, k:k + 1] * _load_token_tiles(y_refs[k], ROUTE_TILE)
    out_ref[...] = _rms(y, g_ref[...])


def _combine(ybuf, x2, w_cols, g):
    t = x2.shape[0]
    n_tiles = t // ROUTE_TILE
    slot = lambda k: pl.BlockSpec((ROUTE_TILE * TILE_SUBLANES, LANES), lambda i: (k * n_tiles + i, 0))
    return pl.pallas_call(
        _combine_kernel,
        grid=(n_tiles,),
        in_specs=[slot(k) for k in range(TOP_K)]
        + [pl.BlockSpec((ROUTE_TILE, D_MODEL), lambda i: (i, 0)),
           pl.BlockSpec((ROUTE_TILE, TOP_K), lambda i: (i, 0)),
           pl.BlockSpec((1, D_MODEL), lambda i: (0, 0))],
        out_specs=pl.BlockSpec((ROUTE_TILE, D_MODEL), lambda i: (i, 0)),
        out_shape=jax.ShapeDtypeStruct((t, D_MODEL), F32),
        compiler_params=pltpu.CompilerParams(dimension_semantics=("parallel",), vmem_limit_bytes=VMEM_LIMIT),
        name="combine",
    )(*([ybuf] * TOP_K), x2, w_cols, g)


def _layer(x, mem, norm_mix, w_in, conv_w, g_conv_out, g_fft_out, w_out, norm_xattn, norm_mem,
           w_q, w_k, w_v, w_o, norm_ffn, w_router, b_router, w_gate_up, b_gate_up, w_down, b_down, tables):
    bsz, seq, d = x.shape
    t = bsz * seq
    cs, m1, m2 = tables
    row = lambda v: v.reshape(1, -1)

    kmem, vmem_ = _memkv(mem, row(norm_mem), w_k.astype(BF16), w_v.astype(BF16))

    bg, cv, a, s = _inproj(x.reshape(t, d), row(norm_mix), w_in.astype(BF16), cs)
    cols = FFT_N2 * FFT_WIDTH
    g = _fft1(a.reshape(bsz, FFT_N1, cols), s.reshape(bsz, FFT_N1, cols), m1)
    yf = _fft2(g.reshape(bsz, 2, FFT_N1, FFT_N2, FFT_WIDTH), m2).reshape(bsz, seq, FFT_WIDTH)

    w_out_b = w_out.astype(BF16)
    wr_t = w_router.T
    wr_hi = wr_t.astype(BF16)
    wr_lo = (wr_t - wr_hi.astype(F32)).astype(BF16)
    tri = (jnp.arange(TOK_TILE)[:, None] < jnp.arange(TOK_TILE)[None, :]).astype(BF16)
    x2, h3, idx_t, w_t, rank_t, cnt = _post(
        x, bg.reshape(bsz, seq, CONV_WIDTH), cv.reshape(bsz, seq, CONV_WIDTH), yf,
        conv_w, row(g_conv_out), row(g_fft_out), w_out_b[:CONV_WIDTH], w_out_b[CONV_WIDTH:],
        row(norm_xattn), w_q.astype(BF16), kmem, vmem_, w_o.astype(BF16), row(norm_ffn),
        wr_hi, wr_lo, jnp.broadcast_to(b_router[:, None], (N_EXPERTS, 128)), tri)

    n_blocks = (t * TOP_K) // ROW_BLOCK + N_EXPERTS
    counts = cnt[:, 0].astype(I32)
    dest = _dest(counts, idx_t, rank_t)
    inv, block_e, n_used, w_next, w_par = _plan(counts, dest.reshape(-1), n_blocks)
    ybuf = _experts(block_e, n_used, w_next, w_par, inv.reshape(n_blocks + 2, 1, ROW_BLOCK), h3,
                    w_gate_up, b_gate_up[:, None, :], w_down, b_down[:, None, :], t * TOP_K)
    return x2.reshape(t, d), ybuf, w_t.T


def kernel(x, mem, norm_mix, w_in, conv_w, g_conv_out, g_fft_out, w_out, norm_xattn, norm_mem, w_q, w_k, w_v, w_o,
           norm_ffn, w_router, b_router, w_gate_up, b_gate_up, w_down, b_down, norm_final):
    bsz, seq, d = x.shape
    depth = norm_mix.shape[0]
    assert depth == 1, "final norm is fused into the combine step of the single layer"
    tables = _dft_tables(seq)
    x2, ybuf, w_cols = _layer(
        x, mem, norm_mix[0], w_in[0], conv_w[0], g_conv_out[0], g_fft_out[0], w_out[0], norm_xattn[0],
        norm_mem[0], w_q[0], w_k[0], w_v[0], w_o[0], norm_ffn[0], w_router[0], b_router[0],
        w_gate_up[0], b_gate_up[0], w_down[0], b_down[0], tables)
    out = _combine(ybuf, x2, w_cols, norm_final.reshape(1, -1))
    return out.reshape(bsz, seq, d)
```

```python
import functools
import math

import numpy as np
import jax
import jax.numpy as jnp
from jax import lax
from jax.experimental import pallas as pl
from jax.experimental.pallas import tpu as pltpu

F32 = jnp.float32
BF16 = jnp.bfloat16
I32 = jnp.int32

D_MODEL = 1024
CONV_WIDTH = 512
FFT_WIDTH = 512
GROUP_DIM = 64
IN_PROJ_WIDTH = 3 * CONV_WIDTH + FFT_WIDTH
MEM_LEN = 256
XATTN_HEADS = 4
XATTN_HEAD_DIM = D_MODEL // XATTN_HEADS
N_EXPERTS = 32
TOP_K = 4
D_EXPERT = D_MODEL
SWIGLU_LIMIT = 7.0
SWIGLU_ALPHA = 1.702
EPS = 1e-5

FFT_N1 = 64
FFT_N2 = 128

TOK_TILE = 512
ROW_BLOCK = 256
OCT_ROWS = 8
OCT_PER_BLOCK = ROW_BLOCK // OCT_ROWS
TILE_OCT = (TOK_TILE * TOP_K) // OCT_ROWS + N_EXPERTS
FFT1_LANES = 8192
FFT2_K1 = 8
BF16_SUBLANES = 16
TILE_SUBLANES = 8
LANES = 128
assert D_MODEL == TILE_SUBLANES * LANES
VMEM_LIMIT = 56 * 1024 * 1024


def _rms(x, g):
    return x * lax.rsqrt(jnp.mean(x * x, axis=-1, keepdims=True) + EPS) * g


def _dot(a, b):
    return jnp.dot(a, b, preferred_element_type=F32)


def _dot_nt(a, b):
    return lax.dot_general(a, b, (((1,), (1,)), ((), ())), preferred_element_type=F32)


def _load_token_tiles(ref, rows, base=0):
    return jnp.concatenate(
        [ref[pl.ds(base + s, rows, stride=TILE_SUBLANES), :] for s in range(TILE_SUBLANES)], axis=-1)


def _store_token_tiles(ref, val, rows):
    for s in range(TILE_SUBLANES):
        ref[pl.ds(s, rows, stride=TILE_SUBLANES), :] = val[:, s * LANES:(s + 1) * LANES]


def _dft_tables(seq):
    assert seq == FFT_N1 * FFT_N2
    c = np.arange(GROUP_DIM)
    ang = 2.0 * np.pi * ((c[:, None] * c[None, :]) % GROUP_DIM) / GROUP_DIM
    groups = FFT_WIDTH // GROUP_DIM
    eye = np.eye(groups)
    cs = np.concatenate([np.kron(eye, np.cos(ang)), np.kron(eye, np.sin(ang))], axis=1) / math.sqrt(GROUP_DIM)
    n1 = np.arange(FFT_N1)
    a1 = 2.0 * np.pi * ((n1[:, None] * n1[None, :]) % FFT_N1) / FFT_N1
    c1, s1 = np.cos(a1), np.sin(a1)
    m1 = np.block([[c1, -s1], [s1, c1]]) / math.sqrt(FFT_N1)
    k1 = np.arange(FFT_N1)[:, None, None]
    k2 = np.arange(FFT_N2)[None, :, None]
    n2 = np.arange(FFT_N2)[None, None, :]
    a2 = 2.0 * np.pi * ((n2 * (k1 + FFT_N1 * k2)) % seq) / seq
    m2 = np.concatenate([np.cos(a2), -np.sin(a2)], axis=2) / math.sqrt(FFT_N2)
    return (jnp.asarray(cs, F32).astype(BF16), jnp.asarray(m1, F32).astype(BF16),
            jnp.asarray(m2, F32).astype(BF16))


def _memkv_kernel(mem_ref, g_ref, wk_ref, wv_ref, k_ref, v_ref):
    m = _rms(mem_ref[0], g_ref[...]).astype(BF16)
    k_ref[0] = _dot(m, wk_ref[...]).astype(BF16)
    v_ref[0] = _dot(m, wv_ref[...]).astype(BF16)


def _memkv(mem, g, wk, wv):
    bsz = mem.shape[0]
    full = lambda shape: pl.BlockSpec(shape, lambda b: (0,) * len(shape))
    per_b = pl.BlockSpec((1, MEM_LEN, D_MODEL), lambda b: (b, 0, 0))
    return pl.pallas_call(
        _memkv_kernel,
        grid=(bsz,),
        in_specs=[per_b, full((1, D_MODEL)), full((D_MODEL, D_MODEL)), full((D_MODEL, D_MODEL))],
        out_specs=[per_b, per_b],
        out_shape=[jax.ShapeDtypeStruct((bsz, MEM_LEN, D_MODEL), BF16)] * 2,
        compiler_params=pltpu.CompilerParams(dimension_semantics=("arbitrary",), vmem_limit_bytes=VMEM_LIMIT),
        name="memkv",
    )(mem, g, wk, wv)


def _inproj_kernel(x_ref, g_ref, win_ref, cs_ref, b_ref, cv_ref, a_ref, s_ref):
    h = _rms(x_ref[...], g_ref[...]).astype(BF16)
    z = _dot(h, win_ref[...])
    b_ref[...] = z[:, :CONV_WIDTH].astype(BF16)
    cv_ref[...] = (z[:, CONV_WIDTH:2 * CONV_WIDTH] * z[:, 2 * CONV_WIDTH:3 * CONV_WIDTH]).astype(BF16)
    u = z[:, 3 * CONV_WIDTH:].astype(BF16)
    ab = _dot(u, cs_ref[...])
    a_ref[...] = ab[:, :FFT_WIDTH].astype(BF16)
    s_ref[...] = ab[:, FFT_WIDTH:].astype(BF16)


def _inproj(x2d, g, w_in, cs):
    t = x2d.shape[0]
    tile = pl.BlockSpec((TOK_TILE, D_MODEL), lambda i: (i, 0))
    half = pl.BlockSpec((TOK_TILE, CONV_WIDTH), lambda i: (i, 0))
    full = lambda shape: pl.BlockSpec(shape, lambda i: (0,) * len(shape))
    return pl.pallas_call(
        _inproj_kernel,
        grid=(t // TOK_TILE,),
        in_specs=[tile, full((1, D_MODEL)), full((D_MODEL, IN_PROJ_WIDTH)), full((FFT_WIDTH, 2 * FFT_WIDTH))],
        out_specs=[half] * 4,
        out_shape=[jax.ShapeDtypeStruct((t, CONV_WIDTH), BF16)] * 4,
        compiler_params=pltpu.CompilerParams(dimension_semantics=("parallel",), vmem_limit_bytes=VMEM_LIMIT),
        name="inproj",
    )(x2d, g, w_in, cs)


def _fft1_kernel(a_ref, s_ref, m1_ref, g_ref):
    x = jnp.concatenate([a_ref[0], s_ref[0]], axis=0)
    g_ref[0] = _dot(m1_ref[...], x).astype(BF16)


def _fft1(a3, s3, m1):
    bsz, _, cols = a3.shape
    blk = pl.BlockSpec((1, FFT_N1, FFT1_LANES), lambda b, j: (b, 0, j))
    return pl.pallas_call(
        _fft1_kernel,
        grid=(bsz, cols // FFT1_LANES),
        in_specs=[blk, blk, pl.BlockSpec((2 * FFT_N1, 2 * FFT_N1), lambda b, j: (0, 0))],
        out_specs=pl.BlockSpec((1, 2 * FFT_N1, FFT1_LANES), lambda b, j: (b, 0, j)),
        out_shape=jax.ShapeDtypeStruct((bsz, 2 * FFT_N1, cols), BF16),
        compiler_params=pltpu.CompilerParams(dimension_semantics=("parallel", "parallel"),
                                             vmem_limit_bytes=VMEM_LIMIT),
        name="fft1",
    )(a3, s3, m1)


def _fft2_kernel(g_ref, m2_ref, y_ref):
    for j in range(FFT2_K1):
        x = jnp.concatenate([g_ref[0, 0, j], g_ref[0, 1, j]], axis=0)
        y_ref[0, :, j * FFT_WIDTH:(j + 1) * FFT_WIDTH] = _dot(m2_ref[j], x).astype(BF16)


def _fft2(g5, m2):
    bsz = g5.shape[0]
    return pl.pallas_call(
        _fft2_kernel,
        grid=(bsz, FFT_N1 // FFT2_K1),
        in_specs=[pl.BlockSpec((1, 2, FFT2_K1, FFT_N2, FFT_WIDTH), lambda b, j: (b, 0, j, 0, 0)),
                  pl.BlockSpec((FFT2_K1, FFT_N2, 2 * FFT_N2), lambda b, j: (j, 0, 0))],
        out_specs=pl.BlockSpec((1, FFT_N2, FFT2_K1 * FFT_WIDTH), lambda b, j: (b, 0, j)),
        out_shape=jax.ShapeDtypeStruct((bsz, FFT_N2, FFT_N1 * FFT_WIDTH), BF16),
        compiler_params=pltpu.CompilerParams(dimension_semantics=("parallel", "parallel"),
                                             vmem_limit_bytes=VMEM_LIMIT),
        name="fft2",
    )(g5, m2)


def _post_kernel(x_ref, bg_ref, cv_ref, cvp_ref, cvn_ref, yf_ref, convw_ref, gc_ref, gf_ref,
                 wot_ref, wob_ref, nx_ref, wq_ref, k_ref, v_ref, wo_ref, nf_ref,
                 wrh_ref, wrl_ref, br_ref, tri_ref, tri32_ref,
                 x2_ref, h3_ref, w_ref, q_ref, cnt_ref):
    i = pl.program_id(1)
    last = pl.num_programs(1) - 1

    cv = cv_ref[0].astype(F32)
    prev_row = jnp.where(i > 0, cvp_ref[0].astype(F32)[BF16_SUBLANES - 1:BF16_SUBLANES, :], 0.0)
    next_row = jnp.where(i < last, cvn_ref[0].astype(F32)[0:1, :], 0.0)
    rows = lax.broadcasted_iota(I32, cv.shape, 0)
    cvm1 = jnp.where(rows == 0, prev_row, pltpu.roll(cv, 1, axis=0))
    cvp1 = jnp.where(rows == TOK_TILE - 1, next_row, pltpu.roll(cv, TOK_TILE - 1, axis=0))
    cw = convw_ref[...]
    y_conv = bg_ref[0].astype(F32) * (cw[0:1] * cvm1 + cw[1:2] * cv + cw[2:3] * cvp1)
    yc_n = _rms(y_conv, gc_ref[...]).astype(BF16)
    yf_n = _rms(yf_ref[0].astype(F32), gf_ref[...]).astype(BF16)
    x1 = x_ref[0] + _dot(yc_n, wot_ref[...]) + _dot(yf_n, wob_ref[...])

    h2 = _rms(x1, nx_ref[...]).astype(BF16)
    q = _dot(h2, wq_ref[...]).astype(BF16)
    heads = []
    for hd in range(XATTN_HEADS):
        sl = slice(hd * XATTN_HEAD_DIM, (hd + 1) * XATTN_HEAD_DIM)
        s = _dot_nt(q[:, sl], k_ref[0, :, sl]) * (XATTN_HEAD_DIM ** -0.5)
        e = jnp.exp(s - jnp.max(s, axis=-1, keepdims=True))
        p = e * (1.0 / jnp.sum(e, axis=-1, keepdims=True))
        heads.append(_dot(p.astype(BF16), v_ref[0, :, sl]).astype(BF16))
    x2 = x1 + _dot(jnp.concatenate(heads, axis=-1), wo_ref[...])
    x2_ref[0] = x2

    h3 = _rms(x2, nf_ref[...])
    _store_token_tiles(h3_ref, h3, TOK_TILE)
    h3h = h3.astype(BF16)
    h3l = (h3 - h3h.astype(F32)).astype(BF16)
    logits = (_dot_nt(wrh_ref[...], h3h) + _dot_nt(wrh_ref[...], h3l) + _dot_nt(wrl_ref[...], h3h)
              + br_ref[:, 0:1])
    eidx = lax.broadcasted_iota(I32, logits.shape, 0)
    work = logits
    vals, idxs = [], []
    for _ in range(TOP_K):
        m = jnp.max(work, axis=0, keepdims=True)
        ik = jnp.min(jnp.where(work == m, eidx, N_EXPERTS), axis=0, keepdims=True)
        vals.append(m)
        idxs.append(ik)
        work = jnp.where(eidx == ik, -jnp.inf, work)
    ex = [jnp.exp(v - vals[0]) for v in vals]
    inv_den = 1.0 / (ex[0] + ex[1] + ex[2] + ex[3])
    w_ref[0] = jnp.concatenate([e * inv_den for e in ex], axis=0)

    sel = jnp.zeros(logits.shape, F32)
    for ik in idxs:
        sel = sel + jnp.where(eidx == ik, 1.0, 0.0)
    cnt = jnp.broadcast_to(jnp.sum(sel, axis=1, keepdims=True), (N_EXPERTS, LANES))
    cnt_ref[0] = cnt
    seg_rows = jnp.floor((cnt + (OCT_ROWS - 1)) * (1.0 / OCT_ROWS)) * OCT_ROWS
    seg_start = _dot(tri32_ref[...], seg_rows.astype(BF16))
    pos = seg_start[:, 0:1] + _dot(sel.astype(BF16), tri_ref[...])
    q_ref[0] = jnp.concatenate(
        [jnp.sum(jnp.where(eidx == ik, pos, 0.0), axis=0, keepdims=True) for ik in idxs], axis=0).astype(I32)


def _post(x3, bg, cv, yf, conv_w, gc, gf, wo_top, wo_bot, nx, wq, kmem, vmem_, wo, nf, wrh, wrl, br, tri, tri32):
    bsz, seq, _ = x3.shape
    nt = seq // TOK_TILE
    t = bsz * seq
    n_tiles = bsz * nt
    halo_per_tile = TOK_TILE // BF16_SUBLANES
    n_halo = seq // BF16_SUBLANES
    full = lambda shape: pl.BlockSpec(shape, lambda b, i: (0,) * len(shape))
    tile_d = pl.BlockSpec((1, TOK_TILE, D_MODEL), lambda b, i: (b, i, 0))
    tile_h = pl.BlockSpec((1, TOK_TILE, CONV_WIDTH), lambda b, i: (b, i, 0))
    halo_prev = pl.BlockSpec((1, BF16_SUBLANES, CONV_WIDTH),
                             lambda b, i: (b, jnp.maximum(i * halo_per_tile - 1, 0), 0))
    halo_next = pl.BlockSpec((1, BF16_SUBLANES, CONV_WIDTH),
                             lambda b, i: (b, jnp.minimum((i + 1) * halo_per_tile, n_halo - 1), 0))
    mem_blk = pl.BlockSpec((1, MEM_LEN, D_MODEL), lambda b, i: (b, 0, 0))
    tok4 = pl.BlockSpec((1, TOP_K, TOK_TILE), lambda b, i: (b * nt + i, 0, 0))
    per_tile = pl.BlockSpec((1, N_EXPERTS, LANES), lambda b, i: (b * nt + i, 0, 0))
    tok_tiles = pl.BlockSpec((TOK_TILE * TILE_SUBLANES, LANES), lambda b, i: (b * nt + i, 0))
    return pl.pallas_call(
        _post_kernel,
        grid=(bsz, nt),
        in_specs=[tile_d, tile_h, tile_h, halo_prev, halo_next, tile_h,
                  full((3, CONV_WIDTH)), full((1, CONV_WIDTH)), full((1, FFT_WIDTH)),
                  full((CONV_WIDTH, D_MODEL)), full((FFT_WIDTH, D_MODEL)), full((1, D_MODEL)),
                  full((D_MODEL, D_MODEL)), mem_blk, mem_blk, full((D_MODEL, D_MODEL)), full((1, D_MODEL)),
                  full((N_EXPERTS, D_MODEL)), full((N_EXPERTS, D_MODEL)), full((N_EXPERTS, 128)),
                  full((TOK_TILE, TOK_TILE)), full((N_EXPERTS, N_EXPERTS))],
        out_specs=[tile_d, tok_tiles, tok4, tok4, per_tile],
        out_shape=[jax.ShapeDtypeStruct((bsz, seq, D_MODEL), F32),
                   jax.ShapeDtypeStruct((t * TILE_SUBLANES, LANES), F32),
                   jax.ShapeDtypeStruct((n_tiles, TOP_K, TOK_TILE), F32),
                   jax.ShapeDtypeStruct((n_tiles, TOP_K, TOK_TILE), I32),
                   jax.ShapeDtypeStruct((n_tiles, N_EXPERTS, LANES), F32)],
        compiler_params=pltpu.CompilerParams(dimension_semantics=("parallel", "parallel"),
                                             vmem_limit_bytes=VMEM_LIMIT),
        name="post",
    )(x3, bg, cv, cv, cv, yf, conv_w, gc, gf, wo_top, wo_bot, nx, wq, kmem, vmem_, wo, nf, wrh, wrl, br, tri, tri32)


def _plan_kernel(cnt_ref, src_ref, dst_ref, be_ref, nu_ref, wnext_ref, wpar_ref, off_ref, nxt_ref, *, n_tiles):
    n_sched = be_ref.shape[0]
    dump0 = n_tiles * TILE_OCT

    def fill_pad(lo, hi):
        def body(p, c):
            o = p & (OCT_PER_BLOCK - 1)
            parity = lax.shift_right_logical(p, OCT_PER_BLOCK.bit_length() - 1) & 1
            src_ref[p] = dump0 + 2 * OCT_PER_BLOCK + o
            dst_ref[p] = dump0 + parity * OCT_PER_BLOCK + o
            return c
        lax.fori_loop(lo, hi, body, 0)

    def clear(tile, c):
        off_ref[tile] = 0
        return c

    lax.fori_loop(0, n_tiles, clear, 0)

    def find_next(i, nxt):
        e = N_EXPERTS - 1 - i
        nxt_ref[e] = nxt
        total = lax.fori_loop(0, n_tiles, lambda tile, s: s + cnt_ref[tile * N_EXPERTS + e], jnp.int32(0))
        return jnp.where(total > 0, e, nxt)

    lax.fori_loop(0, N_EXPERTS, find_next, jnp.int32(-1))
    fill_pad(0, OCT_PER_BLOCK)

    def per_expert(e, carry):
        pos0, blk0, group = carry

        def per_tile(tile, pos):
            n_oct = lax.shift_right_logical(cnt_ref[tile * N_EXPERTS + e] + (OCT_ROWS - 1), OCT_ROWS.bit_length() - 1)
            base = tile * TILE_OCT + off_ref[tile]
            off_ref[tile] = off_ref[tile] + n_oct

            def per_octet(o, c):
                src_ref[pos + o] = base + o
                dst_ref[pos + o] = base + o
                return c

            lax.fori_loop(0, n_oct, per_octet, 0)
            return pos + n_oct

        pos1 = lax.fori_loop(0, n_tiles, per_tile, pos0)
        nb = lax.shift_right_logical(pos1 - pos0 + (OCT_PER_BLOCK - 1), OCT_PER_BLOCK.bit_length() - 1)
        pos2 = pos0 + nb * OCT_PER_BLOCK
        fill_pad(pos1, pos2)

        def fill(j, c):
            be_ref[blk0 + j] = e
            wnext_ref[blk0 + j] = nxt_ref[e]
            wpar_ref[blk0 + j] = group & 1
            return c

        lax.fori_loop(0, nb, fill, 0)
        return pos2, blk0 + nb, group + jnp.where(nb > 0, 1, 0)

    pos, n_used, _ = lax.fori_loop(0, N_EXPERTS, per_expert,
                                   (jnp.int32(OCT_PER_BLOCK), jnp.int32(0), jnp.int32(0)))
    nu_ref[0] = n_used
    tail_e = be_ref[n_used - 1]

    def tail(j, c):
        be_ref[j] = tail_e
        wnext_ref[j] = -1
        wpar_ref[j] = 0
        return c

    lax.fori_loop(n_used, n_sched, tail, 0)
    fill_pad(pos, src_ref.shape[0])


def _plan(counts, n_tiles, n_blocks):
    smem = pl.BlockSpec(memory_space=pltpu.SMEM)
    grid_spec = pltpu.PrefetchScalarGridSpec(
        num_scalar_prefetch=1,
        grid=(1,),
        in_specs=[],
        out_specs=[smem] * 6,
        scratch_shapes=[pltpu.SMEM((n_tiles,), I32), pltpu.SMEM((N_EXPERTS,), I32)],
    )
    octs = jax.ShapeDtypeStruct(((n_blocks + 2) * OCT_PER_BLOCK,), I32)
    sched = jax.ShapeDtypeStruct((n_blocks + 1,), I32)
    return pl.pallas_call(
        functools.partial(_plan_kernel, n_tiles=n_tiles),
        grid_spec=grid_spec,
        out_shape=[octs, octs, sched, jax.ShapeDtypeStruct((1,), I32), sched, sched],
        compiler_params=pltpu.CompilerParams(dimension_semantics=("arbitrary",), vmem_limit_bytes=VMEM_LIMIT),
        name="plan",
    )(counts)


def _localsort_kernel(q_ref, h3_ref, xs_ref):
    xs_ref[...] = jnp.zeros_like(xs_ref)

    @pl.when(pl.program_id(0) < pl.num_programs(0) - 1)
    def _():
        def group(g, c):
            for u in range(TILE_SUBLANES):
                t = g * TILE_SUBLANES + u
                row = _tile_rows(h3_ref, t)[...]
                for k in range(TOP_K):
                    _tile_rows(xs_ref, q_ref[k, t])[...] = row
            return c

        lax.fori_loop(0, TOK_TILE // TILE_SUBLANES, group, 0)


def _localsort(q_tiles, h3t):
    n_tiles = q_tiles.shape[0]
    last = n_tiles - 1
    region = TILE_OCT * OCT_ROWS * TILE_SUBLANES
    return pl.pallas_call(
        _localsort_kernel,
        grid=(n_tiles + 1,),
        in_specs=[pl.BlockSpec((None, TOP_K, TOK_TILE), lambda i: (jnp.minimum(i, last), 0, 0),
                               memory_space=pltpu.SMEM),
                  pl.BlockSpec((TOK_TILE * TILE_SUBLANES, LANES), lambda i: (jnp.minimum(i, last), 0))],
        out_specs=pl.BlockSpec((region, LANES), lambda i: (i, 0)),
        out_shape=jax.ShapeDtypeStruct(((n_tiles + 1) * region, LANES), F32),
        compiler_params=pltpu.CompilerParams(dimension_semantics=("parallel",), vmem_limit_bytes=VMEM_LIMIT),
        name="localsort",
    )(q_tiles, h3t)


def _tile_rows(ref, row, n_rows=1):
    return ref.at[pl.ds(pl.multiple_of(row * TILE_SUBLANES, TILE_SUBLANES), n_rows * TILE_SUBLANES), :]


def _expert_kernel(be_ref, nu_ref, wnext_ref, wpar_ref,
                   src_cur_ref, src_next_ref, dst_prev_ref, bgu_ref, bd_ref, xs_ref, wgu_hbm, wd_hbm,
                   y_ref,
                   xbuf0, xbuf1, obuf0, obuf1, wgu_f32, wd_f32, wgu_bf, wd_bf, gsem, ssem, wsem):
    j = pl.program_id(0)
    nu = nu_ref[0]
    xbuf = (xbuf0, xbuf1)
    obuf = (obuf0, obuf1)
    octet = lambda ref, o: _tile_rows(ref, o * OCT_ROWS, OCT_ROWS)

    def gather_octet(src_ref, o, dst):
        return pltpu.make_async_copy(octet(xs_ref, src_ref[0, o]), octet(dst, o), gsem)

    def scatter_octet(dst_ref, o, src):
        return pltpu.make_async_copy(octet(src, o), octet(y_ref, dst_ref[0, o]), ssem)

    def weight_copies(e, p):
        return (pltpu.make_async_copy(wgu_hbm.at[e], wgu_f32.at[p], wsem.at[p, 0]),
                pltpu.make_async_copy(wd_hbm.at[e], wd_f32.at[p], wsem.at[p, 1]))

    def wait_gather(dst):
        for _ in range(OCT_PER_BLOCK):
            pltpu.make_async_copy(octet(xs_ref, 0), octet(dst, 0), gsem).wait()

    def wait_scatter(src):
        for _ in range(OCT_PER_BLOCK):
            pltpu.make_async_copy(octet(src, 0), octet(y_ref, 0), ssem).wait()

    @pl.when(j == 0)
    def _():
        obuf1[...] = jnp.zeros_like(obuf1)
        for c in weight_copies(be_ref[0], 0):
            c.start()
        for o in range(OCT_PER_BLOCK):
            gather_octet(src_cur_ref, o, xbuf0).start(priority=o % 2)

    first_of_group = jnp.logical_or(j == 0, be_ref[j] != be_ref[jnp.maximum(j - 1, 0)])

    @pl.when(jnp.logical_and(j < nu, first_of_group))
    def _():
        p = wpar_ref[j]
        for c in weight_copies(be_ref[j], p):
            c.wait()
        wgu_bf[...] = wgu_f32[p].astype(BF16)
        wd_bf[...] = wd_f32[p].astype(BF16)

        @pl.when(wnext_ref[j] >= 0)
        def _():
            for c in weight_copies(wnext_ref[j], 1 - p):
                c.start()

    def step(s, compute):
        wait_gather(xbuf[s])

        @pl.when(j >= 1)
        def _():
            wait_scatter(obuf[s])

        if compute:
            for o in range(OCT_PER_BLOCK):
                gather_octet(src_next_ref, o, xbuf[1 - s]).start(priority=o % 2)
        for o in range(OCT_PER_BLOCK):
            scatter_octet(dst_prev_ref, o, obuf[1 - s]).start(priority=(o + 1) % 2)
        if compute:
            x = _load_token_tiles(xbuf[s], ROW_BLOCK).astype(BF16)
            gu = _dot(x, wgu_bf[...]) + bgu_ref[0]
            gate = jnp.minimum(gu[:, :D_EXPERT], SWIGLU_LIMIT)
            up = jnp.clip(gu[:, D_EXPERT:], -SWIGLU_LIMIT, SWIGLU_LIMIT)
            glu = gate * (1.0 / (1.0 + jnp.exp(-SWIGLU_ALPHA * gate)))
            h = ((up + 1.0) * glu).astype(BF16)
            _store_token_tiles(obuf[s], _dot(h, wd_bf[...]) + bd_ref[0], ROW_BLOCK)
        else:
            wait_scatter(obuf[1 - s])

    for s in range(2):
        parity = (j & 1) == s
        pl.when(jnp.logical_and(j < nu, parity))(functools.partial(step, s, True))
        pl.when(jnp.logical_and(j == nu, parity))(functools.partial(step, s, False))


def _experts(block_e, n_used, w_next, w_par, src3, dst3, xs, wgu, bgu, wd, bd):
    n_steps = block_e.shape[0]
    blk_rows = ROW_BLOCK * TILE_SUBLANES
    last_blk = src3.shape[0] - 1
    exp_map = lambda j, be, nu, wn, wp: (be[j], 0, 0)
    oct_spec = lambda off: pl.BlockSpec((None, 1, OCT_PER_BLOCK),
                                        lambda j, be, nu, wn, wp: (jnp.minimum(j + off, last_blk), 0, 0),
                                        memory_space=pltpu.SMEM)
    hbm = pl.BlockSpec(memory_space=pl.ANY)
    grid_spec = pltpu.PrefetchScalarGridSpec(
        num_scalar_prefetch=4,
        grid=(n_steps,),
        in_specs=[oct_spec(1), oct_spec(2), oct_spec(0),
                  pl.BlockSpec((1, 1, 2 * D_EXPERT), exp_map), pl.BlockSpec((1, 1, D_MODEL), exp_map),
                  hbm, hbm, hbm],
        out_specs=hbm,
        scratch_shapes=[pltpu.VMEM((blk_rows, LANES), F32)] * 4
        + [pltpu.VMEM((2, D_MODEL, 2 * D_EXPERT), F32), pltpu.VMEM((2, D_EXPERT, D_MODEL), F32),
           pltpu.VMEM((D_MODEL, 2 * D_EXPERT), BF16), pltpu.VMEM((D_EXPERT, D_MODEL), BF16),
           pltpu.SemaphoreType.DMA, pltpu.SemaphoreType.DMA, pltpu.SemaphoreType.DMA((2, 2))],
    )
    return pl.pallas_call(
        _expert_kernel,
        grid_spec=grid_spec,
        out_shape=jax.ShapeDtypeStruct(xs.shape, F32),
        input_output_aliases={9: 0},
        compiler_params=pltpu.CompilerParams(dimension_semantics=("arbitrary",), vmem_limit_bytes=VMEM_LIMIT),
        name="experts",
    )(block_e, n_used, w_next, w_par, src3, src3, dst3, bgu, bd, xs, wgu, wd)


def _combine_kernel(q_ref, w_ref, y_ref, x2_ref, g_ref, out_ref, acc_ref):
    def group(grp, c):
        for u in range(TILE_SUBLANES):
            t = grp * TILE_SUBLANES + u
            acc = w_ref[0, t] * _tile_rows(y_ref, q_ref[0, t])[...]
            for k in range(1, TOP_K):
                acc = acc + w_ref[k, t] * _tile_rows(y_ref, q_ref[k, t])[...]
            _tile_rows(acc_ref, t)[...] = acc
        return c

    lax.fori_loop(0, TOK_TILE // TILE_SUBLANES, group, 0)
    out_ref[...] = _rms(x2_ref[...] + _load_token_tiles(acc_ref, TOK_TILE), g_ref[...])


def _combine(q_tiles, w_tiles, ybuf, x2, g):
    n_tiles = q_tiles.shape[0]
    region = TILE_OCT * OCT_ROWS * TILE_SUBLANES
    tok4 = pl.BlockSpec((None, TOP_K, TOK_TILE), lambda i: (i, 0, 0), memory_space=pltpu.SMEM)
    return pl.pallas_call(
        _combine_kernel,
        grid=(n_tiles,),
        in_specs=[tok4, tok4,
                  pl.BlockSpec((region, LANES), lambda i: (i, 0)),
                  pl.BlockSpec((TOK_TILE, D_MODEL), lambda i: (i, 0)),
                  pl.BlockSpec((1, D_MODEL), lambda i: (0, 0))],
        out_specs=pl.BlockSpec((TOK_TILE, D_MODEL), lambda i: (i, 0)),
        out_shape=jax.ShapeDtypeStruct(x2.shape, F32),
        scratch_shapes=[pltpu.VMEM((TOK_TILE * TILE_SUBLANES, LANES), F32)],
        compiler_params=pltpu.CompilerParams(dimension_semantics=("parallel",), vmem_limit_bytes=VMEM_LIMIT),
        name="combine",
    )(q_tiles, w_tiles, ybuf, x2, g)


def _layer(x, mem, norm_mix, w_in, conv_w, g_conv_out, g_fft_out, w_out, norm_xattn, norm_mem,
           w_q, w_k, w_v, w_o, norm_ffn, w_router, b_router, w_gate_up, b_gate_up, w_down, b_down, tables):
    bsz, seq, d = x.shape
    t = bsz * seq
    cs, m1, m2 = tables
    row = lambda v: v.reshape(1, -1)

    kmem, vmem_ = _memkv(mem, row(norm_mem), w_k.astype(BF16), w_v.astype(BF16))

    bg, cv, a, s = _inproj(x.reshape(t, d), row(norm_mix), w_in.astype(BF16), cs)
    cols = FFT_N2 * FFT_WIDTH
    g = _fft1(a.reshape(bsz, FFT_N1, cols), s.reshape(bsz, FFT_N1, cols), m1)
    yf = _fft2(g.reshape(bsz, 2, FFT_N1, FFT_N2, FFT_WIDTH), m2).reshape(bsz, seq, FFT_WIDTH)

    w_out_b = w_out.astype(BF16)
    wr_t = w_router.T
    wr_hi = wr_t.astype(BF16)
    wr_lo = (wr_t - wr_hi.astype(F32)).astype(BF16)
    tri = (jnp.arange(TOK_TILE)[:, None] < jnp.arange(TOK_TILE)[None, :]).astype(BF16)
    tri32 = (jnp.arange(N_EXPERTS)[None, :] < jnp.arange(N_EXPERTS)[:, None]).astype(BF16)
    x2, h3, w_tiles, q_tiles, cnt = _post(
        x, bg.reshape(bsz, seq, CONV_WIDTH), cv.reshape(bsz, seq, CONV_WIDTH), yf,
        conv_w, row(g_conv_out), row(g_fft_out), w_out_b[:CONV_WIDTH], w_out_b[CONV_WIDTH:],
        row(norm_xattn), w_q.astype(BF16), kmem, vmem_, w_o.astype(BF16), row(norm_ffn),
        wr_hi, wr_lo, jnp.broadcast_to(b_router[:, None], (N_EXPERTS, 128)), tri, tri32)

    n_tiles = t // TOK_TILE
    n_blocks = (t * TOP_K + n_tiles * N_EXPERTS * (OCT_ROWS - 1)) // ROW_BLOCK + N_EXPERTS
    counts = cnt[:, :, 0].astype(I32).reshape(-1)
    src, dst, block_e, n_used, w_next, w_par = _plan(counts, n_tiles, n_blocks)
    per_block = lambda v: v.reshape(n_blocks + 2, 1, OCT_PER_BLOCK)
    xs = _localsort(q_tiles, h3)
    ybuf = _experts(block_e, n_used, w_next, w_par, per_block(src), per_block(dst), xs,
                    w_gate_up, b_gate_up[:, None, :], w_down, b_down[:, None, :])
    return x2.reshape(t, d), ybuf, q_tiles, w_tiles


def kernel(x, mem, norm_mix, w_in, conv_w, g_conv_out, g_fft_out, w_out, norm_xattn, norm_mem, w_q, w_k, w_v, w_o,
           norm_ffn, w_router, b_router, w_gate_up, b_gate_up, w_down, b_down, norm_final):
    bsz, seq, d = x.shape
    depth = norm_mix.shape[0]
    assert depth == 1, "final norm is fused into the combine step of the single layer"
    tables = _dft_tables(seq)
    x2, ybuf, q_tiles, w_tiles = _layer(
        x, mem, norm_mix[0], w_in[0], conv_w[0], g_conv_out[0], g_fft_out[0], w_out[0], norm_xattn[0],
        norm_mem[0], w_q[0], w_k[0], w_v[0], w_o[0], norm_ffn[0], w_router[0], b_router[0],
        w_gate_up[0], b_gate_up[0], w_down[0], b_down[0], tables)
    out = _combine(q_tiles, w_tiles, ybuf, x2, norm_final.reshape(1, -1))
    return out.reshape(bsz, seq, d)
```

```python
import functools
import math

import numpy as np
import jax
import jax.numpy as jnp
from jax import lax
from jax.experimental import pallas as pl
from jax.experimental.pallas import tpu as pltpu

F32 = jnp.float32
BF16 = jnp.bfloat16
I32 = jnp.int32

D_MODEL = 1024
CONV_WIDTH = 512
FFT_WIDTH = 512
GROUP_DIM = 64
IN_PROJ_WIDTH = 3 * CONV_WIDTH + FFT_WIDTH
MEM_LEN = 256
XATTN_HEADS = 4
XATTN_HEAD_DIM = D_MODEL // XATTN_HEADS
N_EXPERTS = 32
TOP_K = 4
D_EXPERT = D_MODEL
SWIGLU_LIMIT = 7.0
SWIGLU_ALPHA = 1.702
EPS = 1e-5

FFT_N1 = 64
FFT_N2 = 128

TOK_TILE = 512
ROW_BLOCK = 512
ROW_DMA_PRIORITY = 0
WEIGHT_DMA_PRIORITY = 1
OCT_ROWS = 8
OCT_PER_BLOCK = ROW_BLOCK // OCT_ROWS
PLAN_UNROLL = 8
TILE_OCT = (TOK_TILE * TOP_K) // OCT_ROWS + N_EXPERTS
FFT1_LANES = 8192
FFT2_K1 = 8
BF16_SUBLANES = 16
TILE_SUBLANES = 8
LANES = 128
assert D_MODEL == TILE_SUBLANES * LANES
VMEM_LIMIT = 56 * 1024 * 1024


def _rms(x, g):
    return x * lax.rsqrt(jnp.mean(x * x, axis=-1, keepdims=True) + EPS) * g


def _dot(a, b):
    return jnp.dot(a, b, preferred_element_type=F32)


def _dot_nt(a, b):
    return lax.dot_general(a, b, (((1,), (1,)), ((), ())), preferred_element_type=F32)


def _load_token_tiles(ref, rows, base=0):
    return jnp.concatenate(
        [ref[pl.ds(base + s, rows, stride=TILE_SUBLANES), :] for s in range(TILE_SUBLANES)], axis=-1)


def _store_token_tiles(ref, val, rows):
    for s in range(TILE_SUBLANES):
        ref[pl.ds(s, rows, stride=TILE_SUBLANES), :] = val[:, s * LANES:(s + 1) * LANES]


def _dft_tables(seq):
    assert seq == FFT_N1 * FFT_N2
    c = np.arange(GROUP_DIM)
    ang = 2.0 * np.pi * ((c[:, None] * c[None, :]) % GROUP_DIM) / GROUP_DIM
    groups = FFT_WIDTH // GROUP_DIM
    eye = np.eye(groups)
    cs = np.concatenate([np.kron(eye, np.cos(ang)), np.kron(eye, np.sin(ang))], axis=1) / math.sqrt(GROUP_DIM)
    n1 = np.arange(FFT_N1)
    a1 = 2.0 * np.pi * ((n1[:, None] * n1[None, :]) % FFT_N1) / FFT_N1
    c1, s1 = np.cos(a1), np.sin(a1)
    m1 = np.block([[c1, -s1], [s1, c1]]) / math.sqrt(FFT_N1)
    k1 = np.arange(FFT_N1)[:, None, None]
    k2 = np.arange(FFT_N2)[None, :, None]
    n2 = np.arange(FFT_N2)[None, None, :]
    a2 = 2.0 * np.pi * ((n2 * (k1 + FFT_N1 * k2)) % seq) / seq
    m2 = np.concatenate([np.cos(a2), -np.sin(a2)], axis=2) / math.sqrt(FFT_N2)
    return (jnp.asarray(cs, F32).astype(BF16), jnp.asarray(m1, F32).astype(BF16),
            jnp.asarray(m2, F32).astype(BF16))


def _memkv_kernel(mem_ref, g_ref, wk_ref, wv_ref, k_ref, v_ref):
    m = _rms(mem_ref[0], g_ref[...]).astype(BF16)
    k_ref[0] = _dot(m, wk_ref[...]).astype(BF16)
    v_ref[0] = _dot(m, wv_ref[...]).astype(BF16)


def _memkv(mem, g, wk, wv):
    bsz = mem.shape[0]
    full = lambda shape: pl.BlockSpec(shape, lambda b: (0,) * len(shape))
    per_b = pl.BlockSpec((1, MEM_LEN, D_MODEL), lambda b: (b, 0, 0))
    return pl.pallas_call(
        _memkv_kernel,
        grid=(bsz,),
        in_specs=[per_b, full((1, D_MODEL)), full((D_MODEL, D_MODEL)), full((D_MODEL, D_MODEL))],
        out_specs=[per_b, per_b],
        out_shape=[jax.ShapeDtypeStruct((bsz, MEM_LEN, D_MODEL), BF16)] * 2,
        compiler_params=pltpu.CompilerParams(dimension_semantics=("arbitrary",), vmem_limit_bytes=VMEM_LIMIT),
        name="memkv",
    )(mem, g, wk, wv)


def _inproj_kernel(x_ref, g_ref, win_ref, cs_ref, b_ref, cv_ref, a_ref, s_ref):
    h = _rms(x_ref[...], g_ref[...]).astype(BF16)
    z = _dot(h, win_ref[...])
    b_ref[...] = z[:, :CONV_WIDTH].astype(BF16)
    cv_ref[...] = (z[:, CONV_WIDTH:2 * CONV_WIDTH] * z[:, 2 * CONV_WIDTH:3 * CONV_WIDTH]).astype(BF16)
    u = z[:, 3 * CONV_WIDTH:].astype(BF16)
    ab = _dot(u, cs_ref[...])
    a_ref[...] = ab[:, :FFT_WIDTH].astype(BF16)
    s_ref[...] = ab[:, FFT_WIDTH:].astype(BF16)


def _inproj(x2d, g, w_in, cs):
    t = x2d.shape[0]
    tile = pl.BlockSpec((TOK_TILE, D_MODEL), lambda i: (i, 0))
    half = pl.BlockSpec((TOK_TILE, CONV_WIDTH), lambda i: (i, 0))
    full = lambda shape: pl.BlockSpec(shape, lambda i: (0,) * len(shape))
    return pl.pallas_call(
        _inproj_kernel,
        grid=(t // TOK_TILE,),
        in_specs=[tile, full((1, D_MODEL)), full((D_MODEL, IN_PROJ_WIDTH)), full((FFT_WIDTH, 2 * FFT_WIDTH))],
        out_specs=[half] * 4,
        out_shape=[jax.ShapeDtypeStruct((t, CONV_WIDTH), BF16)] * 4,
        compiler_params=pltpu.CompilerParams(dimension_semantics=("parallel",), vmem_limit_bytes=VMEM_LIMIT),
        name="inproj",
    )(x2d, g, w_in, cs)


def _fft1_kernel(a_ref, s_ref, m1_ref, g_ref):
    x = jnp.concatenate([a_ref[0], s_ref[0]], axis=0)
    g_ref[0] = _dot(m1_ref[...], x).astype(BF16)


def _fft1(a3, s3, m1):
    bsz, _, cols = a3.shape
    blk = pl.BlockSpec((1, FFT_N1, FFT1_LANES), lambda b, j: (b, 0, j))
    return pl.pallas_call(
        _fft1_kernel,
        grid=(bsz, cols // FFT1_LANES),
        in_specs=[blk, blk, pl.BlockSpec((2 * FFT_N1, 2 * FFT_N1), lambda b, j: (0, 0))],
        out_specs=pl.BlockSpec((1, 2 * FFT_N1, FFT1_LANES), lambda b, j: (b, 0, j)),
        out_shape=jax.ShapeDtypeStruct((bsz, 2 * FFT_N1, cols), BF16),
        compiler_params=pltpu.CompilerParams(dimension_semantics=("parallel", "parallel"),
                                             vmem_limit_bytes=VMEM_LIMIT),
        name="fft1",
    )(a3, s3, m1)


def _fft2_kernel(g_ref, m2_ref, y_ref):
    for j in range(FFT2_K1):
        x = jnp.concatenate([g_ref[0, 0, j], g_ref[0, 1, j]], axis=0)
        y_ref[0, :, j * FFT_WIDTH:(j + 1) * FFT_WIDTH] = _dot(m2_ref[j], x).astype(BF16)


def _fft2(g5, m2):
    bsz = g5.shape[0]
    return pl.pallas_call(
        _fft2_kernel,
        grid=(bsz, FFT_N1 // FFT2_K1),
        in_specs=[pl.BlockSpec((1, 2, FFT2_K1, FFT_N2, FFT_WIDTH), lambda b, j: (b, 0, j, 0, 0)),
                  pl.BlockSpec((FFT2_K1, FFT_N2, 2 * FFT_N2), lambda b, j: (j, 0, 0))],
        out_specs=pl.BlockSpec((1, FFT_N2, FFT2_K1 * FFT_WIDTH), lambda b, j: (b, 0, j)),
        out_shape=jax.ShapeDtypeStruct((bsz, FFT_N2, FFT_N1 * FFT_WIDTH), BF16),
        compiler_params=pltpu.CompilerParams(dimension_semantics=("parallel", "parallel"),
                                             vmem_limit_bytes=VMEM_LIMIT),
        name="fft2",
    )(g5, m2)


def _post_kernel(x_ref, bg_ref, cv_ref, cvp_ref, cvn_ref, yf_ref, convw_ref, gc_ref, gf_ref,
                 wot_ref, wob_ref, nx_ref, wq_ref, k_ref, v_ref, wo_ref, nf_ref,
                 wrh_ref, wrl_ref, br_ref, tri_ref, tri32_ref,
                 x2_ref, h3_ref, w_ref, q_ref, cnt_ref):
    i = pl.program_id(1)
    last = pl.num_programs(1) - 1

    cv = cv_ref[0].astype(F32)
    prev_row = jnp.where(i > 0, cvp_ref[0].astype(F32)[BF16_SUBLANES - 1:BF16_SUBLANES, :], 0.0)
    next_row = jnp.where(i < last, cvn_ref[0].astype(F32)[0:1, :], 0.0)
    rows = lax.broadcasted_iota(I32, cv.shape, 0)
    cvm1 = jnp.where(rows == 0, prev_row, pltpu.roll(cv, 1, axis=0))
    cvp1 = jnp.where(rows == TOK_TILE - 1, next_row, pltpu.roll(cv, TOK_TILE - 1, axis=0))
    cw = convw_ref[...]
    y_conv = bg_ref[0].astype(F32) * (cw[0:1] * cvm1 + cw[1:2] * cv + cw[2:3] * cvp1)
    yc_n = _rms(y_conv, gc_ref[...]).astype(BF16)
    yf_n = _rms(yf_ref[0].astype(F32), gf_ref[...]).astype(BF16)
    x1 = x_ref[0] + _dot(yc_n, wot_ref[...]) + _dot(yf_n, wob_ref[...])

    h2 = _rms(x1, nx_ref[...]).astype(BF16)
    q = _dot(h2, wq_ref[...]).astype(BF16)
    heads = []
    for hd in range(XATTN_HEADS):
        sl = slice(hd * XATTN_HEAD_DIM, (hd + 1) * XATTN_HEAD_DIM)
        s = _dot_nt(q[:, sl], k_ref[0, :, sl]) * (XATTN_HEAD_DIM ** -0.5)
        e = jnp.exp(s - jnp.max(s, axis=-1, keepdims=True))
        p = e * (1.0 / jnp.sum(e, axis=-1, keepdims=True))
        heads.append(_dot(p.astype(BF16), v_ref[0, :, sl]).astype(BF16))
    x2 = x1 + _dot(jnp.concatenate(heads, axis=-1), wo_ref[...])
    x2_ref[0] = x2

    h3 = _rms(x2, nf_ref[...])
    _store_token_tiles(h3_ref, h3, TOK_TILE)
    h3h = h3.astype(BF16)
    h3l = (h3 - h3h.astype(F32)).astype(BF16)
    logits = (_dot_nt(wrh_ref[...], h3h) + _dot_nt(wrh_ref[...], h3l) + _dot_nt(wrl_ref[...], h3h)
              + br_ref[:, 0:1])
    eidx = lax.broadcasted_iota(I32, logits.shape, 0)
    work = logits
    vals, idxs = [], []
    for _ in range(TOP_K):
        m = jnp.max(work, axis=0, keepdims=True)
        ik = jnp.min(jnp.where(work == m, eidx, N_EXPERTS), axis=0, keepdims=True)
        vals.append(m)
        idxs.append(ik)
        work = jnp.where(eidx == ik, -jnp.inf, work)
    ex = [jnp.exp(v - vals[0]) for v in vals]
    inv_den = 1.0 / (ex[0] + ex[1] + ex[2] + ex[3])
    w_ref[0] = jnp.concatenate([e * inv_den for e in ex], axis=0)

    sel = jnp.zeros(logits.shape, F32)
    for ik in idxs:
        sel = sel + jnp.where(eidx == ik, 1.0, 0.0)
    cnt = jnp.broadcast_to(jnp.sum(sel, axis=1, keepdims=True), (N_EXPERTS, LANES))
    cnt_ref[0] = cnt
    seg_rows = jnp.floor((cnt + (OCT_ROWS - 1)) * (1.0 / OCT_ROWS)) * OCT_ROWS
    seg_start = _dot(tri32_ref[...], seg_rows.astype(BF16))
    pos = seg_start[:, 0:1] + _dot(sel.astype(BF16), tri_ref[...])
    q_ref[0] = jnp.concatenate(
        [jnp.sum(jnp.where(eidx == ik, pos, 0.0), axis=0, keepdims=True) for ik in idxs], axis=0).astype(I32)


def _post(x3, bg, cv, yf, conv_w, gc, gf, wo_top, wo_bot, nx, wq, kmem, vmem_, wo, nf, wrh, wrl, br, tri, tri32):
    bsz, seq, _ = x3.shape
    nt = seq // TOK_TILE
    t = bsz * seq
    n_tiles = bsz * nt
    halo_per_tile = TOK_TILE // BF16_SUBLANES
    n_halo = seq // BF16_SUBLANES
    full = lambda shape: pl.BlockSpec(shape, lambda b, i: (0,) * len(shape))
    tile_d = pl.BlockSpec((1, TOK_TILE, D_MODEL), lambda b, i: (b, i, 0))
    tile_h = pl.BlockSpec((1, TOK_TILE, CONV_WIDTH), lambda b, i: (b, i, 0))
    halo_prev = pl.BlockSpec((1, BF16_SUBLANES, CONV_WIDTH),
                             lambda b, i: (b, jnp.maximum(i * halo_per_tile - 1, 0), 0))
    halo_next = pl.BlockSpec((1, BF16_SUBLANES, CONV_WIDTH),
                             lambda b, i: (b, jnp.minimum((i + 1) * halo_per_tile, n_halo - 1), 0))
    mem_blk = pl.BlockSpec((1, MEM_LEN, D_MODEL), lambda b, i: (b, 0, 0))
    tok4 = pl.BlockSpec((1, TOP_K, TOK_TILE), lambda b, i: (b * nt + i, 0, 0))
    per_tile = pl.BlockSpec((1, N_EXPERTS, LANES), lambda b, i: (b * nt + i, 0, 0))
    tok_tiles = pl.BlockSpec((TOK_TILE * TILE_SUBLANES, LANES), lambda b, i: (b * nt + i, 0))
    return pl.pallas_call(
        _post_kernel,
        grid=(bsz, nt),
        in_specs=[tile_d, tile_h, tile_h, halo_prev, halo_next, tile_h,
                  full((3, CONV_WIDTH)), full((1, CONV_WIDTH)), full((1, FFT_WIDTH)),
                  full((CONV_WIDTH, D_MODEL)), full((FFT_WIDTH, D_MODEL)), full((1, D_MODEL)),
                  full((D_MODEL, D_MODEL)), mem_blk, mem_blk, full((D_MODEL, D_MODEL)), full((1, D_MODEL)),
                  full((N_EXPERTS, D_MODEL)), full((N_EXPERTS, D_MODEL)), full((N_EXPERTS, 128)),
                  full((TOK_TILE, TOK_TILE)), full((N_EXPERTS, N_EXPERTS))],
        out_specs=[tile_d, tok_tiles, tok4, tok4, per_tile],
        out_shape=[jax.ShapeDtypeStruct((bsz, seq, D_MODEL), F32),
                   jax.ShapeDtypeStruct((t * TILE_SUBLANES, LANES), F32),
                   jax.ShapeDtypeStruct((n_tiles, TOP_K, TOK_TILE), F32),
                   jax.ShapeDtypeStruct((n_tiles, TOP_K, TOK_TILE), I32),
                   jax.ShapeDtypeStruct((n_tiles, N_EXPERTS, LANES), F32)],
        compiler_params=pltpu.CompilerParams(dimension_semantics=("parallel", "parallel"),
                                             vmem_limit_bytes=VMEM_LIMIT),
        name="post",
    )(x3, bg, cv, cv, cv, yf, conv_w, gc, gf, wo_top, wo_bot, nx, wq, kmem, vmem_, wo, nf, wrh, wrl, br, tri, tri32)


def _plan_kernel(cnt_ref, src_ref, dst_ref, be_ref, nu_ref, wnext_ref, wpar_ref, off_ref, nxt_ref, *, n_tiles):
    n_sched = be_ref.shape[0]
    dump0 = n_tiles * TILE_OCT

    def fill_pad(lo, hi):
        def body(p, c):
            o = p & (OCT_PER_BLOCK - 1)
            parity = lax.shift_right_logical(p, OCT_PER_BLOCK.bit_length() - 1) & 1
            src_ref[p] = dump0 + 2 * OCT_PER_BLOCK + o
            dst_ref[p] = dump0 + parity * OCT_PER_BLOCK + o
            return c
        lax.fori_loop(lo, hi, body, 0)

    def clear(tile, c):
        off_ref[tile] = 0
        return c

    lax.fori_loop(0, n_tiles, clear, 0)

    def find_next(i, nxt):
        e = N_EXPERTS - 1 - i
        nxt_ref[e] = nxt
        total = lax.fori_loop(0, n_tiles, lambda tile, s: s + cnt_ref[tile * N_EXPERTS + e], jnp.int32(0))
        return jnp.where(total > 0, e, nxt)

    lax.fori_loop(0, N_EXPERTS, find_next, jnp.int32(-1))
    fill_pad(0, OCT_PER_BLOCK)

    def per_expert(e, carry):
        pos0, blk0, group = carry

        def per_tile(tile, pos):
            n_oct = lax.shift_right_logical(cnt_ref[tile * N_EXPERTS + e] + (OCT_ROWS - 1), OCT_ROWS.bit_length() - 1)
            base = tile * TILE_OCT + off_ref[tile]
            off_ref[tile] = off_ref[tile] + n_oct

            def per_group(grp, c):
                for u in range(PLAN_UNROLL):
                    o = grp * PLAN_UNROLL + u
                    src_ref[pos + o] = base + o
                    dst_ref[pos + o] = base + o
                return c

            lax.fori_loop(0, lax.shift_right_logical(n_oct + (PLAN_UNROLL - 1), PLAN_UNROLL.bit_length() - 1),
                          per_group, 0)
            return pos + n_oct

        pos1 = lax.fori_loop(0, n_tiles, per_tile, pos0)
        nb = lax.shift_right_logical(pos1 - pos0 + (OCT_PER_BLOCK - 1), OCT_PER_BLOCK.bit_length() - 1)
        pos2 = pos0 + nb * OCT_PER_BLOCK
        fill_pad(pos1, pos2)

        def fill(j, c):
            be_ref[blk0 + j] = e
            wnext_ref[blk0 + j] = nxt_ref[e]
            wpar_ref[blk0 + j] = group & 1
            return c

        lax.fori_loop(0, nb, fill, 0)
        return pos2, blk0 + nb, group + jnp.where(nb > 0, 1, 0)

    pos, n_used, _ = lax.fori_loop(0, N_EXPERTS, per_expert,
                                   (jnp.int32(OCT_PER_BLOCK), jnp.int32(0), jnp.int32(0)))
    nu_ref[0] = n_used
    tail_e = be_ref[n_used - 1]

    def tail(j, c):
        be_ref[j] = tail_e
        wnext_ref[j] = -1
        wpar_ref[j] = 0
        return c

    lax.fori_loop(n_used, n_sched, tail, 0)
    fill_pad(pos, src_ref.shape[0])


def _plan(counts, n_tiles, n_blocks):
    smem = pl.BlockSpec(memory_space=pltpu.SMEM)
    grid_spec = pltpu.PrefetchScalarGridSpec(
        num_scalar_prefetch=1,
        grid=(1,),
        in_specs=[],
        out_specs=[smem] * 6,
        scratch_shapes=[pltpu.SMEM((n_tiles,), I32), pltpu.SMEM((N_EXPERTS,), I32)],
    )
    octs = jax.ShapeDtypeStruct(((n_blocks + 2) * OCT_PER_BLOCK,), I32)
    sched = jax.ShapeDtypeStruct((n_blocks + 1,), I32)
    return pl.pallas_call(
        functools.partial(_plan_kernel, n_tiles=n_tiles),
        grid_spec=grid_spec,
        out_shape=[octs, octs, sched, jax.ShapeDtypeStruct((1,), I32), sched, sched],
        compiler_params=pltpu.CompilerParams(dimension_semantics=("arbitrary",), vmem_limit_bytes=VMEM_LIMIT),
        name="plan",
    )(counts)


def _localsort_kernel(q_ref, h3_ref, xs_ref):
    xs_ref[...] = jnp.zeros_like(xs_ref)

    @pl.when(pl.program_id(0) < pl.num_programs(0) - 1)
    def _():
        def group(g, c):
            for u in range(TILE_SUBLANES):
                t = g * TILE_SUBLANES + u
                row = _tile_rows(h3_ref, t)[...]
                for k in range(TOP_K):
                    _tile_rows(xs_ref, q_ref[k, t])[...] = row
            return c

        lax.fori_loop(0, TOK_TILE // TILE_SUBLANES, group, 0)


def _localsort(q_tiles, h3t):
    n_tiles = q_tiles.shape[0]
    last = n_tiles - 1
    region = TILE_OCT * OCT_ROWS * TILE_SUBLANES
    return pl.pallas_call(
        _localsort_kernel,
        grid=(n_tiles + 1,),
        in_specs=[pl.BlockSpec((None, TOP_K, TOK_TILE), lambda i: (jnp.minimum(i, last), 0, 0),
                               memory_space=pltpu.SMEM),
                  pl.BlockSpec((TOK_TILE * TILE_SUBLANES, LANES), lambda i: (jnp.minimum(i, last), 0))],
        out_specs=pl.BlockSpec((region, LANES), lambda i: (i, 0)),
        out_shape=jax.ShapeDtypeStruct(((n_tiles + 1) * region, LANES), F32),
        compiler_params=pltpu.CompilerParams(dimension_semantics=("parallel",), vmem_limit_bytes=VMEM_LIMIT),
        name="localsort",
    )(q_tiles, h3t)


def _tile_rows(ref, row, n_rows=1):
    return ref.at[pl.ds(pl.multiple_of(row * TILE_SUBLANES, TILE_SUBLANES), n_rows * TILE_SUBLANES), :]


def _expert_kernel(be_ref, nu_ref, wnext_ref, wpar_ref,
                   src_cur_ref, src_next_ref, dst_prev_ref, bgu_ref, bd_ref, xs_ref, wgu_hbm, wd_hbm,
                   y_ref,
                   xbuf0, xbuf1, obuf0, obuf1, wgu_f32, wd_f32, wgu_bf, wd_bf, gsem, ssem, wsem):
    j = pl.program_id(0)
    nu = nu_ref[0]
    xbuf = (xbuf0, xbuf1)
    obuf = (obuf0, obuf1)
    octet = lambda ref, o: _tile_rows(ref, o * OCT_ROWS, OCT_ROWS)

    def gather_octet(src_ref, o, dst):
        return pltpu.make_async_copy(octet(xs_ref, src_ref[0, o]), octet(dst, o), gsem)

    def scatter_octet(dst_ref, o, src):
        return pltpu.make_async_copy(octet(src, o), octet(y_ref, dst_ref[0, o]), ssem)

    def weight_copies(e, p):
        return (pltpu.make_async_copy(wgu_hbm.at[e], wgu_f32.at[p], wsem.at[p, 0]),
                pltpu.make_async_copy(wd_hbm.at[e], wd_f32.at[p], wsem.at[p, 1]))

    def wait_gather(dst):
        for _ in range(OCT_PER_BLOCK):
            pltpu.make_async_copy(octet(xs_ref, 0), octet(dst, 0), gsem).wait()

    def wait_scatter(src):
        for _ in range(OCT_PER_BLOCK):
            pltpu.make_async_copy(octet(src, 0), octet(y_ref, 0), ssem).wait()

    @pl.when(j == 0)
    def _():
        obuf1[...] = jnp.zeros_like(obuf1)
        for c in weight_copies(be_ref[0], 0):
            c.start(priority=WEIGHT_DMA_PRIORITY)
        for o in range(OCT_PER_BLOCK):
            gather_octet(src_cur_ref, o, xbuf0).start(priority=ROW_DMA_PRIORITY)

    first_of_group = jnp.logical_or(j == 0, be_ref[j] != be_ref[jnp.maximum(j - 1, 0)])

    @pl.when(jnp.logical_and(j < nu, first_of_group))
    def _():
        p = wpar_ref[j]
        for c in weight_copies(be_ref[j], p):
            c.wait()
        wgu_bf[...] = wgu_f32[p].astype(BF16)
        wd_bf[...] = wd_f32[p].astype(BF16)

        @pl.when(wnext_ref[j] >= 0)
        def _():
            for c in weight_copies(wnext_ref[j], 1 - p):
                c.start(priority=WEIGHT_DMA_PRIORITY)

    def step(s, compute):
        wait_gather(xbuf[s])

        @pl.when(j >= 1)
        def _():
            wait_scatter(obuf[s])

        if compute:
            for o in range(OCT_PER_BLOCK):
                gather_octet(src_next_ref, o, xbuf[1 - s]).start(priority=ROW_DMA_PRIORITY)
        for o in range(OCT_PER_BLOCK):
            scatter_octet(dst_prev_ref, o, obuf[1 - s]).start(priority=ROW_DMA_PRIORITY)
        if compute:
            x = _load_token_tiles(xbuf[s], ROW_BLOCK).astype(BF16)
            gu = _dot(x, wgu_bf[...]) + bgu_ref[0]
            gate = jnp.minimum(gu[:, :D_EXPERT], SWIGLU_LIMIT)
            up = jnp.clip(gu[:, D_EXPERT:], -SWIGLU_LIMIT, SWIGLU_LIMIT)
            glu = gate * (1.0 / (1.0 + jnp.exp(-SWIGLU_ALPHA * gate)))
            h = ((up + 1.0) * glu).astype(BF16)
            _store_token_tiles(obuf[s], _dot(h, wd_bf[...]) + bd_ref[0], ROW_BLOCK)
        else:
            wait_scatter(obuf[1 - s])

    for s in range(2):
        parity = (j & 1) == s
        pl.when(jnp.logical_and(j < nu, parity))(functools.partial(step, s, True))
        pl.when(jnp.logical_and(j == nu, parity))(functools.partial(step, s, False))


def _experts(block_e, n_used, w_next, w_par, src3, dst3, xs, wgu, bgu, wd, bd):
    n_steps = block_e.shape[0]
    blk_rows = ROW_BLOCK * TILE_SUBLANES
    last_blk = src3.shape[0] - 1
    exp_map = lambda j, be, nu, wn, wp: (be[j], 0, 0)
    oct_spec = lambda off: pl.BlockSpec((None, 1, OCT_PER_BLOCK),
                                        lambda j, be, nu, wn, wp: (jnp.minimum(j + off, last_blk), 0, 0),
                                        memory_space=pltpu.SMEM)
    hbm = pl.BlockSpec(memory_space=pl.ANY)
    grid_spec = pltpu.PrefetchScalarGridSpec(
        num_scalar_prefetch=4,
        grid=(n_steps,),
        in_specs=[oct_spec(1), oct_spec(2), oct_spec(0),
                  pl.BlockSpec((1, 1, 2 * D_EXPERT), exp_map), pl.BlockSpec((1, 1, D_MODEL), exp_map),
                  hbm, hbm, hbm],
        out_specs=hbm,
        scratch_shapes=[pltpu.VMEM((blk_rows, LANES), F32)] * 4
        + [pltpu.VMEM((2, D_MODEL, 2 * D_EXPERT), F32), pltpu.VMEM((2, D_EXPERT, D_MODEL), F32),
           pltpu.VMEM((D_MODEL, 2 * D_EXPERT), BF16), pltpu.VMEM((D_EXPERT, D_MODEL), BF16),
           pltpu.SemaphoreType.DMA, pltpu.SemaphoreType.DMA, pltpu.SemaphoreType.DMA((2, 2))],
    )
    return pl.pallas_call(
        _expert_kernel,
        grid_spec=grid_spec,
        out_shape=jax.ShapeDtypeStruct(xs.shape, F32),
        input_output_aliases={9: 0},
        compiler_params=pltpu.CompilerParams(dimension_semantics=("arbitrary",), vmem_limit_bytes=VMEM_LIMIT),
        name="experts",
    )(block_e, n_used, w_next, w_par, src3, src3, dst3, bgu, bd, xs, wgu, wd)


def _combine_kernel(q_ref, w_ref, y_ref, x2_ref, g_ref, out_ref, acc_ref):
    def group(grp, c):
        for u in range(TILE_SUBLANES):
            t = grp * TILE_SUBLANES + u
            acc = w_ref[0, t] * _tile_rows(y_ref, q_ref[0, t])[...]
            for k in range(1, TOP_K):
                acc = acc + w_ref[k, t] * _tile_rows(y_ref, q_ref[k, t])[...]
            _tile_rows(acc_ref, t)[...] = acc
        return c

    lax.fori_loop(0, TOK_TILE // TILE_SUBLANES, group, 0)
    out_ref[...] = _rms(x2_ref[...] + _load_token_tiles(acc_ref, TOK_TILE), g_ref[...])


def _combine(q_tiles, w_tiles, ybuf, x2, g):
    n_tiles = q_tiles.shape[0]
    region = TILE_OCT * OCT_ROWS * TILE_SUBLANES
    tok4 = pl.BlockSpec((None, TOP_K, TOK_TILE), lambda i: (i, 0, 0), memory_space=pltpu.SMEM)
    return pl.pallas_call(
        _combine_kernel,
        grid=(n_tiles,),
        in_specs=[tok4, tok4,
                  pl.BlockSpec((region, LANES), lambda i: (i, 0)),
                  pl.BlockSpec((TOK_TILE, D_MODEL), lambda i: (i, 0)),
                  pl.BlockSpec((1, D_MODEL), lambda i: (0, 0))],
        out_specs=pl.BlockSpec((TOK_TILE, D_MODEL), lambda i: (i, 0)),
        out_shape=jax.ShapeDtypeStruct(x2.shape, F32),
        scratch_shapes=[pltpu.VMEM((TOK_TILE * TILE_SUBLANES, LANES), F32)],
        compiler_params=pltpu.CompilerParams(dimension_semantics=("parallel",), vmem_limit_bytes=VMEM_LIMIT),
        name="combine",
    )(q_tiles, w_tiles, ybuf, x2, g)


def _layer(x, mem, norm_mix, w_in, conv_w, g_conv_out, g_fft_out, w_out, norm_xattn, norm_mem,
           w_q, w_k, w_v, w_o, norm_ffn, w_router, b_router, w_gate_up, b_gate_up, w_down, b_down, tables):
    bsz, seq, d = x.shape
    t = bsz * seq
    cs, m1, m2 = tables
    row = lambda v: v.reshape(1, -1)

    kmem, vmem_ = _memkv(mem, row(norm_mem), w_k.astype(BF16), w_v.astype(BF16))

    bg, cv, a, s = _inproj(x.reshape(t, d), row(norm_mix), w_in.astype(BF16), cs)
    cols = FFT_N2 * FFT_WIDTH
    g = _fft1(a.reshape(bsz, FFT_N1, cols), s.reshape(bsz, FFT_N1, cols), m1)
    yf = _fft2(g.reshape(bsz, 2, FFT_N1, FFT_N2, FFT_WIDTH), m2).reshape(bsz, seq, FFT_WIDTH)

    w_out_b = w_out.astype(BF16)
    wr_t = w_router.T
    wr_hi = wr_t.astype(BF16)
    wr_lo = (wr_t - wr_hi.astype(F32)).astype(BF16)
    tri = (jnp.arange(TOK_TILE)[:, None] < jnp.arange(TOK_TILE)[None, :]).astype(BF16)
    tri32 = (jnp.arange(N_EXPERTS)[None, :] < jnp.arange(N_EXPERTS)[:, None]).astype(BF16)
    x2, h3, w_tiles, q_tiles, cnt = _post(
        x, bg.reshape(bsz, seq, CONV_WIDTH), cv.reshape(bsz, seq, CONV_WIDTH), yf,
        conv_w, row(g_conv_out), row(g_fft_out), w_out_b[:CONV_WIDTH], w_out_b[CONV_WIDTH:],
        row(norm_xattn), w_q.astype(BF16), kmem, vmem_, w_o.astype(BF16), row(norm_ffn),
        wr_hi, wr_lo, jnp.broadcast_to(b_router[:, None], (N_EXPERTS, 128)), tri, tri32)

    n_tiles = t // TOK_TILE
    n_blocks = (t * TOP_K + n_tiles * N_EXPERTS * (OCT_ROWS - 1)) // ROW_BLOCK + N_EXPERTS
    counts = cnt[:, :, 0].astype(I32).reshape(-1)
    src, dst, block_e, n_used, w_next, w_par = _plan(counts, n_tiles, n_blocks)
    per_block = lambda v: v.reshape(n_blocks + 2, 1, OCT_PER_BLOCK)
    xs = _localsort(q_tiles, h3)
    ybuf = _experts(block_e, n_used, w_next, w_par, per_block(src), per_block(dst), xs,
                    w_gate_up, b_gate_up[:, None, :], w_down, b_down[:, None, :])
    return x2.reshape(t, d), ybuf, q_tiles, w_tiles


def kernel(x, mem, norm_mix, w_in, conv_w, g_conv_out, g_fft_out, w_out, norm_xattn, norm_mem, w_q, w_k, w_v, w_o,
           norm_ffn, w_router, b_router, w_gate_up, b_gate_up, w_down, b_down, norm_final):
    bsz, seq, d = x.shape
    depth = norm_mix.shape[0]
    assert depth == 1, "final norm is fused into the combine step of the single layer"
    tables = _dft_tables(seq)
    x2, ybuf, q_tiles, w_tiles = _layer(
        x, mem, norm_mix[0], w_in[0], conv_w[0], g_conv_out[0], g_fft_out[0], w_out[0], norm_xattn[0],
        norm_mem[0], w_q[0], w_k[0], w_v[0], w_o[0], norm_ffn[0], w_router[0], b_router[0],
        w_gate_up[0], b_gate_up[0], w_down[0], b_down[0], tables)
    out = _combine(q_tiles, w_tiles, ybuf, x2, norm_final.reshape(1, -1))
    return out.reshape(bsz, seq, d)
```

```python
import functools
import math

import numpy as np
import jax
import jax.numpy as jnp
from jax import lax
from jax.experimental import pallas as pl
from jax.experimental.pallas import tpu as pltpu

F32 = jnp.float32
BF16 = jnp.bfloat16
I32 = jnp.int32

D_MODEL = 1024
CONV_WIDTH = 512
FFT_WIDTH = 512
GROUP_DIM = 64
IN_PROJ_WIDTH = 3 * CONV_WIDTH + FFT_WIDTH
MEM_LEN = 256
XATTN_HEADS = 4
XATTN_HEAD_DIM = D_MODEL // XATTN_HEADS
N_EXPERTS = 32
TOP_K = 4
D_EXPERT = D_MODEL
SWIGLU_LIMIT = 7.0
SWIGLU_ALPHA = 1.702
EPS = 1e-5

FFT_N1 = 64
FFT_N2 = 128

TOK_TILE = 512
ROW_BLOCK = 512
ROW_DMA_PRIORITY = 0
WEIGHT_DMA_PRIORITY = 1
OCT_ROWS = 8
OCT_PER_BLOCK = ROW_BLOCK // OCT_ROWS
PLAN_UNROLL = 8
TILE_OCT = (TOK_TILE * TOP_K) // OCT_ROWS + N_EXPERTS
FFT1_LANES = 8192
FFT2_K1 = 8
BF16_SUBLANES = 16
TILE_SUBLANES = 8
LANES = 128
assert D_MODEL == TILE_SUBLANES * LANES
VMEM_LIMIT = 56 * 1024 * 1024


def _rms(x, g):
    return x * lax.rsqrt(jnp.mean(x * x, axis=-1, keepdims=True) + EPS) * g


def _dot(a, b):
    return jnp.dot(a, b, preferred_element_type=F32)


def _dot_nt(a, b):
    return lax.dot_general(a, b, (((1,), (1,)), ((), ())), preferred_element_type=F32)


def _load_token_tiles(ref, rows, base=0):
    return jnp.concatenate(
        [ref[pl.ds(base + s, rows, stride=TILE_SUBLANES), :] for s in range(TILE_SUBLANES)], axis=-1)


def _store_token_tiles(ref, val, rows):
    for s in range(TILE_SUBLANES):
        ref[pl.ds(s, rows, stride=TILE_SUBLANES), :] = val[:, s * LANES:(s + 1) * LANES]


def _dft_tables(seq):
    assert seq == FFT_N1 * FFT_N2
    c = np.arange(GROUP_DIM)
    ang = 2.0 * np.pi * ((c[:, None] * c[None, :]) % GROUP_DIM) / GROUP_DIM
    groups = FFT_WIDTH // GROUP_DIM
    eye = np.eye(groups)
    cs = np.concatenate([np.kron(eye, np.cos(ang)), np.kron(eye, np.sin(ang))], axis=1) / math.sqrt(GROUP_DIM)
    n1 = np.arange(FFT_N1)
    a1 = 2.0 * np.pi * ((n1[:, None] * n1[None, :]) % FFT_N1) / FFT_N1
    c1, s1 = np.cos(a1), np.sin(a1)
    m1 = np.block([[c1, -s1], [s1, c1]]) / math.sqrt(FFT_N1)
    k1 = np.arange(FFT_N1)[:, None, None]
    k2 = np.arange(FFT_N2)[None, :, None]
    n2 = np.arange(FFT_N2)[None, None, :]
    a2 = 2.0 * np.pi * ((n2 * (k1 + FFT_N1 * k2)) % seq) / seq
    m2 = np.concatenate([np.cos(a2), -np.sin(a2)], axis=2) / math.sqrt(FFT_N2)
    return (jnp.asarray(cs, F32).astype(BF16), jnp.asarray(m1, F32).astype(BF16),
            jnp.asarray(m2, F32).astype(BF16))


def _memkv_kernel(mem_ref, g_ref, wk_ref, wv_ref, k_ref, v_ref):
    m = _rms(mem_ref[0], g_ref[...]).astype(BF16)
    k_ref[0] = _dot(m, wk_ref[...]).astype(BF16)
    v_ref[0] = _dot(m, wv_ref[...]).astype(BF16)


def _memkv(mem, g, wk, wv):
    bsz = mem.shape[0]
    full = lambda shape: pl.BlockSpec(shape, lambda b: (0,) * len(shape))
    per_b = pl.BlockSpec((1, MEM_LEN, D_MODEL), lambda b: (b, 0, 0))
    return pl.pallas_call(
        _memkv_kernel,
        grid=(bsz,),
        in_specs=[per_b, full((1, D_MODEL)), full((D_MODEL, D_MODEL)), full((D_MODEL, D_MODEL))],
        out_specs=[per_b, per_b],
        out_shape=[jax.ShapeDtypeStruct((bsz, MEM_LEN, D_MODEL), BF16)] * 2,
        compiler_params=pltpu.CompilerParams(dimension_semantics=("arbitrary",), vmem_limit_bytes=VMEM_LIMIT),
        name="memkv",
    )(mem, g, wk, wv)


def _inproj_kernel(x_ref, g_ref, win_ref, cs_ref, b_ref, cv_ref, a_ref, s_ref):
    h = _rms(x_ref[...], g_ref[...]).astype(BF16)
    z = _dot(h, win_ref[...])
    b_ref[...] = z[:, :CONV_WIDTH].astype(BF16)
    cv_ref[...] = (z[:, CONV_WIDTH:2 * CONV_WIDTH] * z[:, 2 * CONV_WIDTH:3 * CONV_WIDTH]).astype(BF16)
    u = z[:, 3 * CONV_WIDTH:].astype(BF16)
    ab = _dot(u, cs_ref[...])
    a_ref[...] = ab[:, :FFT_WIDTH].astype(BF16)
    s_ref[...] = ab[:, FFT_WIDTH:].astype(BF16)


def _inproj(x2d, g, w_in, cs):
    t = x2d.shape[0]
    tile = pl.BlockSpec((TOK_TILE, D_MODEL), lambda i: (i, 0))
    half = pl.BlockSpec((TOK_TILE, CONV_WIDTH), lambda i: (i, 0))
    full = lambda shape: pl.BlockSpec(shape, lambda i: (0,) * len(shape))
    return pl.pallas_call(
        _inproj_kernel,
        grid=(t // TOK_TILE,),
        in_specs=[tile, full((1, D_MODEL)), full((D_MODEL, IN_PROJ_WIDTH)), full((FFT_WIDTH, 2 * FFT_WIDTH))],
        out_specs=[half] * 4,
        out_shape=[jax.ShapeDtypeStruct((t, CONV_WIDTH), BF16)] * 4,
        compiler_params=pltpu.CompilerParams(dimension_semantics=("parallel",), vmem_limit_bytes=VMEM_LIMIT),
        name="inproj",
    )(x2d, g, w_in, cs)


def _fft1_kernel(a_ref, s_ref, m1_ref, g_ref):
    x = jnp.concatenate([a_ref[0], s_ref[0]], axis=0)
    g_ref[0] = _dot(m1_ref[...], x).astype(BF16)


def _fft1(a3, s3, m1):
    bsz, _, cols = a3.shape
    blk = pl.BlockSpec((1, FFT_N1, FFT1_LANES), lambda b, j: (b, 0, j))
    return pl.pallas_call(
        _fft1_kernel,
        grid=(bsz, cols // FFT1_LANES),
        in_specs=[blk, blk, pl.BlockSpec((2 * FFT_N1, 2 * FFT_N1), lambda b, j: (0, 0))],
        out_specs=pl.BlockSpec((1, 2 * FFT_N1, FFT1_LANES), lambda b, j: (b, 0, j)),
        out_shape=jax.ShapeDtypeStruct((bsz, 2 * FFT_N1, cols), BF16),
        compiler_params=pltpu.CompilerParams(dimension_semantics=("parallel", "parallel"),
                                             vmem_limit_bytes=VMEM_LIMIT),
        name="fft1",
    )(a3, s3, m1)


def _fft2_kernel(g_ref, m2_ref, y_ref):
    for j in range(FFT2_K1):
        x = jnp.concatenate([g_ref[0, 0, j], g_ref[0, 1, j]], axis=0)
        y_ref[0, :, j * FFT_WIDTH:(j + 1) * FFT_WIDTH] = _dot(m2_ref[j], x).astype(BF16)


def _fft2(g5, m2):
    bsz = g5.shape[0]
    return pl.pallas_call(
        _fft2_kernel,
        grid=(bsz, FFT_N1 // FFT2_K1),
        in_specs=[pl.BlockSpec((1, 2, FFT2_K1, FFT_N2, FFT_WIDTH), lambda b, j: (b, 0, j, 0, 0)),
                  pl.BlockSpec((FFT2_K1, FFT_N2, 2 * FFT_N2), lambda b, j: (j, 0, 0))],
        out_specs=pl.BlockSpec((1, FFT_N2, FFT2_K1 * FFT_WIDTH), lambda b, j: (b, 0, j)),
        out_shape=jax.ShapeDtypeStruct((bsz, FFT_N2, FFT_N1 * FFT_WIDTH), BF16),
        compiler_params=pltpu.CompilerParams(dimension_semantics=("parallel", "parallel"),
                                             vmem_limit_bytes=VMEM_LIMIT),
        name="fft2",
    )(g5, m2)


def _post_kernel(x_ref, bg_ref, cv_ref, cvp_ref, cvn_ref, yf_ref, convw_ref, gc_ref, gf_ref,
                 wot_ref, wob_ref, nx_ref, wq_ref, k_ref, v_ref, wo_ref, nf_ref,
                 wrh_ref, wrl_ref, br_ref, tri_ref, tri32_ref,
                 x2_ref, h3_ref, w_ref, q_ref, cnt_ref):
    i = pl.program_id(1)
    last = pl.num_programs(1) - 1

    cv = cv_ref[0].astype(F32)
    prev_row = jnp.where(i > 0, cvp_ref[0].astype(F32)[BF16_SUBLANES - 1:BF16_SUBLANES, :], 0.0)
    next_row = jnp.where(i < last, cvn_ref[0].astype(F32)[0:1, :], 0.0)
    rows = lax.broadcasted_iota(I32, cv.shape, 0)
    cvm1 = jnp.where(rows == 0, prev_row, pltpu.roll(cv, 1, axis=0))
    cvp1 = jnp.where(rows == TOK_TILE - 1, next_row, pltpu.roll(cv, TOK_TILE - 1, axis=0))
    cw = convw_ref[...]
    y_conv = bg_ref[0].astype(F32) * (cw[0:1] * cvm1 + cw[1:2] * cv + cw[2:3] * cvp1)
    yc_n = _rms(y_conv, gc_ref[...]).astype(BF16)
    yf_n = _rms(yf_ref[0].astype(F32), gf_ref[...]).astype(BF16)
    x1 = x_ref[0] + _dot(yc_n, wot_ref[...]) + _dot(yf_n, wob_ref[...])

    h2 = _rms(x1, nx_ref[...]).astype(BF16)
    q = _dot(h2, wq_ref[...]).astype(BF16)
    heads = []
    for hd in range(XATTN_HEADS):
        sl = slice(hd * XATTN_HEAD_DIM, (hd + 1) * XATTN_HEAD_DIM)
        s = _dot_nt(q[:, sl], k_ref[0, :, sl]) * (XATTN_HEAD_DIM ** -0.5)
        e = jnp.exp(s - jnp.max(s, axis=-1, keepdims=True))
        p = e * (1.0 / jnp.sum(e, axis=-1, keepdims=True))
        heads.append(_dot(p.astype(BF16), v_ref[0, :, sl]).astype(BF16))
    x2 = x1 + _dot(jnp.concatenate(heads, axis=-1), wo_ref[...])
    x2_ref[0] = x2

    h3 = _rms(x2, nf_ref[...])
    _store_token_tiles(h3_ref, h3, TOK_TILE)
    h3h = h3.astype(BF16)
    h3l = (h3 - h3h.astype(F32)).astype(BF16)
    logits = (_dot_nt(wrh_ref[...], h3h) + _dot_nt(wrh_ref[...], h3l) + _dot_nt(wrl_ref[...], h3h)
              + br_ref[:, 0:1])
    eidx = lax.broadcasted_iota(I32, logits.shape, 0)
    work = logits
    vals, idxs = [], []
    for _ in range(TOP_K):
        m = jnp.max(work, axis=0, keepdims=True)
        ik = jnp.min(jnp.where(work == m, eidx, N_EXPERTS), axis=0, keepdims=True)
        vals.append(m)
        idxs.append(ik)
        work = jnp.where(eidx == ik, -jnp.inf, work)
    ex = [jnp.exp(v - vals[0]) for v in vals]
    inv_den = 1.0 / (ex[0] + ex[1] + ex[2] + ex[3])
    w_ref[0] = jnp.concatenate([e * inv_den for e in ex], axis=0)

    sel = jnp.zeros(logits.shape, F32)
    for ik in idxs:
        sel = sel + jnp.where(eidx == ik, 1.0, 0.0)
    cnt = jnp.broadcast_to(jnp.sum(sel, axis=1, keepdims=True), (N_EXPERTS, LANES))
    cnt_ref[0] = cnt
    seg_rows = jnp.floor((cnt + (OCT_ROWS - 1)) * (1.0 / OCT_ROWS)) * OCT_ROWS
    seg_start = _dot(tri32_ref[...], seg_rows.astype(BF16))
    pos = seg_start[:, 0:1] + _dot(sel.astype(BF16), tri_ref[...])
    q_ref[0] = jnp.concatenate(
        [jnp.sum(jnp.where(eidx == ik, pos, 0.0), axis=0, keepdims=True) for ik in idxs],
        axis=0).astype(I32) * TILE_SUBLANES


def _post(x3, bg, cv, yf, conv_w, gc, gf, wo_top, wo_bot, nx, wq, kmem, vmem_, wo, nf, wrh, wrl, br, tri, tri32):
    bsz, seq, _ = x3.shape
    nt = seq // TOK_TILE
    t = bsz * seq
    n_tiles = bsz * nt
    halo_per_tile = TOK_TILE // BF16_SUBLANES
    n_halo = seq // BF16_SUBLANES
    full = lambda shape: pl.BlockSpec(shape, lambda b, i: (0,) * len(shape))
    tile_d = pl.BlockSpec((1, TOK_TILE, D_MODEL), lambda b, i: (b, i, 0))
    tile_h = pl.BlockSpec((1, TOK_TILE, CONV_WIDTH), lambda b, i: (b, i, 0))
    halo_prev = pl.BlockSpec((1, BF16_SUBLANES, CONV_WIDTH),
                             lambda b, i: (b, jnp.maximum(i * halo_per_tile - 1, 0), 0))
    halo_next = pl.BlockSpec((1, BF16_SUBLANES, CONV_WIDTH),
                             lambda b, i: (b, jnp.minimum((i + 1) * halo_per_tile, n_halo - 1), 0))
    mem_blk = pl.BlockSpec((1, MEM_LEN, D_MODEL), lambda b, i: (b, 0, 0))
    tok4 = pl.BlockSpec((1, TOP_K, TOK_TILE), lambda b, i: (b * nt + i, 0, 0))
    per_tile = pl.BlockSpec((1, N_EXPERTS, LANES), lambda b, i: (b * nt + i, 0, 0))
    tok_tiles = pl.BlockSpec((TOK_TILE * TILE_SUBLANES, LANES), lambda b, i: (b * nt + i, 0))
    return pl.pallas_call(
        _post_kernel,
        grid=(bsz, nt),
        in_specs=[tile_d, tile_h, tile_h, halo_prev, halo_next, tile_h,
                  full((3, CONV_WIDTH)), full((1, CONV_WIDTH)), full((1, FFT_WIDTH)),
                  full((CONV_WIDTH, D_MODEL)), full((FFT_WIDTH, D_MODEL)), full((1, D_MODEL)),
                  full((D_MODEL, D_MODEL)), mem_blk, mem_blk, full((D_MODEL, D_MODEL)), full((1, D_MODEL)),
                  full((N_EXPERTS, D_MODEL)), full((N_EXPERTS, D_MODEL)), full((N_EXPERTS, 128)),
                  full((TOK_TILE, TOK_TILE)), full((N_EXPERTS, N_EXPERTS))],
        out_specs=[tile_d, tok_tiles, tok4, tok4, per_tile],
        out_shape=[jax.ShapeDtypeStruct((bsz, seq, D_MODEL), F32),
                   jax.ShapeDtypeStruct((t * TILE_SUBLANES, LANES), F32),
                   jax.ShapeDtypeStruct((n_tiles, TOP_K, TOK_TILE), F32),
                   jax.ShapeDtypeStruct((n_tiles, TOP_K, TOK_TILE), I32),
                   jax.ShapeDtypeStruct((n_tiles, N_EXPERTS, LANES), F32)],
        compiler_params=pltpu.CompilerParams(dimension_semantics=("parallel", "parallel"),
                                             vmem_limit_bytes=VMEM_LIMIT),
        name="post",
    )(x3, bg, cv, cv, cv, yf, conv_w, gc, gf, wo_top, wo_bot, nx, wq, kmem, vmem_, wo, nf, wrh, wrl, br, tri, tri32)


def _plan_kernel(cnt_ref, src_ref, dst_ref, be_ref, nu_ref, wnext_ref, wpar_ref, off_ref, nxt_ref, *, n_tiles):
    n_sched = be_ref.shape[0]
    dump0 = n_tiles * TILE_OCT

    def fill_pad(lo, hi):
        def body(p, c):
            o = p & (OCT_PER_BLOCK - 1)
            parity = lax.shift_right_logical(p, OCT_PER_BLOCK.bit_length() - 1) & 1
            src_ref[p] = dump0 + 2 * OCT_PER_BLOCK + o
            dst_ref[p] = dump0 + parity * OCT_PER_BLOCK + o
            return c
        lax.fori_loop(lo, hi, body, 0)

    def clear(tile, c):
        off_ref[tile] = 0
        return c

    lax.fori_loop(0, n_tiles, clear, 0)

    def find_next(i, nxt):
        e = N_EXPERTS - 1 - i
        nxt_ref[e] = nxt
        total = lax.fori_loop(0, n_tiles, lambda tile, s: s + cnt_ref[tile * N_EXPERTS + e], jnp.int32(0))
        return jnp.where(total > 0, e, nxt)

    lax.fori_loop(0, N_EXPERTS, find_next, jnp.int32(-1))
    fill_pad(0, OCT_PER_BLOCK)

    def per_expert(e, carry):
        pos0, blk0, group = carry

        def per_tile(tile, pos):
            n_oct = lax.shift_right_logical(cnt_ref[tile * N_EXPERTS + e] + (OCT_ROWS - 1), OCT_ROWS.bit_length() - 1)
            base = tile * TILE_OCT + off_ref[tile]
            off_ref[tile] = off_ref[tile] + n_oct

            def per_group(grp, c):
                for u in range(PLAN_UNROLL):
                    o = grp * PLAN_UNROLL + u
                    src_ref[pos + o] = base + o
                    dst_ref[pos + o] = base + o
                return c

            lax.fori_loop(0, lax.shift_right_logical(n_oct + (PLAN_UNROLL - 1), PLAN_UNROLL.bit_length() - 1),
                          per_group, 0)
            return pos + n_oct

        pos1 = lax.fori_loop(0, n_tiles, per_tile, pos0)
        nb = lax.shift_right_logical(pos1 - pos0 + (OCT_PER_BLOCK - 1), OCT_PER_BLOCK.bit_length() - 1)
        pos2 = pos0 + nb * OCT_PER_BLOCK
        fill_pad(pos1, pos2)

        def fill(j, c):
            be_ref[blk0 + j] = e
            wnext_ref[blk0 + j] = nxt_ref[e]
            wpar_ref[blk0 + j] = group & 1
            return c

        lax.fori_loop(0, nb, fill, 0)
        return pos2, blk0 + nb, group + jnp.where(nb > 0, 1, 0)

    pos, n_used, _ = lax.fori_loop(0, N_EXPERTS, per_expert,
                                   (jnp.int32(OCT_PER_BLOCK), jnp.int32(0), jnp.int32(0)))
    nu_ref[0] = n_used
    tail_e = be_ref[n_used - 1]

    def tail(j, c):
        be_ref[j] = tail_e
        wnext_ref[j] = -1
        wpar_ref[j] = 0
        return c

    lax.fori_loop(n_used, n_sched, tail, 0)
    fill_pad(pos, src_ref.shape[0])


def _plan(counts, n_tiles, n_blocks):
    smem = pl.BlockSpec(memory_space=pltpu.SMEM)
    grid_spec = pltpu.PrefetchScalarGridSpec(
        num_scalar_prefetch=1,
        grid=(1,),
        in_specs=[],
        out_specs=[smem] * 6,
        scratch_shapes=[pltpu.SMEM((n_tiles,), I32), pltpu.SMEM((N_EXPERTS,), I32)],
    )
    octs = jax.ShapeDtypeStruct(((n_blocks + 2) * OCT_PER_BLOCK,), I32)
    sched = jax.ShapeDtypeStruct((n_blocks + 1,), I32)
    return pl.pallas_call(
        functools.partial(_plan_kernel, n_tiles=n_tiles),
        grid_spec=grid_spec,
        out_shape=[octs, octs, sched, jax.ShapeDtypeStruct((1,), I32), sched, sched],
        compiler_params=pltpu.CompilerParams(dimension_semantics=("arbitrary",), vmem_limit_bytes=VMEM_LIMIT),
        name="plan",
    )(counts)


def _localsort_kernel(q_ref, h3_ref, xs_ref):
    xs_ref[...] = jnp.zeros_like(xs_ref)

    @pl.when(pl.program_id(0) < pl.num_programs(0) - 1)
    def _():
        def group(g, c):
            for u in range(TILE_SUBLANES):
                t = g * TILE_SUBLANES + u
                row = _tile_rows(h3_ref, t)[...]
                for k in range(TOP_K):
                    _tile_at(xs_ref, q_ref[0, t * TOP_K + k])[...] = row
            return c

        lax.fori_loop(0, TOK_TILE // TILE_SUBLANES, group, 0)


def _localsort(q_tiles, h3t):
    n_tiles = q_tiles.shape[0]
    last = n_tiles - 1
    region = TILE_OCT * OCT_ROWS * TILE_SUBLANES
    return pl.pallas_call(
        _localsort_kernel,
        grid=(n_tiles + 1,),
        in_specs=[pl.BlockSpec((None, 1, TOP_K * TOK_TILE), lambda i: (jnp.minimum(i, last), 0, 0),
                               memory_space=pltpu.SMEM),
                  pl.BlockSpec((TOK_TILE * TILE_SUBLANES, LANES), lambda i: (jnp.minimum(i, last), 0))],
        out_specs=pl.BlockSpec((region, LANES), lambda i: (i, 0)),
        out_shape=jax.ShapeDtypeStruct(((n_tiles + 1) * region, LANES), F32),
        compiler_params=pltpu.CompilerParams(dimension_semantics=("parallel",), vmem_limit_bytes=VMEM_LIMIT),
        name="localsort",
    )(q_tiles, h3t)


def _tile_at(ref, sublane_row):
    return ref.at[pl.ds(pl.multiple_of(sublane_row, TILE_SUBLANES), TILE_SUBLANES), :]


def _tile_rows(ref, row, n_rows=1):
    return ref.at[pl.ds(pl.multiple_of(row * TILE_SUBLANES, TILE_SUBLANES), n_rows * TILE_SUBLANES), :]


def _expert_kernel(be_ref, nu_ref, wnext_ref, wpar_ref,
                   src_cur_ref, src_next_ref, dst_prev_ref, bgu_ref, bd_ref, xs_ref, wgu_hbm, wd_hbm,
                   y_ref,
                   xbuf0, xbuf1, obuf0, obuf1, wgu_f32, wd_f32, wgu_bf, wd_bf, gsem, ssem, wsem):
    j = pl.program_id(0)
    nu = nu_ref[0]
    xbuf = (xbuf0, xbuf1)
    obuf = (obuf0, obuf1)
    octet = lambda ref, o: _tile_rows(ref, o * OCT_ROWS, OCT_ROWS)

    def gather_octet(src_ref, o, dst):
        return pltpu.make_async_copy(octet(xs_ref, src_ref[0, o]), octet(dst, o), gsem)

    def scatter_octet(dst_ref, o, src):
        return pltpu.make_async_copy(octet(src, o), octet(y_ref, dst_ref[0, o]), ssem)

    def weight_copies(e, p):
        return (pltpu.make_async_copy(wgu_hbm.at[e], wgu_f32.at[p], wsem.at[p, 0]),
                pltpu.make_async_copy(wd_hbm.at[e], wd_f32.at[p], wsem.at[p, 1]))

    def wait_gather(dst):
        for _ in range(OCT_PER_BLOCK):
            pltpu.make_async_copy(octet(xs_ref, 0), octet(dst, 0), gsem).wait()

    def wait_scatter(src):
        for _ in range(OCT_PER_BLOCK):
            pltpu.make_async_copy(octet(src, 0), octet(y_ref, 0), ssem).wait()

    @pl.when(j == 0)
    def _():
        obuf1[...] = jnp.zeros_like(obuf1)
        for c in weight_copies(be_ref[0], 0):
            c.start(priority=WEIGHT_DMA_PRIORITY)
        for o in range(OCT_PER_BLOCK):
            gather_octet(src_cur_ref, o, xbuf0).start(priority=ROW_DMA_PRIORITY)

    first_of_group = jnp.logical_or(j == 0, be_ref[j] != be_ref[jnp.maximum(j - 1, 0)])

    @pl.when(jnp.logical_and(j < nu, first_of_group))
    def _():
        p = wpar_ref[j]
        for c in weight_copies(be_ref[j], p):
            c.wait()
        wgu_bf[...] = wgu_f32[p].astype(BF16)
        wd_bf[...] = wd_f32[p].astype(BF16)

        @pl.when(wnext_ref[j] >= 0)
        def _():
            for c in weight_copies(wnext_ref[j], 1 - p):
                c.start(priority=WEIGHT_DMA_PRIORITY)

    def step(s, compute):
        wait_gather(xbuf[s])

        @pl.when(j >= 1)
        def _():
            wait_scatter(obuf[s])

        def start_scatter():
            for o in range(OCT_PER_BLOCK):
                scatter_octet(dst_prev_ref, o, obuf[1 - s]).start(priority=ROW_DMA_PRIORITY)

        if compute:
            for o in range(OCT_PER_BLOCK):
                gather_octet(src_next_ref, o, xbuf[1 - s]).start(priority=ROW_DMA_PRIORITY)
            x = _load_token_tiles(xbuf[s], ROW_BLOCK).astype(BF16)
            gu = _dot(x, wgu_bf[...]) + bgu_ref[0]
            gate = jnp.minimum(gu[:, :D_EXPERT], SWIGLU_LIMIT)
            up = jnp.clip(gu[:, D_EXPERT:], -SWIGLU_LIMIT, SWIGLU_LIMIT)
            glu = gate * (1.0 / (1.0 + jnp.exp(-SWIGLU_ALPHA * gate)))
            h = ((up + 1.0) * glu).astype(BF16)
            start_scatter()
            _store_token_tiles(obuf[s], _dot(h, wd_bf[...]) + bd_ref[0], ROW_BLOCK)
        else:
            start_scatter()
            wait_scatter(obuf[1 - s])

    for s in range(2):
        parity = (j & 1) == s
        pl.when(jnp.logical_and(j < nu, parity))(functools.partial(step, s, True))
        pl.when(jnp.logical_and(j == nu, parity))(functools.partial(step, s, False))


def _experts(block_e, n_used, w_next, w_par, src3, dst3, xs, wgu, bgu, wd, bd):
    n_steps = block_e.shape[0]
    blk_rows = ROW_BLOCK * TILE_SUBLANES
    last_blk = src3.shape[0] - 1
    exp_map = lambda j, be, nu, wn, wp: (be[j], 0, 0)
    oct_spec = lambda off: pl.BlockSpec((None, 1, OCT_PER_BLOCK),
                                        lambda j, be, nu, wn, wp: (jnp.minimum(j + off, last_blk), 0, 0),
                                        memory_space=pltpu.SMEM)
    hbm = pl.BlockSpec(memory_space=pl.ANY)
    grid_spec = pltpu.PrefetchScalarGridSpec(
        num_scalar_prefetch=4,
        grid=(n_steps,),
        in_specs=[oct_spec(1), oct_spec(2), oct_spec(0),
                  pl.BlockSpec((1, 1, 2 * D_EXPERT), exp_map), pl.BlockSpec((1, 1, D_MODEL), exp_map),
                  hbm, hbm, hbm],
        out_specs=hbm,
        scratch_shapes=[pltpu.VMEM((blk_rows, LANES), F32)] * 4
        + [pltpu.VMEM((2, D_MODEL, 2 * D_EXPERT), F32), pltpu.VMEM((2, D_EXPERT, D_MODEL), F32),
           pltpu.VMEM((D_MODEL, 2 * D_EXPERT), BF16), pltpu.VMEM((D_EXPERT, D_MODEL), BF16),
           pltpu.SemaphoreType.DMA, pltpu.SemaphoreType.DMA, pltpu.SemaphoreType.DMA((2, 2))],
    )
    return pl.pallas_call(
        _expert_kernel,
        grid_spec=grid_spec,
        out_shape=jax.ShapeDtypeStruct(xs.shape, F32),
        input_output_aliases={9: 0},
        compiler_params=pltpu.CompilerParams(dimension_semantics=("arbitrary",), vmem_limit_bytes=VMEM_LIMIT),
        name="experts",
    )(block_e, n_used, w_next, w_par, src3, src3, dst3, bgu, bd, xs, wgu, wd)


def _combine_kernel(q_ref, w_ref, y_ref, x2_ref, g_ref, out_ref, acc_ref):
    def group(grp, c):
        for u in range(TILE_SUBLANES):
            t = grp * TILE_SUBLANES + u
            acc = w_ref[0, t * TOP_K] * _tile_at(y_ref, q_ref[0, t * TOP_K])[...]
            for k in range(1, TOP_K):
                acc = acc + w_ref[0, t * TOP_K + k] * _tile_at(y_ref, q_ref[0, t * TOP_K + k])[...]
            _tile_rows(acc_ref, t)[...] = acc
        return c

    lax.fori_loop(0, TOK_TILE // TILE_SUBLANES, group, 0)
    out_ref[...] = _rms(x2_ref[...] + _load_token_tiles(acc_ref, TOK_TILE), g_ref[...])


def _combine(q_tiles, w_tiles, ybuf, x2, g):
    n_tiles = q_tiles.shape[0]
    region = TILE_OCT * OCT_ROWS * TILE_SUBLANES
    tok4 = pl.BlockSpec((None, 1, TOP_K * TOK_TILE), lambda i: (i, 0, 0), memory_space=pltpu.SMEM)
    return pl.pallas_call(
        _combine_kernel,
        grid=(n_tiles,),
        in_specs=[tok4, tok4,
                  pl.BlockSpec((region, LANES), lambda i: (i, 0)),
                  pl.BlockSpec((TOK_TILE, D_MODEL), lambda i: (i, 0)),
                  pl.BlockSpec((1, D_MODEL), lambda i: (0, 0))],
        out_specs=pl.BlockSpec((TOK_TILE, D_MODEL), lambda i: (i, 0)),
        out_shape=jax.ShapeDtypeStruct(x2.shape, F32),
        scratch_shapes=[pltpu.VMEM((TOK_TILE * TILE_SUBLANES, LANES), F32)],
        compiler_params=pltpu.CompilerParams(dimension_semantics=("parallel",), vmem_limit_bytes=VMEM_LIMIT),
        name="combine",
    )(q_tiles, w_tiles, ybuf, x2, g)


def _layer(x, mem, norm_mix, w_in, conv_w, g_conv_out, g_fft_out, w_out, norm_xattn, norm_mem,
           w_q, w_k, w_v, w_o, norm_ffn, w_router, b_router, w_gate_up, b_gate_up, w_down, b_down, tables):
    bsz, seq, d = x.shape
    t = bsz * seq
    cs, m1, m2 = tables
    row = lambda v: v.reshape(1, -1)

    kmem, vmem_ = _memkv(mem, row(norm_mem), w_k.astype(BF16), w_v.astype(BF16))

    bg, cv, a, s = _inproj(x.reshape(t, d), row(norm_mix), w_in.astype(BF16), cs)
    cols = FFT_N2 * FFT_WIDTH
    g = _fft1(a.reshape(bsz, FFT_N1, cols), s.reshape(bsz, FFT_N1, cols), m1)
    yf = _fft2(g.reshape(bsz, 2, FFT_N1, FFT_N2, FFT_WIDTH), m2).reshape(bsz, seq, FFT_WIDTH)

    w_out_b = w_out.astype(BF16)
    wr_t = w_router.T
    wr_hi = wr_t.astype(BF16)
    wr_lo = (wr_t - wr_hi.astype(F32)).astype(BF16)
    tri = (jnp.arange(TOK_TILE)[:, None] < jnp.arange(TOK_TILE)[None, :]).astype(BF16)
    tri32 = (jnp.arange(N_EXPERTS)[None, :] < jnp.arange(N_EXPERTS)[:, None]).astype(BF16)
    x2, h3, w_tiles, q_tiles, cnt = _post(
        x, bg.reshape(bsz, seq, CONV_WIDTH), cv.reshape(bsz, seq, CONV_WIDTH), yf,
        conv_w, row(g_conv_out), row(g_fft_out), w_out_b[:CONV_WIDTH], w_out_b[CONV_WIDTH:],
        row(norm_xattn), w_q.astype(BF16), kmem, vmem_, w_o.astype(BF16), row(norm_ffn),
        wr_hi, wr_lo, jnp.broadcast_to(b_router[:, None], (N_EXPERTS, 128)), tri, tri32)

    n_tiles = t // TOK_TILE
    n_blocks = (t * TOP_K + n_tiles * N_EXPERTS * (OCT_ROWS - 1)) // ROW_BLOCK + N_EXPERTS
    counts = cnt[:, :, 0].astype(I32).reshape(-1)
    src, dst, block_e, n_used, w_next, w_par = _plan(counts, n_tiles, n_blocks)
    per_block = lambda v: v.reshape(n_blocks + 2, 1, OCT_PER_BLOCK)
    by_token = lambda v: v.transpose(0, 2, 1).reshape(n_tiles, 1, TOK_TILE * TOP_K)
    q_tiles, w_tiles = by_token(q_tiles), by_token(w_tiles)
    xs = _localsort(q_tiles, h3)
    ybuf = _experts(block_e, n_used, w_next, w_par, per_block(src), per_block(dst), xs,
                    w_gate_up, b_gate_up[:, None, :], w_down, b_down[:, None, :])
    return x2.reshape(t, d), ybuf, q_tiles, w_tiles


def kernel(x, mem, norm_mix, w_in, conv_w, g_conv_out, g_fft_out, w_out, norm_xattn, norm_mem, w_q, w_k, w_v, w_o,
           norm_ffn, w_router, b_router, w_gate_up, b_gate_up, w_down, b_down, norm_final):
    bsz, seq, d = x.shape
    depth = norm_mix.shape[0]
    assert depth == 1, "final norm is fused into the combine step of the single layer"
    tables = _dft_tables(seq)
    x2, ybuf, q_tiles, w_tiles = _layer(
        x, mem, norm_mix[0], w_in[0], conv_w[0], g_conv_out[0], g_fft_out[0], w_out[0], norm_xattn[0],
        norm_mem[0], w_q[0], w_k[0], w_v[0], w_o[0], norm_ffn[0], w_router[0], b_router[0],
        w_gate_up[0], b_gate_up[0], w_down[0], b_down[0], tables)
    out = _combine(q_tiles, w_tiles, ybuf, x2, norm_final.reshape(1, -1))
    return out.reshape(bsz, seq, d)
```

```python
import functools
import math

import numpy as np
import jax
import jax.numpy as jnp
from jax import lax
from jax.experimental import pallas as pl
from jax.experimental.pallas import tpu as pltpu

F32 = jnp.float32
BF16 = jnp.bfloat16
I32 = jnp.int32

D_MODEL = 1024
CONV_WIDTH = 512
FFT_WIDTH = 512
GROUP_DIM = 64
IN_PROJ_WIDTH = 3 * CONV_WIDTH + FFT_WIDTH
MEM_LEN = 256
XATTN_HEADS = 4
XATTN_HEAD_DIM = D_MODEL // XATTN_HEADS
N_EXPERTS = 32
TOP_K = 4
D_EXPERT = D_MODEL
SWIGLU_LIMIT = 7.0
SWIGLU_ALPHA = 1.702
EPS = 1e-5

FFT_N1 = 64
FFT_N2 = 128

TOK_TILE = 512
ROW_BLOCK = 512
ROW_DMA_PRIORITY = 0
WEIGHT_DMA_PRIORITY = 1
OCT_ROWS = 8
OCT_PER_BLOCK = ROW_BLOCK // OCT_ROWS
PLAN_UNROLL = 8
TILE_OCT = (TOK_TILE * TOP_K) // OCT_ROWS + N_EXPERTS
FFT1_LANES = 8192
FFT2_K1 = 8
BF16_SUBLANES = 16
TILE_SUBLANES = 8
LANES = 128
assert D_MODEL == TILE_SUBLANES * LANES
VMEM_LIMIT = 56 * 1024 * 1024


def _rms(x, g):
    return x * lax.rsqrt(jnp.mean(x * x, axis=-1, keepdims=True) + EPS) * g


def _dot(a, b):
    return jnp.dot(a, b, preferred_element_type=F32)


def _dot_nt(a, b):
    return lax.dot_general(a, b, (((1,), (1,)), ((), ())), preferred_element_type=F32)


def _load_token_tiles(ref, rows, base=0):
    return jnp.concatenate(
        [ref[pl.ds(base + s, rows, stride=TILE_SUBLANES), :] for s in range(TILE_SUBLANES)], axis=-1)


def _store_token_tiles(ref, val, rows):
    for s in range(TILE_SUBLANES):
        ref[pl.ds(s, rows, stride=TILE_SUBLANES), :] = val[:, s * LANES:(s + 1) * LANES]


def _dft_tables(seq):
    assert seq == FFT_N1 * FFT_N2
    c = np.arange(GROUP_DIM)
    ang = 2.0 * np.pi * ((c[:, None] * c[None, :]) % GROUP_DIM) / GROUP_DIM
    groups = FFT_WIDTH // GROUP_DIM
    eye = np.eye(groups)
    cs = np.concatenate([np.kron(eye, np.cos(ang)), np.kron(eye, np.sin(ang))], axis=1) / math.sqrt(GROUP_DIM)
    n1 = np.arange(FFT_N1)
    a1 = 2.0 * np.pi * ((n1[:, None] * n1[None, :]) % FFT_N1) / FFT_N1
    c1, s1 = np.cos(a1), np.sin(a1)
    m1 = np.block([[c1, -s1], [s1, c1]]) / math.sqrt(FFT_N1)
    k1 = np.arange(FFT_N1)[:, None, None]
    k2 = np.arange(FFT_N2)[None, :, None]
    n2 = np.arange(FFT_N2)[None, None, :]
    a2 = 2.0 * np.pi * ((n2 * (k1 + FFT_N1 * k2)) % seq) / seq
    m2 = np.concatenate([np.cos(a2), -np.sin(a2)], axis=2) / math.sqrt(FFT_N2)
    return (jnp.asarray(cs, F32).astype(BF16), jnp.asarray(m1, F32).astype(BF16),
            jnp.asarray(m2, F32).astype(BF16))


def _memkv_kernel(mem_ref, g_ref, wk_ref, wv_ref, k_ref, v_ref):
    m = _rms(mem_ref[0], g_ref[...]).astype(BF16)
    k_ref[0] = _dot(m, wk_ref[...]).astype(BF16)
    v_ref[0] = _dot(m, wv_ref[...]).astype(BF16)


def _memkv(mem, g, wk, wv):
    bsz = mem.shape[0]
    full = lambda shape: pl.BlockSpec(shape, lambda b: (0,) * len(shape))
    per_b = pl.BlockSpec((1, MEM_LEN, D_MODEL), lambda b: (b, 0, 0))
    return pl.pallas_call(
        _memkv_kernel,
        grid=(bsz,),
        in_specs=[per_b, full((1, D_MODEL)), full((D_MODEL, D_MODEL)), full((D_MODEL, D_MODEL))],
        out_specs=[per_b, per_b],
        out_shape=[jax.ShapeDtypeStruct((bsz, MEM_LEN, D_MODEL), BF16)] * 2,
        compiler_params=pltpu.CompilerParams(dimension_semantics=("arbitrary",), vmem_limit_bytes=VMEM_LIMIT),
        name="memkv",
    )(mem, g, wk, wv)


def _inproj_kernel(x_ref, g_ref, win_ref, cs_ref, b_ref, cv_ref, a_ref, s_ref):
    h = _rms(x_ref[...], g_ref[...]).astype(BF16)
    z = _dot(h, win_ref[...])
    b_ref[...] = z[:, :CONV_WIDTH].astype(BF16)
    cv_ref[...] = (z[:, CONV_WIDTH:2 * CONV_WIDTH] * z[:, 2 * CONV_WIDTH:3 * CONV_WIDTH]).astype(BF16)
    u = z[:, 3 * CONV_WIDTH:].astype(BF16)
    ab = _dot(u, cs_ref[...])
    a_ref[...] = ab[:, :FFT_WIDTH].astype(BF16)
    s_ref[...] = ab[:, FFT_WIDTH:].astype(BF16)


def _inproj(x2d, g, w_in, cs):
    t = x2d.shape[0]
    tile = pl.BlockSpec((TOK_TILE, D_MODEL), lambda i: (i, 0))
    half = pl.BlockSpec((TOK_TILE, CONV_WIDTH), lambda i: (i, 0))
    full = lambda shape: pl.BlockSpec(shape, lambda i: (0,) * len(shape))
    return pl.pallas_call(
        _inproj_kernel,
        grid=(t // TOK_TILE,),
        in_specs=[tile, full((1, D_MODEL)), full((D_MODEL, IN_PROJ_WIDTH)), full((FFT_WIDTH, 2 * FFT_WIDTH))],
        out_specs=[half] * 4,
        out_shape=[jax.ShapeDtypeStruct((t, CONV_WIDTH), BF16)] * 4,
        compiler_params=pltpu.CompilerParams(dimension_semantics=("parallel",), vmem_limit_bytes=VMEM_LIMIT),
        name="inproj",
    )(x2d, g, w_in, cs)


def _fft1_kernel(a_ref, s_ref, m1_ref, g_ref):
    x = jnp.concatenate([a_ref[0], s_ref[0]], axis=0)
    g_ref[0] = _dot(m1_ref[...], x).astype(BF16)


def _fft1(a3, s3, m1):
    bsz, _, cols = a3.shape
    blk = pl.BlockSpec((1, FFT_N1, FFT1_LANES), lambda b, j: (b, 0, j))
    return pl.pallas_call(
        _fft1_kernel,
        grid=(bsz, cols // FFT1_LANES),
        in_specs=[blk, blk, pl.BlockSpec((2 * FFT_N1, 2 * FFT_N1), lambda b, j: (0, 0))],
        out_specs=pl.BlockSpec((1, 2 * FFT_N1, FFT1_LANES), lambda b, j: (b, 0, j)),
        out_shape=jax.ShapeDtypeStruct((bsz, 2 * FFT_N1, cols), BF16),
        compiler_params=pltpu.CompilerParams(dimension_semantics=("parallel", "parallel"),
                                             vmem_limit_bytes=VMEM_LIMIT),
        name="fft1",
    )(a3, s3, m1)


def _fft2_kernel(g_ref, m2_ref, y_ref):
    for j in range(FFT2_K1):
        x = jnp.concatenate([g_ref[0, 0, j], g_ref[0, 1, j]], axis=0)
        y_ref[0, :, j * FFT_WIDTH:(j + 1) * FFT_WIDTH] = _dot(m2_ref[j], x)


def _fft2(g5, m2):
    bsz = g5.shape[0]
    return pl.pallas_call(
        _fft2_kernel,
        grid=(bsz, FFT_N1 // FFT2_K1),
        in_specs=[pl.BlockSpec((1, 2, FFT2_K1, FFT_N2, FFT_WIDTH), lambda b, j: (b, 0, j, 0, 0)),
                  pl.BlockSpec((FFT2_K1, FFT_N2, 2 * FFT_N2), lambda b, j: (j, 0, 0))],
        out_specs=pl.BlockSpec((1, FFT_N2, FFT2_K1 * FFT_WIDTH), lambda b, j: (b, 0, j)),
        out_shape=jax.ShapeDtypeStruct((bsz, FFT_N2, FFT_N1 * FFT_WIDTH), F32),
        compiler_params=pltpu.CompilerParams(dimension_semantics=("parallel", "parallel"),
                                             vmem_limit_bytes=VMEM_LIMIT),
        name="fft2",
    )(g5, m2)


def _post_kernel(x_ref, bg_ref, cv_ref, cvp_ref, cvn_ref, yf_ref, convw_ref, gc_ref, gf_ref,
                 wot_ref, wob_ref, nx_ref, wq_ref, k_ref, v_ref, wo_ref, nf_ref,
                 wrh_ref, wrl_ref, br_ref, tri_ref, tri32_ref,
                 x2_ref, h3_ref, w_ref, q_ref, cnt_ref, yfs_ref):
    i = pl.program_id(1)
    last = pl.num_programs(1) - 1

    cv = cv_ref[0].astype(F32)
    prev_row = jnp.where(i > 0, cvp_ref[0].astype(F32)[BF16_SUBLANES - 1:BF16_SUBLANES, :], 0.0)
    next_row = jnp.where(i < last, cvn_ref[0].astype(F32)[0:1, :], 0.0)
    rows = lax.broadcasted_iota(I32, cv.shape, 0)
    cvm1 = jnp.where(rows == 0, prev_row, pltpu.roll(cv, 1, axis=0))
    cvp1 = jnp.where(rows == TOK_TILE - 1, next_row, pltpu.roll(cv, TOK_TILE - 1, axis=0))
    cw = convw_ref[...]
    y_conv = bg_ref[0].astype(F32) * (cw[0:1] * cvm1 + cw[1:2] * cv + cw[2:3] * cvp1)
    yc_n = _rms(y_conv, gc_ref[...]).astype(BF16)
    lane_groups = FFT_WIDTH // LANES
    for k1 in range(FFT_N1):
        for lg in range(lane_groups):
            lane0 = k1 * FFT_WIDTH + lg * LANES
            yfs_ref[lg, pl.ds(k1, TOK_TILE // FFT_N1, stride=FFT_N1), :] = yf_ref[0, :, lane0:lane0 + LANES]
    yf = jnp.concatenate([yfs_ref[lg] for lg in range(lane_groups)], axis=-1)
    yf_n = _rms(yf, gf_ref[...]).astype(BF16)
    x1 = x_ref[0] + _dot(yc_n, wot_ref[...]) + _dot(yf_n, wob_ref[...])

    h2 = _rms(x1, nx_ref[...]).astype(BF16)
    q = _dot(h2, wq_ref[...]).astype(BF16)
    heads = []
    for hd in range(XATTN_HEADS):
        sl = slice(hd * XATTN_HEAD_DIM, (hd + 1) * XATTN_HEAD_DIM)
        s = _dot_nt(q[:, sl], k_ref[0, :, sl]) * (XATTN_HEAD_DIM ** -0.5)
        e = jnp.exp(s - jnp.max(s, axis=-1, keepdims=True))
        p = e * (1.0 / jnp.sum(e, axis=-1, keepdims=True))
        heads.append(_dot(p.astype(BF16), v_ref[0, :, sl]).astype(BF16))
    x2 = x1 + _dot(jnp.concatenate(heads, axis=-1), wo_ref[...])
    x2_ref[0] = x2

    h3 = _rms(x2, nf_ref[...])
    _store_token_tiles(h3_ref, h3, TOK_TILE)
    h3h = h3.astype(BF16)
    h3l = (h3 - h3h.astype(F32)).astype(BF16)
    logits = (_dot_nt(wrh_ref[...], h3h) + _dot_nt(wrh_ref[...], h3l) + _dot_nt(wrl_ref[...], h3h)
              + br_ref[:, 0:1])
    eidx = lax.broadcasted_iota(I32, logits.shape, 0)
    work = logits
    vals, idxs = [], []
    for _ in range(TOP_K):
        m = jnp.max(work, axis=0, keepdims=True)
        ik = jnp.min(jnp.where(work == m, eidx, N_EXPERTS), axis=0, keepdims=True)
        vals.append(m)
        idxs.append(ik)
        work = jnp.where(eidx == ik, -jnp.inf, work)
    ex = [jnp.exp(v - vals[0]) for v in vals]
    inv_den = 1.0 / (ex[0] + ex[1] + ex[2] + ex[3])
    w_ref[0] = jnp.concatenate([e * inv_den for e in ex], axis=0)

    sel = jnp.zeros(logits.shape, F32)
    for ik in idxs:
        sel = sel + jnp.where(eidx == ik, 1.0, 0.0)
    cnt = jnp.broadcast_to(jnp.sum(sel, axis=1, keepdims=True), (N_EXPERTS, LANES))
    cnt_ref[0] = cnt
    seg_rows = jnp.floor((cnt + (OCT_ROWS - 1)) * (1.0 / OCT_ROWS)) * OCT_ROWS
    seg_start = _dot(tri32_ref[...], seg_rows.astype(BF16))
    pos = seg_start[:, 0:1] + _dot(sel.astype(BF16), tri_ref[...])
    q_ref[0] = jnp.concatenate(
        [jnp.sum(jnp.where(eidx == ik, pos, 0.0), axis=0, keepdims=True) for ik in idxs],
        axis=0).astype(I32) * TILE_SUBLANES


def _post(x3, bg, cv, yf, conv_w, gc, gf, wo_top, wo_bot, nx, wq, kmem, vmem_, wo, nf, wrh, wrl, br, tri, tri32):
    bsz, seq, _ = x3.shape
    nt = seq // TOK_TILE
    t = bsz * seq
    n_tiles = bsz * nt
    halo_per_tile = TOK_TILE // BF16_SUBLANES
    n_halo = seq // BF16_SUBLANES
    full = lambda shape: pl.BlockSpec(shape, lambda b, i: (0,) * len(shape))
    tile_d = pl.BlockSpec((1, TOK_TILE, D_MODEL), lambda b, i: (b, i, 0))
    tile_h = pl.BlockSpec((1, TOK_TILE, CONV_WIDTH), lambda b, i: (b, i, 0))
    halo_prev = pl.BlockSpec((1, BF16_SUBLANES, CONV_WIDTH),
                             lambda b, i: (b, jnp.maximum(i * halo_per_tile - 1, 0), 0))
    halo_next = pl.BlockSpec((1, BF16_SUBLANES, CONV_WIDTH),
                             lambda b, i: (b, jnp.minimum((i + 1) * halo_per_tile, n_halo - 1), 0))
    mem_blk = pl.BlockSpec((1, MEM_LEN, D_MODEL), lambda b, i: (b, 0, 0))
    tok4 = pl.BlockSpec((1, TOP_K, TOK_TILE), lambda b, i: (b * nt + i, 0, 0))
    per_tile = pl.BlockSpec((1, N_EXPERTS, LANES), lambda b, i: (b * nt + i, 0, 0))
    tok_tiles = pl.BlockSpec((TOK_TILE * TILE_SUBLANES, LANES), lambda b, i: (b * nt + i, 0))
    return pl.pallas_call(
        _post_kernel,
        grid=(bsz, nt),
        in_specs=[tile_d, tile_h, tile_h, halo_prev, halo_next,
                  pl.BlockSpec((1, TOK_TILE // FFT_N1, FFT_N1 * FFT_WIDTH), lambda b, i: (b, i, 0)),
                  full((3, CONV_WIDTH)), full((1, CONV_WIDTH)), full((1, FFT_WIDTH)),
                  full((CONV_WIDTH, D_MODEL)), full((FFT_WIDTH, D_MODEL)), full((1, D_MODEL)),
                  full((D_MODEL, D_MODEL)), mem_blk, mem_blk, full((D_MODEL, D_MODEL)), full((1, D_MODEL)),
                  full((N_EXPERTS, D_MODEL)), full((N_EXPERTS, D_MODEL)), full((N_EXPERTS, 128)),
                  full((TOK_TILE, TOK_TILE)), full((N_EXPERTS, N_EXPERTS))],
        out_specs=[tile_d, tok_tiles, tok4, tok4, per_tile],
        out_shape=[jax.ShapeDtypeStruct((bsz, seq, D_MODEL), F32),
                   jax.ShapeDtypeStruct((t * TILE_SUBLANES, LANES), F32),
                   jax.ShapeDtypeStruct((n_tiles, TOP_K, TOK_TILE), F32),
                   jax.ShapeDtypeStruct((n_tiles, TOP_K, TOK_TILE), I32),
                   jax.ShapeDtypeStruct((n_tiles, N_EXPERTS, LANES), F32)],
        scratch_shapes=[pltpu.VMEM((FFT_WIDTH // LANES, TOK_TILE, LANES), F32)],
        compiler_params=pltpu.CompilerParams(dimension_semantics=("parallel", "parallel"),
                                             vmem_limit_bytes=VMEM_LIMIT),
        name="post",
    )(x3, bg, cv, cv, cv, yf, conv_w, gc, gf, wo_top, wo_bot, nx, wq, kmem, vmem_, wo, nf, wrh, wrl, br, tri, tri32)


def _plan_kernel(cnt_ref, src_ref, dst_ref, be_ref, nu_ref, wnext_ref, wpar_ref, off_ref, nxt_ref, *, n_tiles):
    n_sched = be_ref.shape[0]
    dump0 = n_tiles * TILE_OCT

    def fill_pad(lo, hi):
        def body(p, c):
            o = p & (OCT_PER_BLOCK - 1)
            parity = lax.shift_right_logical(p, OCT_PER_BLOCK.bit_length() - 1) & 1
            src_ref[p] = dump0 + 2 * OCT_PER_BLOCK + o
            dst_ref[p] = dump0 + parity * OCT_PER_BLOCK + o
            return c
        lax.fori_loop(lo, hi, body, 0)

    def clear(tile, c):
        off_ref[tile] = 0
        return c

    lax.fori_loop(0, n_tiles, clear, 0)

    def find_next(i, nxt):
        e = N_EXPERTS - 1 - i
        nxt_ref[e] = nxt
        total = lax.fori_loop(0, n_tiles, lambda tile, s: s + cnt_ref[tile * N_EXPERTS + e], jnp.int32(0))
        return jnp.where(total > 0, e, nxt)

    lax.fori_loop(0, N_EXPERTS, find_next, jnp.int32(-1))
    fill_pad(0, OCT_PER_BLOCK)

    def per_expert(e, carry):
        pos0, blk0, group = carry

        def per_tile(tile, pos):
            n_oct = lax.shift_right_logical(cnt_ref[tile * N_EXPERTS + e] + (OCT_ROWS - 1), OCT_ROWS.bit_length() - 1)
            base = tile * TILE_OCT + off_ref[tile]
            off_ref[tile] = off_ref[tile] + n_oct

            def per_group(grp, c):
                for u in range(PLAN_UNROLL):
                    o = grp * PLAN_UNROLL + u
                    src_ref[pos + o] = base + o
                    dst_ref[pos + o] = base + o
                return c

            per_group(0, 0)
            per_group(1, 0)
            lax.fori_loop(2, lax.shift_right_logical(n_oct + (PLAN_UNROLL - 1), PLAN_UNROLL.bit_length() - 1),
                          per_group, 0)
            return pos + n_oct

        pos1 = lax.fori_loop(0, n_tiles, per_tile, pos0)
        nb = lax.shift_right_logical(pos1 - pos0 + (OCT_PER_BLOCK - 1), OCT_PER_BLOCK.bit_length() - 1)
        pos2 = pos0 + nb * OCT_PER_BLOCK
        fill_pad(pos1, pos2)

        def fill(j, c):
            be_ref[blk0 + j] = e
            wnext_ref[blk0 + j] = nxt_ref[e]
            wpar_ref[blk0 + j] = group & 1
            return c

        lax.fori_loop(0, nb, fill, 0)
        return pos2, blk0 + nb, group + jnp.where(nb > 0, 1, 0)

    pos, n_used, _ = lax.fori_loop(0, N_EXPERTS, per_expert,
                                   (jnp.int32(OCT_PER_BLOCK), jnp.int32(0), jnp.int32(0)))
    nu_ref[0] = n_used
    tail_e = be_ref[n_used - 1]

    def tail(j, c):
        be_ref[j] = tail_e
        wnext_ref[j] = -1
        wpar_ref[j] = 0
        return c

    lax.fori_loop(n_used, n_sched, tail, 0)
    fill_pad(pos, src_ref.shape[0])


def _plan(counts, n_tiles, n_blocks):
    smem = pl.BlockSpec(memory_space=pltpu.SMEM)
    grid_spec = pltpu.PrefetchScalarGridSpec(
        num_scalar_prefetch=1,
        grid=(1,),
        in_specs=[],
        out_specs=[smem] * 6,
        scratch_shapes=[pltpu.SMEM((n_tiles,), I32), pltpu.SMEM((N_EXPERTS,), I32)],
    )
    octs = jax.ShapeDtypeStruct(((n_blocks + 2) * OCT_PER_BLOCK,), I32)
    sched = jax.ShapeDtypeStruct((n_blocks + 1,), I32)
    return pl.pallas_call(
        functools.partial(_plan_kernel, n_tiles=n_tiles),
        grid_spec=grid_spec,
        out_shape=[octs, octs, sched, jax.ShapeDtypeStruct((1,), I32), sched, sched],
        compiler_params=pltpu.CompilerParams(dimension_semantics=("arbitrary",), vmem_limit_bytes=VMEM_LIMIT),
        name="plan",
    )(counts)


def _localsort_kernel(q_ref, h3_ref, xs_ref):
    xs_ref[...] = jnp.zeros_like(xs_ref)

    @pl.when(pl.program_id(0) < pl.num_programs(0) - 1)
    def _():
        def group(g, c):
            for u in range(TILE_SUBLANES):
                t = g * TILE_SUBLANES + u
                row = _tile_rows(h3_ref, t)[...]
                for k in range(TOP_K):
                    _tile_at(xs_ref, q_ref[0, t * TOP_K + k])[...] = row
            return c

        lax.fori_loop(0, TOK_TILE // TILE_SUBLANES, group, 0)


def _localsort(q_tiles, h3t):
    n_tiles = q_tiles.shape[0]
    last = n_tiles - 1
    region = TILE_OCT * OCT_ROWS * TILE_SUBLANES
    return pl.pallas_call(
        _localsort_kernel,
        grid=(n_tiles + 1,),
        in_specs=[pl.BlockSpec((None, 1, TOP_K * TOK_TILE), lambda i: (jnp.minimum(i, last), 0, 0),
                               memory_space=pltpu.SMEM),
                  pl.BlockSpec((TOK_TILE * TILE_SUBLANES, LANES), lambda i: (jnp.minimum(i, last), 0))],
        out_specs=pl.BlockSpec((region, LANES), lambda i: (i, 0)),
        out_shape=jax.ShapeDtypeStruct(((n_tiles + 1) * region, LANES), F32),
        compiler_params=pltpu.CompilerParams(dimension_semantics=("parallel",), vmem_limit_bytes=VMEM_LIMIT),
        name="localsort",
    )(q_tiles, h3t)


def _tile_at(ref, sublane_row):
    return ref.at[pl.ds(pl.multiple_of(sublane_row, TILE_SUBLANES), TILE_SUBLANES), :]


def _tile_rows(ref, row, n_rows=1):
    return ref.at[pl.ds(pl.multiple_of(row * TILE_SUBLANES, TILE_SUBLANES), n_rows * TILE_SUBLANES), :]


def _expert_kernel(be_ref, nu_ref, wnext_ref, wpar_ref,
                   src_cur_ref, src_next_ref, dst_prev_ref, bgu_ref, bd_ref, xs_ref, wgu_hbm, wd_hbm,
                   y_ref,
                   xbuf0, xbuf1, obuf0, obuf1, wgu_f32, wd_f32, wgu_bf, wd_bf, gsem, ssem, wsem):
    j = pl.program_id(0)
    nu = nu_ref[0]
    xbuf = (xbuf0, xbuf1)
    obuf = (obuf0, obuf1)
    octet = lambda ref, o: _tile_rows(ref, o * OCT_ROWS, OCT_ROWS)

    def gather_octet(src_ref, o, dst):
        return pltpu.make_async_copy(octet(xs_ref, src_ref[0, o]), octet(dst, o), gsem)

    def scatter_octet(dst_ref, o, src):
        return pltpu.make_async_copy(octet(src, o), octet(y_ref, dst_ref[0, o]), ssem)

    def weight_copies(e, p):
        return (pltpu.make_async_copy(wgu_hbm.at[e], wgu_f32.at[p], wsem.at[p, 0]),
                pltpu.make_async_copy(wd_hbm.at[e], wd_f32.at[p], wsem.at[p, 1]))

    def wait_gather(dst):
        for _ in range(OCT_PER_BLOCK):
            pltpu.make_async_copy(octet(xs_ref, 0), octet(dst, 0), gsem).wait()

    def wait_scatter(src):
        for _ in range(OCT_PER_BLOCK):
            pltpu.make_async_copy(octet(src, 0), octet(y_ref, 0), ssem).wait()

    @pl.when(j == 0)
    def _():
        obuf1[...] = jnp.zeros_like(obuf1)
        for c in weight_copies(be_ref[0], 0):
            c.start(priority=WEIGHT_DMA_PRIORITY)
        for o in range(OCT_PER_BLOCK):
            gather_octet(src_cur_ref, o, xbuf0).start(priority=ROW_DMA_PRIORITY)

    first_of_group = jnp.logical_or(j == 0, be_ref[j] != be_ref[jnp.maximum(j - 1, 0)])

    @pl.when(jnp.logical_and(j < nu, first_of_group))
    def _():
        p = wpar_ref[j]
        for c in weight_copies(be_ref[j], p):
            c.wait()
        wgu_bf[...] = wgu_f32[p].astype(BF16)
        wd_bf[...] = wd_f32[p].astype(BF16)

        @pl.when(wnext_ref[j] >= 0)
        def _():
            for c in weight_copies(wnext_ref[j], 1 - p):
                c.start(priority=WEIGHT_DMA_PRIORITY)

    def step(s, compute):
        wait_gather(xbuf[s])

        @pl.when(j >= 1)
        def _():
            wait_scatter(obuf[s])

        def start_scatter():
            for o in range(OCT_PER_BLOCK):
                scatter_octet(dst_prev_ref, o, obuf[1 - s]).start(priority=ROW_DMA_PRIORITY)

        if compute:
            for o in range(OCT_PER_BLOCK):
                gather_octet(src_next_ref, o, xbuf[1 - s]).start(priority=ROW_DMA_PRIORITY)
            x = _load_token_tiles(xbuf[s], ROW_BLOCK).astype(BF16)
            gu = _dot(x, wgu_bf[...]) + bgu_ref[0]
            gate = jnp.minimum(gu[:, :D_EXPERT], SWIGLU_LIMIT)
            up = jnp.clip(gu[:, D_EXPERT:], -SWIGLU_LIMIT, SWIGLU_LIMIT)
            glu = gate * (1.0 / (1.0 + jnp.exp(-SWIGLU_ALPHA * gate)))
            h = ((up + 1.0) * glu).astype(BF16)
            start_scatter()
            _store_token_tiles(obuf[s], _dot(h, wd_bf[...]) + bd_ref[0], ROW_BLOCK)
        else:
            start_scatter()
            wait_scatter(obuf[1 - s])

    for s in range(2):
        parity = (j & 1) == s
        pl.when(jnp.logical_and(j < nu, parity))(functools.partial(step, s, True))
        pl.when(jnp.logical_and(j == nu, parity))(functools.partial(step, s, False))


def _experts(block_e, n_used, w_next, w_par, src3, dst3, xs, wgu, bgu, wd, bd):
    n_steps = block_e.shape[0]
    blk_rows = ROW_BLOCK * TILE_SUBLANES
    last_blk = src3.shape[0] - 1
    exp_map = lambda j, be, nu, wn, wp: (be[j], 0, 0)
    oct_spec = lambda off: pl.BlockSpec((None, 1, OCT_PER_BLOCK),
                                        lambda j, be, nu, wn, wp: (jnp.minimum(j + off, last_blk), 0, 0),
                                        memory_space=pltpu.SMEM)
    hbm = pl.BlockSpec(memory_space=pl.ANY)
    grid_spec = pltpu.PrefetchScalarGridSpec(
        num_scalar_prefetch=4,
        grid=(n_steps,),
        in_specs=[oct_spec(1), oct_spec(2), oct_spec(0),
                  pl.BlockSpec((1, 1, 2 * D_EXPERT), exp_map), pl.BlockSpec((1, 1, D_MODEL), exp_map),
                  hbm, hbm, hbm],
        out_specs=hbm,
        scratch_shapes=[pltpu.VMEM((blk_rows, LANES), F32)] * 4
        + [pltpu.VMEM((2, D_MODEL, 2 * D_EXPERT), F32), pltpu.VMEM((2, D_EXPERT, D_MODEL), F32),
           pltpu.VMEM((D_MODEL, 2 * D_EXPERT), BF16), pltpu.VMEM((D_EXPERT, D_MODEL), BF16),
           pltpu.SemaphoreType.DMA, pltpu.SemaphoreType.DMA, pltpu.SemaphoreType.DMA((2, 2))],
    )
    return pl.pallas_call(
        _expert_kernel,
        grid_spec=grid_spec,
        out_shape=jax.ShapeDtypeStruct(xs.shape, F32),
        input_output_aliases={9: 0},
        compiler_params=pltpu.CompilerParams(dimension_semantics=("arbitrary",), vmem_limit_bytes=VMEM_LIMIT),
        name="experts",
    )(block_e, n_used, w_next, w_par, src3, src3, dst3, bgu, bd, xs, wgu, wd)


def _combine_kernel(q_ref, w_ref, y_ref, x2_ref, g_ref, out_ref, acc_ref):
    def group(grp, c):
        for u in range(TILE_SUBLANES):
            t = grp * TILE_SUBLANES + u
            acc = w_ref[0, t * TOP_K] * _tile_at(y_ref, q_ref[0, t * TOP_K])[...]
            for k in range(1, TOP_K):
                acc = acc + w_ref[0, t * TOP_K + k] * _tile_at(y_ref, q_ref[0, t * TOP_K + k])[...]
            _tile_rows(acc_ref, t)[...] = acc
        return c

    lax.fori_loop(0, TOK_TILE // TILE_SUBLANES, group, 0)
    out_ref[...] = _rms(x2_ref[...] + _load_token_tiles(acc_ref, TOK_TILE), g_ref[...])


def _combine(q_tiles, w_tiles, ybuf, x2, g):
    n_tiles = q_tiles.shape[0]
    region = TILE_OCT * OCT_ROWS * TILE_SUBLANES
    tok4 = pl.BlockSpec((None, 1, TOP_K * TOK_TILE), lambda i: (i, 0, 0), memory_space=pltpu.SMEM)
    return pl.pallas_call(
        _combine_kernel,
        grid=(n_tiles,),
        in_specs=[tok4, tok4,
                  pl.BlockSpec((region, LANES), lambda i: (i, 0)),
                  pl.BlockSpec((TOK_TILE, D_MODEL), lambda i: (i, 0)),
                  pl.BlockSpec((1, D_MODEL), lambda i: (0, 0))],
        out_specs=pl.BlockSpec((TOK_TILE, D_MODEL), lambda i: (i, 0)),
        out_shape=jax.ShapeDtypeStruct(x2.shape, F32),
        scratch_shapes=[pltpu.VMEM((TOK_TILE * TILE_SUBLANES, LANES), F32)],
        compiler_params=pltpu.CompilerParams(dimension_semantics=("parallel",), vmem_limit_bytes=VMEM_LIMIT),
        name="combine",
    )(q_tiles, w_tiles, ybuf, x2, g)


def _layer(x, mem, norm_mix, w_in, conv_w, g_conv_out, g_fft_out, w_out, norm_xattn, norm_mem,
           w_q, w_k, w_v, w_o, norm_ffn, w_router, b_router, w_gate_up, b_gate_up, w_down, b_down, tables):
    bsz, seq, d = x.shape
    t = bsz * seq
    cs, m1, m2 = tables
    row = lambda v: v.reshape(1, -1)

    kmem, vmem_ = _memkv(mem, row(norm_mem), w_k.astype(BF16), w_v.astype(BF16))

    bg, cv, a, s = _inproj(x.reshape(t, d), row(norm_mix), w_in.astype(BF16), cs)
    cols = FFT_N2 * FFT_WIDTH
    g = _fft1(a.reshape(bsz, FFT_N1, cols), s.reshape(bsz, FFT_N1, cols), m1)
    yf = _fft2(g.reshape(bsz, 2, FFT_N1, FFT_N2, FFT_WIDTH), m2)

    w_out_b = w_out.astype(BF16)
    wr_t = w_router.T
    wr_hi = wr_t.astype(BF16)
    wr_lo = (wr_t - wr_hi.astype(F32)).astype(BF16)
    tri = (jnp.arange(TOK_TILE)[:, None] < jnp.arange(TOK_TILE)[None, :]).astype(BF16)
    tri32 = (jnp.arange(N_EXPERTS)[None, :] < jnp.arange(N_EXPERTS)[:, None]).astype(BF16)
    x2, h3, w_tiles, q_tiles, cnt = _post(
        x, bg.reshape(bsz, seq, CONV_WIDTH), cv.reshape(bsz, seq, CONV_WIDTH), yf,
        conv_w, row(g_conv_out), row(g_fft_out), w_out_b[:CONV_WIDTH], w_out_b[CONV_WIDTH:],
        row(norm_xattn), w_q.astype(BF16), kmem, vmem_, w_o.astype(BF16), row(norm_ffn),
        wr_hi, wr_lo, jnp.broadcast_to(b_router[:, None], (N_EXPERTS, 128)), tri, tri32)

    n_tiles = t // TOK_TILE
    n_blocks = (t * TOP_K + n_tiles * N_EXPERTS * (OCT_ROWS - 1)) // ROW_BLOCK + N_EXPERTS
    counts = cnt[:, :, 0].astype(I32).reshape(-1)
    src, dst, block_e, n_used, w_next, w_par = _plan(counts, n_tiles, n_blocks)
    per_block = lambda v: v.reshape(n_blocks + 2, 1, OCT_PER_BLOCK)
    by_token = lambda v: v.transpose(0, 2, 1).reshape(n_tiles, 1, TOK_TILE * TOP_K)
    q_tiles, w_tiles = by_token(q_tiles), by_token(w_tiles)
    xs = _localsort(q_tiles, h3)
    ybuf = _experts(block_e, n_used, w_next, w_par, per_block(src), per_block(dst), xs,
                    w_gate_up, b_gate_up[:, None, :], w_down, b_down[:, None, :])
    return x2.reshape(t, d), ybuf, q_tiles, w_tiles


def kernel(x, mem, norm_mix, w_in, conv_w, g_conv_out, g_fft_out, w_out, norm_xattn, norm_mem, w_q, w_k, w_v, w_o,
           norm_ffn, w_router, b_router, w_gate_up, b_gate_up, w_down, b_down, norm_final):
    bsz, seq, d = x.shape
    depth = norm_mix.shape[0]
    assert depth == 1, "final norm is fused into the combine step of the single layer"
    tables = _dft_tables(seq)
    x2, ybuf, q_tiles, w_tiles = _layer(
        x, mem, norm_mix[0], w_in[0], conv_w[0], g_conv_out[0], g_fft_out[0], w_out[0], norm_xattn[0],
        norm_mem[0], w_q[0], w_k[0], w_v[0], w_o[0], norm_ffn[0], w_router[0], b_router[0],
        w_gate_up[0], b_gate_up[0], w_down[0], b_down[0], tables)
    out = _combine(q_tiles, w_tiles, ybuf, x2, norm_final.reshape(1, -1))
    return out.reshape(bsz, seq, d)
```

```python
import functools
import math

import numpy as np
import jax
import jax.numpy as jnp
from jax import lax
from jax.experimental import pallas as pl
from jax.experimental.pallas import tpu as pltpu

F32 = jnp.float32
BF16 = jnp.bfloat16
I32 = jnp.int32

D_MODEL = 1024
CONV_WIDTH = 512
FFT_WIDTH = 512
GROUP_DIM = 64
IN_PROJ_WIDTH = 3 * CONV_WIDTH + FFT_WIDTH
MEM_LEN = 256
XATTN_HEADS = 4
XATTN_HEAD_DIM = D_MODEL // XATTN_HEADS
N_EXPERTS = 32
TOP_K = 4
D_EXPERT = D_MODEL
SWIGLU_LIMIT = 7.0
SWIGLU_ALPHA = 1.702
EPS = 1e-5

FFT_N1 = 64
FFT_N2 = 128

TOK_TILE = 512
POST_SUB = 2
INPROJ_TILE = 1024
ROW_BLOCK = 512
GATHER_DMA_PRIORITY = 0
SCATTER_DMA_PRIORITY = 1
WEIGHT_DMA_PRIORITY = 1
OCT_ROWS = 8
OCT_PER_BLOCK = ROW_BLOCK // OCT_ROWS
PLAN_UNROLL = 8
TILE_OCT = (TOK_TILE * TOP_K) // OCT_ROWS + N_EXPERTS
FFT1_LANES = 8192
FFT2_K1 = 8
BF16_SUBLANES = 16
TILE_SUBLANES = 8
LANES = 128
assert D_MODEL == TILE_SUBLANES * LANES
VMEM_LIMIT = 56 * 1024 * 1024


def _rms(x, g):
    return x * lax.rsqrt(jnp.mean(x * x, axis=-1, keepdims=True) + EPS) * g


def _dot(a, b):
    return jnp.dot(a, b, preferred_element_type=F32)


def _dot_nt(a, b):
    return lax.dot_general(a, b, (((1,), (1,)), ((), ())), preferred_element_type=F32)


def _load_token_tiles(ref, rows, base=0):
    return jnp.concatenate(
        [ref[pl.ds(base + s, rows, stride=TILE_SUBLANES), :] for s in range(TILE_SUBLANES)], axis=-1)


def _store_token_tiles(ref, val, rows, base=0):
    for s in range(TILE_SUBLANES):
        ref[pl.ds(base + s, rows, stride=TILE_SUBLANES), :] = val[:, s * LANES:(s + 1) * LANES]


def _dft_tables(seq):
    assert seq == FFT_N1 * FFT_N2
    c = np.arange(GROUP_DIM)
    ang = 2.0 * np.pi * ((c[:, None] * c[None, :]) % GROUP_DIM) / GROUP_DIM
    groups = FFT_WIDTH // GROUP_DIM
    eye = np.eye(groups)
    cs = np.concatenate([np.kron(eye, np.cos(ang)), np.kron(eye, np.sin(ang))], axis=1) / math.sqrt(GROUP_DIM)
    n1 = np.arange(FFT_N1)
    a1 = 2.0 * np.pi * ((n1[:, None] * n1[None, :]) % FFT_N1) / FFT_N1
    c1, s1 = np.cos(a1), np.sin(a1)
    m1 = np.block([[c1, -s1], [s1, c1]]) / math.sqrt(FFT_N1)
    k1 = np.arange(FFT_N1)[:, None, None]
    k2 = np.arange(FFT_N2)[None, :, None]
    n2 = np.arange(FFT_N2)[None, None, :]
    a2 = 2.0 * np.pi * ((n2 * (k1 + FFT_N1 * k2)) % seq) / seq
    m2 = np.concatenate([np.cos(a2), -np.sin(a2)], axis=2) / math.sqrt(FFT_N2)
    return (jnp.asarray(cs, F32).astype(BF16), jnp.asarray(m1, F32).astype(BF16),
            jnp.asarray(m2, F32).astype(BF16))


def _memkv_kernel(mem_ref, g_ref, wk_ref, wv_ref, k_ref, v_ref):
    m = _rms(mem_ref[0], g_ref[...]).astype(BF16)
    k_ref[0] = _dot(m, wk_ref[...]).astype(BF16)
    v_ref[0] = _dot(m, wv_ref[...]).astype(BF16)


def _memkv(mem, g, wk, wv):
    bsz = mem.shape[0]
    full = lambda shape: pl.BlockSpec(shape, lambda b: (0,) * len(shape))
    per_b = pl.BlockSpec((1, MEM_LEN, D_MODEL), lambda b: (b, 0, 0))
    return pl.pallas_call(
        _memkv_kernel,
        grid=(bsz,),
        in_specs=[per_b, full((1, D_MODEL)), full((D_MODEL, D_MODEL)), full((D_MODEL, D_MODEL))],
        out_specs=[per_b, per_b],
        out_shape=[jax.ShapeDtypeStruct((bsz, MEM_LEN, D_MODEL), BF16)] * 2,
        compiler_params=pltpu.CompilerParams(dimension_semantics=("arbitrary",), vmem_limit_bytes=VMEM_LIMIT),
        name="memkv",
    )(mem, g, wk, wv)


def _inproj_kernel(x_ref, g_ref, win_ref, cs_ref, b_ref, cv_ref, a_ref, s_ref):
    h = _rms(x_ref[...], g_ref[...]).astype(BF16)
    z = _dot(h, win_ref[...])
    b_ref[...] = z[:, :CONV_WIDTH].astype(BF16)
    cv_ref[...] = (z[:, CONV_WIDTH:2 * CONV_WIDTH] * z[:, 2 * CONV_WIDTH:3 * CONV_WIDTH]).astype(BF16)
    u = z[:, 3 * CONV_WIDTH:].astype(BF16)
    ab = _dot(u, cs_ref[...])
    a_ref[...] = ab[:, :FFT_WIDTH].astype(BF16)
    s_ref[...] = ab[:, FFT_WIDTH:].astype(BF16)


def _inproj(x2d, g, w_in, cs):
    t = x2d.shape[0]
    tile = pl.BlockSpec((INPROJ_TILE, D_MODEL), lambda i: (i, 0))
    half = pl.BlockSpec((INPROJ_TILE, CONV_WIDTH), lambda i: (i, 0))
    full = lambda shape: pl.BlockSpec(shape, lambda i: (0,) * len(shape))
    return pl.pallas_call(
        _inproj_kernel,
        grid=(t // INPROJ_TILE,),
        in_specs=[tile, full((1, D_MODEL)), full((D_MODEL, IN_PROJ_WIDTH)), full((FFT_WIDTH, 2 * FFT_WIDTH))],
        out_specs=[half] * 4,
        out_shape=[jax.ShapeDtypeStruct((t, CONV_WIDTH), BF16)] * 4,
        compiler_params=pltpu.CompilerParams(dimension_semantics=("parallel",), vmem_limit_bytes=VMEM_LIMIT),
        name="inproj",
    )(x2d, g, w_in, cs)


def _fft1_kernel(a_ref, s_ref, m1_ref, g_ref):
    x = jnp.concatenate([a_ref[0], s_ref[0]], axis=0)
    g_ref[0] = _dot(m1_ref[...], x).astype(BF16)


def _fft1(a3, s3, m1):
    bsz, _, cols = a3.shape
    blk = pl.BlockSpec((1, FFT_N1, FFT1_LANES), lambda b, j: (b, 0, j))
    return pl.pallas_call(
        _fft1_kernel,
        grid=(bsz, cols // FFT1_LANES),
        in_specs=[blk, blk, pl.BlockSpec((2 * FFT_N1, 2 * FFT_N1), lambda b, j: (0, 0))],
        out_specs=pl.BlockSpec((1, 2 * FFT_N1, FFT1_LANES), lambda b, j: (b, 0, j)),
        out_shape=jax.ShapeDtypeStruct((bsz, 2 * FFT_N1, cols), BF16),
        compiler_params=pltpu.CompilerParams(dimension_semantics=("parallel", "parallel"),
                                             vmem_limit_bytes=VMEM_LIMIT),
        name="fft1",
    )(a3, s3, m1)


def _fft2_kernel(g_ref, m2_ref, y_ref):
    for j in range(FFT2_K1):
        x = jnp.concatenate([g_ref[0, 0, j], g_ref[0, 1, j]], axis=0)
        y_ref[0, :, j * FFT_WIDTH:(j + 1) * FFT_WIDTH] = _dot(m2_ref[j], x).astype(BF16)


def _fft2(g5, m2):
    bsz = g5.shape[0]
    return pl.pallas_call(
        _fft2_kernel,
        grid=(bsz, FFT_N1 // FFT2_K1),
        in_specs=[pl.BlockSpec((1, 2, FFT2_K1, FFT_N2, FFT_WIDTH), lambda b, j: (b, 0, j, 0, 0)),
                  pl.BlockSpec((FFT2_K1, FFT_N2, 2 * FFT_N2), lambda b, j: (j, 0, 0))],
        out_specs=pl.BlockSpec((1, FFT_N2, FFT2_K1 * FFT_WIDTH), lambda b, j: (b, 0, j)),
        out_shape=jax.ShapeDtypeStruct((bsz, FFT_N2, FFT_N1 * FFT_WIDTH), BF16),
        compiler_params=pltpu.CompilerParams(dimension_semantics=("parallel", "parallel"),
                                             vmem_limit_bytes=VMEM_LIMIT),
        name="fft2",
    )(g5, m2)


def _post_kernel(x_ref, bg_ref, cv_ref, cvp_ref, cvn_ref, yf_ref, convw_ref, gc_ref, gf_ref,
                 wot_ref, wob_ref, nx_ref, wq_ref, k_ref, v_ref, wo_ref, nf_ref,
                 wrh_ref, wrl_ref, br_ref, tri_ref, tri32_ref,
                 x2_ref, h3_ref, w_ref, q_ref, cnt_ref):
    for sub in range(POST_SUB):
        _post_tile(sub, x_ref, bg_ref, cv_ref, cvp_ref, cvn_ref, yf_ref, convw_ref, gc_ref, gf_ref,
                   wot_ref, wob_ref, nx_ref, wq_ref, k_ref, v_ref, wo_ref, nf_ref,
                   wrh_ref, wrl_ref, br_ref, tri_ref, tri32_ref,
                   x2_ref, h3_ref, w_ref, q_ref, cnt_ref)


def _post_tile(sub, x_ref, bg_ref, cv_ref, cvp_ref, cvn_ref, yf_ref, convw_ref, gc_ref, gf_ref,
               wot_ref, wob_ref, nx_ref, wq_ref, k_ref, v_ref, wo_ref, nf_ref,
               wrh_ref, wrl_ref, br_ref, tri_ref, tri32_ref,
               x2_ref, h3_ref, w_ref, q_ref, cnt_ref):
    i = pl.program_id(1)
    last = pl.num_programs(1) - 1
    r0 = sub * TOK_TILE
    tile = slice(r0, r0 + TOK_TILE)

    cv = cv_ref[0, tile].astype(F32)
    if sub == 0:
        prev_row = jnp.where(i > 0, cvp_ref[0].astype(F32)[BF16_SUBLANES - 1:BF16_SUBLANES, :], 0.0)
    else:
        prev_row = cv_ref[0, r0 - BF16_SUBLANES:r0].astype(F32)[BF16_SUBLANES - 1:BF16_SUBLANES, :]
    if sub == POST_SUB - 1:
        next_row = jnp.where(i < last, cvn_ref[0].astype(F32)[0:1, :], 0.0)
    else:
        next_row = cv_ref[0, r0 + TOK_TILE:r0 + TOK_TILE + BF16_SUBLANES].astype(F32)[0:1, :]
    rows = lax.broadcasted_iota(I32, cv.shape, 0)
    cvm1 = jnp.where(rows == 0, prev_row, pltpu.roll(cv, 1, axis=0))
    cvp1 = jnp.where(rows == TOK_TILE - 1, next_row, pltpu.roll(cv, TOK_TILE - 1, axis=0))
    cw = convw_ref[...]
    y_conv = bg_ref[0, tile].astype(F32) * (cw[0:1] * cvm1 + cw[1:2] * cv + cw[2:3] * cvp1)
    yc_n = _rms(y_conv, gc_ref[...]).astype(BF16)
    yf_n = _rms(yf_ref[0, tile].astype(F32), gf_ref[...]).astype(BF16)
    x1 = x_ref[0, tile] + _dot(yc_n, wot_ref[...]) + _dot(yf_n, wob_ref[...])

    h2 = _rms(x1, nx_ref[...]).astype(BF16)
    q = _dot(h2, wq_ref[...]).astype(BF16)
    heads = []
    for hd in range(XATTN_HEADS):
        sl = slice(hd * XATTN_HEAD_DIM, (hd + 1) * XATTN_HEAD_DIM)
        s = _dot_nt(q[:, sl], k_ref[0, :, sl]) * (XATTN_HEAD_DIM ** -0.5)
        e = jnp.exp(s - jnp.max(s, axis=-1, keepdims=True))
        p = e * (1.0 / jnp.sum(e, axis=-1, keepdims=True))
        heads.append(_dot(p.astype(BF16), v_ref[0, :, sl]).astype(BF16))
    x2 = x1 + _dot(jnp.concatenate(heads, axis=-1), wo_ref[...])
    x2_ref[0, tile] = x2

    h3 = _rms(x2, nf_ref[...])
    _store_token_tiles(h3_ref, h3, TOK_TILE, base=r0 * TILE_SUBLANES)
    h3h = h3.astype(BF16)
    h3l = (h3 - h3h.astype(F32)).astype(BF16)
    logits = (_dot_nt(wrh_ref[...], h3h) + _dot_nt(wrh_ref[...], h3l) + _dot_nt(wrl_ref[...], h3h)
              + br_ref[:, 0:1])
    eidx = lax.broadcasted_iota(I32, logits.shape, 0)
    work = logits
    vals, idxs = [], []
    for _ in range(TOP_K):
        m = jnp.max(work, axis=0, keepdims=True)
        ik = jnp.min(jnp.where(work == m, eidx, N_EXPERTS), axis=0, keepdims=True)
        vals.append(m)
        idxs.append(ik)
        work = jnp.where(eidx == ik, -jnp.inf, work)
    ex = [jnp.exp(v - vals[0]) for v in vals]
    inv_den = 1.0 / (ex[0] + ex[1] + ex[2] + ex[3])
    w_ref[sub] = jnp.concatenate([e * inv_den for e in ex], axis=0)

    sel = jnp.zeros(logits.shape, F32)
    for ik in idxs:
        sel = sel + jnp.where(eidx == ik, 1.0, 0.0)
    cnt = jnp.broadcast_to(jnp.sum(sel, axis=1, keepdims=True), (N_EXPERTS, LANES))
    cnt_ref[sub] = cnt
    seg_rows = jnp.floor((cnt + (OCT_ROWS - 1)) * (1.0 / OCT_ROWS)) * OCT_ROWS
    seg_start = _dot(tri32_ref[...], seg_rows.astype(BF16))
    pos = seg_start[:, 0:1] + _dot(sel.astype(BF16), tri_ref[...])
    q_ref[sub] = jnp.concatenate(
        [jnp.sum(jnp.where(eidx == ik, pos, 0.0), axis=0, keepdims=True) for ik in idxs],
        axis=0).astype(I32) * TILE_SUBLANES


def _post(x3, bg, cv, yf, conv_w, gc, gf, wo_top, wo_bot, nx, wq, kmem, vmem_, wo, nf, wrh, wrl, br, tri, tri32):
    bsz, seq, _ = x3.shape
    step_rows = POST_SUB * TOK_TILE
    nt = seq // step_rows
    t = bsz * seq
    n_tiles = t // TOK_TILE
    halo_per_tile = step_rows // BF16_SUBLANES
    n_halo = seq // BF16_SUBLANES
    full = lambda shape: pl.BlockSpec(shape, lambda b, i: (0,) * len(shape))
    tile_d = pl.BlockSpec((1, step_rows, D_MODEL), lambda b, i: (b, i, 0))
    tile_h = pl.BlockSpec((1, step_rows, CONV_WIDTH), lambda b, i: (b, i, 0))
    halo_prev = pl.BlockSpec((1, BF16_SUBLANES, CONV_WIDTH),
                             lambda b, i: (b, jnp.maximum(i * halo_per_tile - 1, 0), 0))
    halo_next = pl.BlockSpec((1, BF16_SUBLANES, CONV_WIDTH),
                             lambda b, i: (b, jnp.minimum((i + 1) * halo_per_tile, n_halo - 1), 0))
    mem_blk = pl.BlockSpec((1, MEM_LEN, D_MODEL), lambda b, i: (b, 0, 0))
    tok4 = pl.BlockSpec((POST_SUB, TOP_K, TOK_TILE), lambda b, i: (b * nt + i, 0, 0))
    per_tile = pl.BlockSpec((POST_SUB, N_EXPERTS, LANES), lambda b, i: (b * nt + i, 0, 0))
    tok_tiles = pl.BlockSpec((step_rows * TILE_SUBLANES, LANES), lambda b, i: (b * nt + i, 0))
    return pl.pallas_call(
        _post_kernel,
        grid=(bsz, nt),
        in_specs=[tile_d, tile_h, tile_h, halo_prev, halo_next, tile_h,
                  full((3, CONV_WIDTH)), full((1, CONV_WIDTH)), full((1, FFT_WIDTH)),
                  full((CONV_WIDTH, D_MODEL)), full((FFT_WIDTH, D_MODEL)), full((1, D_MODEL)),
                  full((D_MODEL, D_MODEL)), mem_blk, mem_blk, full((D_MODEL, D_MODEL)), full((1, D_MODEL)),
                  full((N_EXPERTS, D_MODEL)), full((N_EXPERTS, D_MODEL)), full((N_EXPERTS, 128)),
                  full((TOK_TILE, TOK_TILE)), full((N_EXPERTS, N_EXPERTS))],
        out_specs=[tile_d, tok_tiles, tok4, tok4, per_tile],
        out_shape=[jax.ShapeDtypeStruct((bsz, seq, D_MODEL), F32),
                   jax.ShapeDtypeStruct((t * TILE_SUBLANES, LANES), F32),
                   jax.ShapeDtypeStruct((n_tiles, TOP_K, TOK_TILE), F32),
                   jax.ShapeDtypeStruct((n_tiles, TOP_K, TOK_TILE), I32),
                   jax.ShapeDtypeStruct((n_tiles, N_EXPERTS, LANES), F32)],
        compiler_params=pltpu.CompilerParams(dimension_semantics=("parallel", "parallel"),
                                             vmem_limit_bytes=VMEM_LIMIT),
        name="post",
    )(x3, bg, cv, cv, cv, yf, conv_w, gc, gf, wo_top, wo_bot, nx, wq, kmem, vmem_, wo, nf, wrh, wrl, br, tri, tri32)


def _plan_kernel(cnt_ref, src_ref, dst_ref, be_ref, nu_ref, wnext_ref, wpar_ref, off_ref, nxt_ref, *, n_tiles):
    n_sched = be_ref.shape[0]
    dump0 = n_tiles * TILE_OCT

    def fill_pad(lo, hi):
        def body(p, c):
            o = p & (OCT_PER_BLOCK - 1)
            parity = lax.shift_right_logical(p, OCT_PER_BLOCK.bit_length() - 1) & 1
            src_ref[p] = dump0 + 2 * OCT_PER_BLOCK + o
            dst_ref[p] = dump0 + parity * OCT_PER_BLOCK + o
            return c
        lax.fori_loop(lo, hi, body, 0)

    def clear(tile, c):
        off_ref[tile] = 0
        return c

    lax.fori_loop(0, n_tiles, clear, 0)

    def find_next(i, nxt):
        e = N_EXPERTS - 1 - i
        nxt_ref[e] = nxt
        total = lax.fori_loop(0, n_tiles, lambda tile, s: s + cnt_ref[tile * N_EXPERTS + e], jnp.int32(0))
        return jnp.where(total > 0, e, nxt)

    lax.fori_loop(0, N_EXPERTS, find_next, jnp.int32(-1))
    fill_pad(0, OCT_PER_BLOCK)

    def per_expert(e, carry):
        pos0, blk0, group = carry

        def per_tile(tile, pos):
            n_oct = lax.shift_right_logical(cnt_ref[tile * N_EXPERTS + e] + (OCT_ROWS - 1), OCT_ROWS.bit_length() - 1)
            base = tile * TILE_OCT + off_ref[tile]
            off_ref[tile] = off_ref[tile] + n_oct

            def per_group(grp, c):
                for u in range(PLAN_UNROLL):
                    o = grp * PLAN_UNROLL + u
                    src_ref[pos + o] = base + o
                    dst_ref[pos + o] = base + o
                return c

            per_group(0, 0)
            per_group(1, 0)
            lax.fori_loop(2, lax.shift_right_logical(n_oct + (PLAN_UNROLL - 1), PLAN_UNROLL.bit_length() - 1),
                          per_group, 0)
            return pos + n_oct

        pos1 = lax.fori_loop(0, n_tiles, per_tile, pos0)
        nb = lax.shift_right_logical(pos1 - pos0 + (OCT_PER_BLOCK - 1), OCT_PER_BLOCK.bit_length() - 1)
        pos2 = pos0 + nb * OCT_PER_BLOCK
        fill_pad(pos1, pos2)

        def fill(j, c):
            be_ref[blk0 + j] = e
            wnext_ref[blk0 + j] = nxt_ref[e]
            wpar_ref[blk0 + j] = group & 1
            return c

        lax.fori_loop(0, nb, fill, 0)
        return pos2, blk0 + nb, group + jnp.where(nb > 0, 1, 0)

    pos, n_used, _ = lax.fori_loop(0, N_EXPERTS, per_expert,
                                   (jnp.int32(OCT_PER_BLOCK), jnp.int32(0), jnp.int32(0)))
    nu_ref[0] = n_used
    tail_e = be_ref[n_used - 1]

    def tail(j, c):
        be_ref[j] = tail_e
        wnext_ref[j] = -1
        wpar_ref[j] = 0
        return c

    lax.fori_loop(n_used, n_sched, tail, 0)
    fill_pad(pos, src_ref.shape[0])


def _plan(counts, n_tiles, n_blocks):
    smem = pl.BlockSpec(memory_space=pltpu.SMEM)
    grid_spec = pltpu.PrefetchScalarGridSpec(
        num_scalar_prefetch=1,
        grid=(1,),
        in_specs=[],
        out_specs=[smem] * 6,
        scratch_shapes=[pltpu.SMEM((n_tiles,), I32), pltpu.SMEM((N_EXPERTS,), I32)],
    )
    octs = jax.ShapeDtypeStruct(((n_blocks + 2) * OCT_PER_BLOCK,), I32)
    sched = jax.ShapeDtypeStruct((n_blocks + 1,), I32)
    return pl.pallas_call(
        functools.partial(_plan_kernel, n_tiles=n_tiles),
        grid_spec=grid_spec,
        out_shape=[octs, octs, sched, jax.ShapeDtypeStruct((1,), I32), sched, sched],
        compiler_params=pltpu.CompilerParams(dimension_semantics=("arbitrary",), vmem_limit_bytes=VMEM_LIMIT),
        name="plan",
    )(counts)


def _localsort_kernel(q_ref, h3_ref, xs_ref):
    xs_ref[...] = jnp.zeros_like(xs_ref)

    @pl.when(pl.program_id(0) < pl.num_programs(0) - 1)
    def _():
        def group(g, c):
            for u in range(TILE_SUBLANES):
                t = g * TILE_SUBLANES + u
                row = _tile_rows(h3_ref, t)[...]
                for k in range(TOP_K):
                    _tile_at(xs_ref, q_ref[0, t * TOP_K + k])[...] = row
            return c

        lax.fori_loop(0, TOK_TILE // TILE_SUBLANES, group, 0)


def _localsort(q_tiles, h3t):
    n_tiles = q_tiles.shape[0]
    last = n_tiles - 1
    region = TILE_OCT * OCT_ROWS * TILE_SUBLANES
    return pl.pallas_call(
        _localsort_kernel,
        grid=(n_tiles + 1,),
        in_specs=[pl.BlockSpec((None, 1, TOP_K * TOK_TILE), lambda i: (jnp.minimum(i, last), 0, 0),
                               memory_space=pltpu.SMEM),
                  pl.BlockSpec((TOK_TILE * TILE_SUBLANES, LANES), lambda i: (jnp.minimum(i, last), 0))],
        out_specs=pl.BlockSpec((region, LANES), lambda i: (i, 0)),
        out_shape=jax.ShapeDtypeStruct(((n_tiles + 1) * region, LANES), F32),
        compiler_params=pltpu.CompilerParams(dimension_semantics=("parallel",), vmem_limit_bytes=VMEM_LIMIT),
        name="localsort",
    )(q_tiles, h3t)


def _tile_at(ref, sublane_row):
    return ref.at[pl.ds(pl.multiple_of(sublane_row, TILE_SUBLANES), TILE_SUBLANES), :]


def _tile_rows(ref, row, n_rows=1):
    return ref.at[pl.ds(pl.multiple_of(row * TILE_SUBLANES, TILE_SUBLANES), n_rows * TILE_SUBLANES), :]


def _expert_kernel(be_ref, nu_ref, wnext_ref, wpar_ref,
                   src_cur_ref, src_next_ref, dst_prev_ref, bgu_ref, bd_ref, xs_ref, wgu_hbm, wd_hbm,
                   y_ref,
                   xbuf0, xbuf1, obuf0, obuf1, wgu_f32, wd_f32, wgu_bf, wd_bf, gsem, ssem, wsem):
    j = pl.program_id(0)
    nu = nu_ref[0]
    xbuf = (xbuf0, xbuf1)
    obuf = (obuf0, obuf1)
    octet = lambda ref, o: _tile_rows(ref, o * OCT_ROWS, OCT_ROWS)

    def gather_octet(src_ref, o, dst):
        return pltpu.make_async_copy(octet(xs_ref, src_ref[0, o]), octet(dst, o), gsem)

    def scatter_octet(dst_ref, o, src):
        return pltpu.make_async_copy(octet(src, o), octet(y_ref, dst_ref[0, o]), ssem)

    def weight_copies(e, p):
        return (pltpu.make_async_copy(wgu_hbm.at[e], wgu_f32.at[p], wsem.at[p, 0]),
                pltpu.make_async_copy(wd_hbm.at[e], wd_f32.at[p], wsem.at[p, 1]))

    def wait_gather(dst):
        for _ in range(OCT_PER_BLOCK):
            pltpu.make_async_copy(octet(xs_ref, 0), octet(dst, 0), gsem).wait()

    def wait_scatter(src):
        for _ in range(OCT_PER_BLOCK):
            pltpu.make_async_copy(octet(src, 0), octet(y_ref, 0), ssem).wait()

    @pl.when(j == 0)
    def _():
        obuf1[...] = jnp.zeros_like(obuf1)
        for c in weight_copies(be_ref[0], 0):
            c.start(priority=WEIGHT_DMA_PRIORITY)
        for o in range(OCT_PER_BLOCK):
            gather_octet(src_cur_ref, o, xbuf0).start(priority=GATHER_DMA_PRIORITY)

    first_of_group = jnp.logical_or(j == 0, be_ref[j] != be_ref[jnp.maximum(j - 1, 0)])

    @pl.when(jnp.logical_and(j < nu, first_of_group))
    def _():
        p = wpar_ref[j]
        for c in weight_copies(be_ref[j], p):
            c.wait()
        wgu_bf[...] = wgu_f32[p].astype(BF16)
        wd_bf[...] = wd_f32[p].astype(BF16)

        @pl.when(wnext_ref[j] >= 0)
        def _():
            for c in weight_copies(wnext_ref[j], 1 - p):
                c.start(priority=WEIGHT_DMA_PRIORITY)

    def step(s, compute):
        wait_gather(xbuf[s])

        @pl.when(j >= 1)
        def _():
            wait_scatter(obuf[s])

        def start_scatter():
            for o in range(OCT_PER_BLOCK):
                scatter_octet(dst_prev_ref, o, obuf[1 - s]).start(priority=SCATTER_DMA_PRIORITY)

        if compute:
            for o in range(OCT_PER_BLOCK):
                gather_octet(src_next_ref, o, xbuf[1 - s]).start(priority=GATHER_DMA_PRIORITY)
            x = _load_token_tiles(xbuf[s], ROW_BLOCK).astype(BF16)
            gu = _dot(x, wgu_bf[...]) + bgu_ref[0]
            gate = jnp.minimum(gu[:, :D_EXPERT], SWIGLU_LIMIT)
            up = jnp.clip(gu[:, D_EXPERT:], -SWIGLU_LIMIT, SWIGLU_LIMIT)
            glu = gate * (1.0 / (1.0 + jnp.exp(-SWIGLU_ALPHA * gate)))
            h = ((up + 1.0) * glu).astype(BF16)
            start_scatter()
            _store_token_tiles(obuf[s], _dot(h, wd_bf[...]) + bd_ref[0], ROW_BLOCK)
        else:
            start_scatter()
            wait_scatter(obuf[1 - s])

    for s in range(2):
        parity = (j & 1) == s
        pl.when(jnp.logical_and(j < nu, parity))(functools.partial(step, s, True))
        pl.when(jnp.logical_and(j == nu, parity))(functools.partial(step, s, False))


def _experts(block_e, n_used, w_next, w_par, src3, dst3, xs, wgu, bgu, wd, bd):
    n_steps = block_e.shape[0]
    blk_rows = ROW_BLOCK * TILE_SUBLANES
    last_blk = src3.shape[0] - 1
    exp_map = lambda j, be, nu, wn, wp: (be[j], 0, 0)
    oct_spec = lambda off: pl.BlockSpec((None, 1, OCT_PER_BLOCK),
                                        lambda j, be, nu, wn, wp: (jnp.minimum(j + off, last_blk), 0, 0),
                                        memory_space=pltpu.SMEM)
    hbm = pl.BlockSpec(memory_space=pl.ANY)
    grid_spec = pltpu.PrefetchScalarGridSpec(
        num_scalar_prefetch=4,
        grid=(n_steps,),
        in_specs=[oct_spec(1), oct_spec(2), oct_spec(0),
                  pl.BlockSpec((1, 1, 2 * D_EXPERT), exp_map), pl.BlockSpec((1, 1, D_MODEL), exp_map),
                  hbm, hbm, hbm],
        out_specs=hbm,
        scratch_shapes=[pltpu.VMEM((blk_rows, LANES), F32)] * 4
        + [pltpu.VMEM((2, D_MODEL, 2 * D_EXPERT), F32), pltpu.VMEM((2, D_EXPERT, D_MODEL), F32),
           pltpu.VMEM((D_MODEL, 2 * D_EXPERT), BF16), pltpu.VMEM((D_EXPERT, D_MODEL), BF16),
           pltpu.SemaphoreType.DMA, pltpu.SemaphoreType.DMA, pltpu.SemaphoreType.DMA((2, 2))],
    )
    return pl.pallas_call(
        _expert_kernel,
        grid_spec=grid_spec,
        out_shape=jax.ShapeDtypeStruct(xs.shape, F32),
        input_output_aliases={9: 0},
        compiler_params=pltpu.CompilerParams(dimension_semantics=("arbitrary",), vmem_limit_bytes=VMEM_LIMIT),
        name="experts",
    )(block_e, n_used, w_next, w_par, src3, src3, dst3, bgu, bd, xs, wgu, wd)


def _combine_kernel(q_ref, w_ref, y_ref, x2_ref, g_ref, out_ref, acc_ref):
    def group(grp, c):
        for u in range(TILE_SUBLANES):
            t = grp * TILE_SUBLANES + u
            acc = w_ref[0, t * TOP_K] * _tile_at(y_ref, q_ref[0, t * TOP_K])[...]
            for k in range(1, TOP_K):
                acc = acc + w_ref[0, t * TOP_K + k] * _tile_at(y_ref, q_ref[0, t * TOP_K + k])[...]
            _tile_rows(acc_ref, t)[...] = acc
        return c

    lax.fori_loop(0, TOK_TILE // TILE_SUBLANES, group, 0)
    out_ref[...] = _rms(x2_ref[...] + _load_token_tiles(acc_ref, TOK_TILE), g_ref[...])


def _combine(q_tiles, w_tiles, ybuf, x2, g):
    n_tiles = q_tiles.shape[0]
    region = TILE_OCT * OCT_ROWS * TILE_SUBLANES
    tok4 = pl.BlockSpec((None, 1, TOP_K * TOK_TILE), lambda i: (i, 0, 0), memory_space=pltpu.SMEM)
    return pl.pallas_call(
        _combine_kernel,
        grid=(n_tiles,),
        in_specs=[tok4, tok4,
                  pl.BlockSpec((region, LANES), lambda i: (i, 0)),
                  pl.BlockSpec((TOK_TILE, D_MODEL), lambda i: (i, 0)),
                  pl.BlockSpec((1, D_MODEL), lambda i: (0, 0))],
        out_specs=pl.BlockSpec((TOK_TILE, D_MODEL), lambda i: (i, 0)),
        out_shape=jax.ShapeDtypeStruct(x2.shape, F32),
        scratch_shapes=[pltpu.VMEM((TOK_TILE * TILE_SUBLANES, LANES), F32)],
        compiler_params=pltpu.CompilerParams(dimension_semantics=("parallel",), vmem_limit_bytes=VMEM_LIMIT),
        name="combine",
    )(q_tiles, w_tiles, ybuf, x2, g)


def _layer(x, mem, norm_mix, w_in, conv_w, g_conv_out, g_fft_out, w_out, norm_xattn, norm_mem,
           w_q, w_k, w_v, w_o, norm_ffn, w_router, b_router, w_gate_up, b_gate_up, w_down, b_down, tables):
    bsz, seq, d = x.shape
    t = bsz * seq
    cs, m1, m2 = tables
    row = lambda v: v.reshape(1, -1)

    kmem, vmem_ = _memkv(mem, row(norm_mem), w_k.astype(BF16), w_v.astype(BF16))

    bg, cv, a, s = _inproj(x.reshape(t, d), row(norm_mix), w_in.astype(BF16), cs)
    cols = FFT_N2 * FFT_WIDTH
    g = _fft1(a.reshape(bsz, FFT_N1, cols), s.reshape(bsz, FFT_N1, cols), m1)
    yf = _fft2(g.reshape(bsz, 2, FFT_N1, FFT_N2, FFT_WIDTH), m2).reshape(bsz, seq, FFT_WIDTH)

    w_out_b = w_out.astype(BF16)
    wr_t = w_router.T
    wr_hi = wr_t.astype(BF16)
    wr_lo = (wr_t - wr_hi.astype(F32)).astype(BF16)
    tri = (jnp.arange(TOK_TILE)[:, None] < jnp.arange(TOK_TILE)[None, :]).astype(BF16)
    tri32 = (jnp.arange(N_EXPERTS)[None, :] < jnp.arange(N_EXPERTS)[:, None]).astype(BF16)
    x2, h3, w_tiles, q_tiles, cnt = _post(
        x, bg.reshape(bsz, seq, CONV_WIDTH), cv.reshape(bsz, seq, CONV_WIDTH), yf,
        conv_w, row(g_conv_out), row(g_fft_out), w_out_b[:CONV_WIDTH], w_out_b[CONV_WIDTH:],
        row(norm_xattn), w_q.astype(BF16), kmem, vmem_, w_o.astype(BF16), row(norm_ffn),
        wr_hi, wr_lo, jnp.broadcast_to(b_router[:, None], (N_EXPERTS, 128)), tri, tri32)

    n_tiles = t // TOK_TILE
    n_blocks = (t * TOP_K + n_tiles * N_EXPERTS * (OCT_ROWS - 1)) // ROW_BLOCK + N_EXPERTS
    counts = cnt[:, :, 0].astype(I32).reshape(-1)
    src, dst, block_e, n_used, w_next, w_par = _plan(counts, n_tiles, n_blocks)
    per_block = lambda v: v.reshape(n_blocks + 2, 1, OCT_PER_BLOCK)
    by_token = lambda v: v.transpose(0, 2, 1).reshape(n_tiles, 1, TOK_TILE * TOP_K)
    q_tiles, w_tiles = by_token(q_tiles), by_token(w_tiles)
    xs = _localsort(q_tiles, h3)
    ybuf = _experts(block_e, n_used, w_next, w_par, per_block(src), per_block(dst), xs,
                    w_gate_up, b_gate_up[:, None, :], w_down, b_down[:, None, :])
    return x2.reshape(t, d), ybuf, q_tiles, w_tiles


def kernel(x, mem, norm_mix, w_in, conv_w, g_conv_out, g_fft_out, w_out, norm_xattn, norm_mem, w_q, w_k, w_v, w_o,
           norm_ffn, w_router, b_router, w_gate_up, b_gate_up, w_down, b_down, norm_final):
    bsz, seq, d = x.shape
    depth = norm_mix.shape[0]
    assert depth == 1, "final norm is fused into the combine step of the single layer"
    tables = _dft_tables(seq)
    x2, ybuf, q_tiles, w_tiles = _layer(
        x, mem, norm_mix[0], w_in[0], conv_w[0], g_conv_out[0], g_fft_out[0], w_out[0], norm_xattn[0],
        norm_mem[0], w_q[0], w_k[0], w_v[0], w_o[0], norm_ffn[0], w_router[0], b_router[0],
        w_gate_up[0], b_gate_up[0], w_down[0], b_down[0], tables)
    out = _combine(q_tiles, w_tiles, ybuf, x2, norm_final.reshape(1, -1))
    return out.reshape(bsz, seq, d)
```

```python
import functools
import math

import numpy as np
import jax
import jax.numpy as jnp
from jax import lax
from jax.experimental import pallas as pl
from jax.experimental.pallas import tpu as pltpu

F32 = jnp.float32
BF16 = jnp.bfloat16
I32 = jnp.int32

D_MODEL = 1024
CONV_WIDTH = 512
FFT_WIDTH = 512
GROUP_DIM = 64
IN_PROJ_WIDTH = 3 * CONV_WIDTH + FFT_WIDTH
MEM_LEN = 256
XATTN_HEADS = 4
XATTN_HEAD_DIM = D_MODEL // XATTN_HEADS
N_EXPERTS = 32
TOP_K = 4
D_EXPERT = D_MODEL
SWIGLU_LIMIT = 7.0
SWIGLU_ALPHA = 1.702
EPS = 1e-5

FFT_N1 = 64
FFT_N2 = 128

TOK_TILE = 512
POST_SUB = 2
INPROJ_TILE = 1024
ROW_BLOCK = 512
GATHER_DMA_PRIORITY = 0
SCATTER_DMA_PRIORITY = 1
WEIGHT_DMA_PRIORITY = 1
OCT_ROWS = 8
OCT_PER_BLOCK = ROW_BLOCK // OCT_ROWS
PLAN_UNROLL = 8
TILE_OCT = (TOK_TILE * TOP_K) // OCT_ROWS + N_EXPERTS
FFT1_LANES = 8192
FFT2_K1 = 8
BF16_SUBLANES = 16
TILE_SUBLANES = 8
LANES = 128
assert D_MODEL == TILE_SUBLANES * LANES
VMEM_LIMIT = 56 * 1024 * 1024


def _rms(x, g):
    return x * lax.rsqrt(jnp.mean(x * x, axis=-1, keepdims=True) + EPS) * g


def _dot(a, b):
    return jnp.dot(a, b, preferred_element_type=F32)


def _dot_nt(a, b):
    return lax.dot_general(a, b, (((1,), (1,)), ((), ())), preferred_element_type=F32)


def _load_token_tiles(ref, rows, base=0):
    return jnp.concatenate(
        [ref[pl.ds(base + s, rows, stride=TILE_SUBLANES), :] for s in range(TILE_SUBLANES)], axis=-1)


def _store_token_tiles(ref, val, rows, base=0):
    for s in range(TILE_SUBLANES):
        ref[pl.ds(base + s, rows, stride=TILE_SUBLANES), :] = val[:, s * LANES:(s + 1) * LANES]


def _dft_tables(seq):
    assert seq == FFT_N1 * FFT_N2
    c = np.arange(GROUP_DIM)
    ang = 2.0 * np.pi * ((c[:, None] * c[None, :]) % GROUP_DIM) / GROUP_DIM
    groups = FFT_WIDTH // GROUP_DIM
    eye = np.eye(groups)
    cs = np.concatenate([np.kron(eye, np.cos(ang)), np.kron(eye, np.sin(ang))], axis=1) / math.sqrt(GROUP_DIM)
    n1 = np.arange(FFT_N1)
    a1 = 2.0 * np.pi * ((n1[:, None] * n1[None, :]) % FFT_N1) / FFT_N1
    c1, s1 = np.cos(a1), np.sin(a1)
    m1 = np.block([[c1, -s1], [s1, c1]]) / math.sqrt(FFT_N1)
    k1 = np.arange(FFT_N1)[:, None, None]
    k2 = np.arange(FFT_N2)[None, :, None]
    n2 = np.arange(FFT_N2)[None, None, :]
    a2 = 2.0 * np.pi * ((n2 * (k1 + FFT_N1 * k2)) % seq) / seq
    m2 = np.concatenate([np.cos(a2), -np.sin(a2)], axis=2) / math.sqrt(FFT_N2)
    return (jnp.asarray(cs, F32).astype(BF16), jnp.asarray(m1, F32).astype(BF16),
            jnp.asarray(m2, F32).astype(BF16))


def _memkv_kernel(mem_ref, g_ref, wk_ref, wv_ref, k_ref, v_ref):
    m = _rms(mem_ref[0], g_ref[...]).astype(BF16)
    k_ref[0] = _dot(m, wk_ref[...]).astype(BF16)
    v_ref[0] = _dot(m, wv_ref[...]).astype(BF16)


def _memkv(mem, g, wk, wv):
    bsz = mem.shape[0]
    full = lambda shape: pl.BlockSpec(shape, lambda b: (0,) * len(shape))
    per_b = pl.BlockSpec((1, MEM_LEN, D_MODEL), lambda b: (b, 0, 0))
    return pl.pallas_call(
        _memkv_kernel,
        grid=(bsz,),
        in_specs=[per_b, full((1, D_MODEL)), full((D_MODEL, D_MODEL)), full((D_MODEL, D_MODEL))],
        out_specs=[per_b, per_b],
        out_shape=[jax.ShapeDtypeStruct((bsz, MEM_LEN, D_MODEL), BF16)] * 2,
        compiler_params=pltpu.CompilerParams(dimension_semantics=("arbitrary",), vmem_limit_bytes=VMEM_LIMIT),
        name="memkv",
    )(mem, g, wk, wv)


def _inproj_kernel(x_ref, g_ref, win_ref, cs_ref, b_ref, cv_ref, a_ref, s_ref):
    h = _rms(x_ref[...], g_ref[...]).astype(BF16)
    z = _dot(h, win_ref[...])
    b_ref[...] = z[:, :CONV_WIDTH].astype(BF16)
    cv_ref[...] = (z[:, CONV_WIDTH:2 * CONV_WIDTH] * z[:, 2 * CONV_WIDTH:3 * CONV_WIDTH]).astype(BF16)
    u = z[:, 3 * CONV_WIDTH:].astype(BF16)
    ab = _dot(u, cs_ref[...])
    a_ref[...] = ab[:, :FFT_WIDTH].astype(BF16)
    s_ref[...] = ab[:, FFT_WIDTH:].astype(BF16)


def _inproj(x2d, g, w_in, cs):
    t = x2d.shape[0]
    tile = pl.BlockSpec((INPROJ_TILE, D_MODEL), lambda i: (i, 0))
    half = pl.BlockSpec((INPROJ_TILE, CONV_WIDTH), lambda i: (i, 0))
    full = lambda shape: pl.BlockSpec(shape, lambda i: (0,) * len(shape))
    return pl.pallas_call(
        _inproj_kernel,
        grid=(t // INPROJ_TILE,),
        in_specs=[tile, full((1, D_MODEL)), full((D_MODEL, IN_PROJ_WIDTH)), full((FFT_WIDTH, 2 * FFT_WIDTH))],
        out_specs=[half] * 4,
        out_shape=[jax.ShapeDtypeStruct((t, CONV_WIDTH), BF16)] * 4,
        compiler_params=pltpu.CompilerParams(dimension_semantics=("parallel",), vmem_limit_bytes=VMEM_LIMIT),
        name="inproj",
    )(x2d, g, w_in, cs)


def _fft1_kernel(a_ref, s_ref, m1_ref, g_ref):
    x = jnp.concatenate([a_ref[0], s_ref[0]], axis=0)
    g_ref[0] = _dot(m1_ref[...], x).astype(BF16)


def _fft1(a3, s3, m1):
    bsz, _, cols = a3.shape
    blk = pl.BlockSpec((1, FFT_N1, FFT1_LANES), lambda b, j: (b, 0, j))
    return pl.pallas_call(
        _fft1_kernel,
        grid=(bsz, cols // FFT1_LANES),
        in_specs=[blk, blk, pl.BlockSpec((2 * FFT_N1, 2 * FFT_N1), lambda b, j: (0, 0))],
        out_specs=pl.BlockSpec((1, 2 * FFT_N1, FFT1_LANES), lambda b, j: (b, 0, j)),
        out_shape=jax.ShapeDtypeStruct((bsz, 2 * FFT_N1, cols), BF16),
        compiler_params=pltpu.CompilerParams(dimension_semantics=("parallel", "parallel"),
                                             vmem_limit_bytes=VMEM_LIMIT),
        name="fft1",
    )(a3, s3, m1)


def _fft2_kernel(g_ref, m2_ref, y_ref):
    for j in range(FFT2_K1):
        x = jnp.concatenate([g_ref[0, 0, j], g_ref[0, 1, j]], axis=0)
        y_ref[0, :, j * FFT_WIDTH:(j + 1) * FFT_WIDTH] = _dot(m2_ref[j], x).astype(BF16)


def _fft2(g5, m2):
    bsz = g5.shape[0]
    return pl.pallas_call(
        _fft2_kernel,
        grid=(bsz, FFT_N1 // FFT2_K1),
        in_specs=[pl.BlockSpec((1, 2, FFT2_K1, FFT_N2, FFT_WIDTH), lambda b, j: (b, 0, j, 0, 0)),
                  pl.BlockSpec((FFT2_K1, FFT_N2, 2 * FFT_N2), lambda b, j: (j, 0, 0))],
        out_specs=pl.BlockSpec((1, FFT_N2, FFT2_K1 * FFT_WIDTH), lambda b, j: (b, 0, j)),
        out_shape=jax.ShapeDtypeStruct((bsz, FFT_N2, FFT_N1 * FFT_WIDTH), BF16),
        compiler_params=pltpu.CompilerParams(dimension_semantics=("parallel", "parallel"),
                                             vmem_limit_bytes=VMEM_LIMIT),
        name="fft2",
    )(g5, m2)


def _post_kernel(x_ref, bg_ref, cv_ref, cvp_ref, cvn_ref, yf_ref, convw_ref, gc_ref, gf_ref,
                 wot_ref, wob_ref, nx_ref, wq_ref, k_ref, v_ref, wo_ref, nf_ref,
                 wrh_ref, wrl_ref, br_ref, tri_ref, tri32_ref,
                 x2_ref, h3_ref, w_ref, q_ref, cnt_ref):
    for sub in range(POST_SUB):
        _post_tile(sub, x_ref, bg_ref, cv_ref, cvp_ref, cvn_ref, yf_ref, convw_ref, gc_ref, gf_ref,
                   wot_ref, wob_ref, nx_ref, wq_ref, k_ref, v_ref, wo_ref, nf_ref,
                   wrh_ref, wrl_ref, br_ref, tri_ref, tri32_ref,
                   x2_ref, h3_ref, w_ref, q_ref, cnt_ref)


def _post_tile(sub, x_ref, bg_ref, cv_ref, cvp_ref, cvn_ref, yf_ref, convw_ref, gc_ref, gf_ref,
               wot_ref, wob_ref, nx_ref, wq_ref, k_ref, v_ref, wo_ref, nf_ref,
               wrh_ref, wrl_ref, br_ref, tri_ref, tri32_ref,
               x2_ref, h3_ref, w_ref, q_ref, cnt_ref):
    i = pl.program_id(1)
    last = pl.num_programs(1) - 1
    r0 = sub * TOK_TILE
    tile = slice(r0, r0 + TOK_TILE)

    cv = cv_ref[0, tile].astype(F32)
    if sub == 0:
        prev_row = jnp.where(i > 0, cvp_ref[0].astype(F32)[BF16_SUBLANES - 1:BF16_SUBLANES, :], 0.0)
    else:
        prev_row = cv_ref[0, r0 - BF16_SUBLANES:r0].astype(F32)[BF16_SUBLANES - 1:BF16_SUBLANES, :]
    if sub == POST_SUB - 1:
        next_row = jnp.where(i < last, cvn_ref[0].astype(F32)[0:1, :], 0.0)
    else:
        next_row = cv_ref[0, r0 + TOK_TILE:r0 + TOK_TILE + BF16_SUBLANES].astype(F32)[0:1, :]
    rows = lax.broadcasted_iota(I32, cv.shape, 0)
    cvm1 = jnp.where(rows == 0, prev_row, pltpu.roll(cv, 1, axis=0))
    cvp1 = jnp.where(rows == TOK_TILE - 1, next_row, pltpu.roll(cv, TOK_TILE - 1, axis=0))
    cw = convw_ref[...]
    y_conv = bg_ref[0, tile].astype(F32) * (cw[0:1] * cvm1 + cw[1:2] * cv + cw[2:3] * cvp1)
    yc_n = _rms(y_conv, gc_ref[...]).astype(BF16)
    yf_n = _rms(yf_ref[0, tile].astype(F32), gf_ref[...]).astype(BF16)
    x1 = x_ref[0, tile] + _dot(yc_n, wot_ref[...]) + _dot(yf_n, wob_ref[...])

    h2 = _rms(x1, nx_ref[...]).astype(BF16)
    q = _dot(h2, wq_ref[...]).astype(BF16)
    heads = []
    for hd in range(XATTN_HEADS):
        sl = slice(hd * XATTN_HEAD_DIM, (hd + 1) * XATTN_HEAD_DIM)
        s = _dot_nt(q[:, sl], k_ref[0, :, sl]) * (XATTN_HEAD_DIM ** -0.5)
        e = jnp.exp(s - jnp.max(s, axis=-1, keepdims=True))
        p = e * (1.0 / jnp.sum(e, axis=-1, keepdims=True))
        heads.append(_dot(p.astype(BF16), v_ref[0, :, sl]).astype(BF16))
    x2 = x1 + _dot(jnp.concatenate(heads, axis=-1), wo_ref[...])
    x2_ref[0, tile] = x2

    h3 = _rms(x2, nf_ref[...])
    _store_token_tiles(h3_ref, h3, TOK_TILE, base=r0 * TILE_SUBLANES)
    h3h = h3.astype(BF16)
    h3l = (h3 - h3h.astype(F32)).astype(BF16)
    logits = (_dot_nt(wrh_ref[...], h3h) + _dot_nt(wrh_ref[...], h3l) + _dot_nt(wrl_ref[...], h3h)
              + br_ref[:, 0:1])
    eidx = lax.broadcasted_iota(I32, logits.shape, 0)
    work = logits
    vals, idxs = [], []
    for _ in range(TOP_K):
        m = jnp.max(work, axis=0, keepdims=True)
        ik = jnp.min(jnp.where(work == m, eidx, N_EXPERTS), axis=0, keepdims=True)
        vals.append(m)
        idxs.append(ik)
        work = jnp.where(eidx == ik, -jnp.inf, work)
    ex = [jnp.exp(v - vals[0]) for v in vals]
    inv_den = 1.0 / (ex[0] + ex[1] + ex[2] + ex[3])
    w_ref[sub] = jnp.concatenate([e * inv_den for e in ex], axis=0)

    sel = jnp.zeros(logits.shape, F32)
    for ik in idxs:
        sel = sel + jnp.where(eidx == ik, 1.0, 0.0)
    cnt = jnp.broadcast_to(jnp.sum(sel, axis=1, keepdims=True), (N_EXPERTS, LANES))
    cnt_ref[sub] = cnt
    seg_rows = jnp.floor((cnt + (OCT_ROWS - 1)) * (1.0 / OCT_ROWS)) * OCT_ROWS
    seg_start = _dot(tri32_ref[...], seg_rows.astype(BF16))
    pos = seg_start[:, 0:1] + _dot(sel.astype(BF16), tri_ref[...])
    q_ref[sub] = jnp.concatenate(
        [jnp.sum(jnp.where(eidx == ik, pos, 0.0), axis=0, keepdims=True) for ik in idxs],
        axis=0).astype(I32) * TILE_SUBLANES


def _post(x3, bg, cv, yf, conv_w, gc, gf, wo_top, wo_bot, nx, wq, kmem, vmem_, wo, nf, wrh, wrl, br, tri, tri32):
    bsz, seq, _ = x3.shape
    step_rows = POST_SUB * TOK_TILE
    nt = seq // step_rows
    t = bsz * seq
    n_tiles = t // TOK_TILE
    halo_per_tile = step_rows // BF16_SUBLANES
    n_halo = seq // BF16_SUBLANES
    full = lambda shape: pl.BlockSpec(shape, lambda b, i: (0,) * len(shape))
    tile_d = pl.BlockSpec((1, step_rows, D_MODEL), lambda b, i: (b, i, 0))
    tile_h = pl.BlockSpec((1, step_rows, CONV_WIDTH), lambda b, i: (b, i, 0))
    halo_prev = pl.BlockSpec((1, BF16_SUBLANES, CONV_WIDTH),
                             lambda b, i: (b, jnp.maximum(i * halo_per_tile - 1, 0), 0))
    halo_next = pl.BlockSpec((1, BF16_SUBLANES, CONV_WIDTH),
                             lambda b, i: (b, jnp.minimum((i + 1) * halo_per_tile, n_halo - 1), 0))
    mem_blk = pl.BlockSpec((1, MEM_LEN, D_MODEL), lambda b, i: (b, 0, 0))
    tok4 = pl.BlockSpec((POST_SUB, TOP_K, TOK_TILE), lambda b, i: (b * nt + i, 0, 0))
    per_tile = pl.BlockSpec((POST_SUB, N_EXPERTS, LANES), lambda b, i: (b * nt + i, 0, 0))
    tok_tiles = pl.BlockSpec((step_rows * TILE_SUBLANES, LANES), lambda b, i: (b * nt + i, 0))
    return pl.pallas_call(
        _post_kernel,
        grid=(bsz, nt),
        in_specs=[tile_d, tile_h, tile_h, halo_prev, halo_next, tile_h,
                  full((3, CONV_WIDTH)), full((1, CONV_WIDTH)), full((1, FFT_WIDTH)),
                  full((CONV_WIDTH, D_MODEL)), full((FFT_WIDTH, D_MODEL)), full((1, D_MODEL)),
                  full((D_MODEL, D_MODEL)), mem_blk, mem_blk, full((D_MODEL, D_MODEL)), full((1, D_MODEL)),
                  full((N_EXPERTS, D_MODEL)), full((N_EXPERTS, D_MODEL)), full((N_EXPERTS, 128)),
                  full((TOK_TILE, TOK_TILE)), full((N_EXPERTS, N_EXPERTS))],
        out_specs=[tile_d, tok_tiles, tok4, tok4, per_tile],
        out_shape=[jax.ShapeDtypeStruct((bsz, seq, D_MODEL), F32),
                   jax.ShapeDtypeStruct((t * TILE_SUBLANES, LANES), F32),
                   jax.ShapeDtypeStruct((n_tiles, TOP_K, TOK_TILE), F32),
                   jax.ShapeDtypeStruct((n_tiles, TOP_K, TOK_TILE), I32),
                   jax.ShapeDtypeStruct((n_tiles, N_EXPERTS, LANES), F32)],
        compiler_params=pltpu.CompilerParams(dimension_semantics=("parallel", "parallel"),
                                             vmem_limit_bytes=VMEM_LIMIT),
        name="post",
    )(x3, bg, cv, cv, cv, yf, conv_w, gc, gf, wo_top, wo_bot, nx, wq, kmem, vmem_, wo, nf, wrh, wrl, br, tri, tri32)


def _plan_kernel(cnt_ref, src_ref, dst_ref, be_ref, nu_ref, wnext_ref, wpar_ref, off_ref, nxt_ref, *, n_tiles):
    n_sched = be_ref.shape[0]
    dump0 = n_tiles * TILE_OCT

    def fill_pad(lo, hi):
        def body(p, c):
            o = p & (OCT_PER_BLOCK - 1)
            parity = lax.shift_right_logical(p, OCT_PER_BLOCK.bit_length() - 1) & 1
            src_ref[p] = dump0 + 2 * OCT_PER_BLOCK + o
            dst_ref[p] = dump0 + parity * OCT_PER_BLOCK + o
            return c
        lax.fori_loop(lo, hi, body, 0)

    def clear(tile, c):
        off_ref[tile] = 0
        return c

    lax.fori_loop(0, n_tiles, clear, 0)

    def find_next(i, nxt):
        e = N_EXPERTS - 1 - i
        nxt_ref[e] = nxt
        total = lax.fori_loop(0, n_tiles, lambda tile, s: s + cnt_ref[tile * N_EXPERTS + e], jnp.int32(0))
        return jnp.where(total > 0, e, nxt)

    lax.fori_loop(0, N_EXPERTS, find_next, jnp.int32(-1))
    fill_pad(0, OCT_PER_BLOCK)

    def per_expert(e, carry):
        pos0, blk0, group = carry

        def per_tile(tile, pos):
            n_oct = lax.shift_right_logical(cnt_ref[tile * N_EXPERTS + e] + (OCT_ROWS - 1), OCT_ROWS.bit_length() - 1)
            base = tile * TILE_OCT + off_ref[tile]
            off_ref[tile] = off_ref[tile] + n_oct

            def per_group(grp, c):
                for u in range(PLAN_UNROLL):
                    o = grp * PLAN_UNROLL + u
                    src_ref[pos + o] = base + o
                    dst_ref[pos + o] = base + o
                return c

            per_group(0, 0)
            per_group(1, 0)
            lax.fori_loop(2, lax.shift_right_logical(n_oct + (PLAN_UNROLL - 1), PLAN_UNROLL.bit_length() - 1),
                          per_group, 0)
            return pos + n_oct

        pos1 = lax.fori_loop(0, n_tiles, per_tile, pos0)
        nb = lax.shift_right_logical(pos1 - pos0 + (OCT_PER_BLOCK - 1), OCT_PER_BLOCK.bit_length() - 1)
        pos2 = pos0 + nb * OCT_PER_BLOCK
        fill_pad(pos1, pos2)

        def fill(j, c):
            be_ref[blk0 + j] = e
            wnext_ref[blk0 + j] = nxt_ref[e]
            wpar_ref[blk0 + j] = group & 1
            return c

        lax.fori_loop(0, nb, fill, 0)
        return pos2, blk0 + nb, group + jnp.where(nb > 0, 1, 0)

    pos, n_used, _ = lax.fori_loop(0, N_EXPERTS, per_expert,
                                   (jnp.int32(OCT_PER_BLOCK), jnp.int32(0), jnp.int32(0)))
    nu_ref[0] = n_used
    tail_e = be_ref[n_used - 1]

    def tail(j, c):
        be_ref[j] = tail_e
        wnext_ref[j] = -1
        wpar_ref[j] = 0
        return c

    lax.fori_loop(n_used, n_sched, tail, 0)
    fill_pad(pos, src_ref.shape[0])


def _plan(counts, n_tiles, n_blocks):
    smem = pl.BlockSpec(memory_space=pltpu.SMEM)
    grid_spec = pltpu.PrefetchScalarGridSpec(
        num_scalar_prefetch=1,
        grid=(1,),
        in_specs=[],
        out_specs=[smem] * 6,
        scratch_shapes=[pltpu.SMEM((n_tiles,), I32), pltpu.SMEM((N_EXPERTS,), I32)],
    )
    octs = jax.ShapeDtypeStruct(((n_blocks + 2) * OCT_PER_BLOCK,), I32)
    sched = jax.ShapeDtypeStruct((n_blocks + 1,), I32)
    return pl.pallas_call(
        functools.partial(_plan_kernel, n_tiles=n_tiles),
        grid_spec=grid_spec,
        out_shape=[octs, octs, sched, jax.ShapeDtypeStruct((1,), I32), sched, sched],
        compiler_params=pltpu.CompilerParams(dimension_semantics=("arbitrary",), vmem_limit_bytes=VMEM_LIMIT),
        name="plan",
    )(counts)


def _localsort_kernel(q_ref, h3_ref, xs_ref):
    xs_ref[...] = jnp.zeros_like(xs_ref)

    @pl.when(pl.program_id(0) < pl.num_programs(0) - 1)
    def _():
        def group(g, c):
            for u in range(TILE_SUBLANES):
                t = g * TILE_SUBLANES + u
                row = _tile_rows(h3_ref, t)[...]
                for k in range(TOP_K):
                    _tile_at(xs_ref, q_ref[0, t * TOP_K + k])[...] = row
            return c

        lax.fori_loop(0, TOK_TILE // TILE_SUBLANES, group, 0)


def _localsort(q_tiles, h3t):
    n_tiles = q_tiles.shape[0]
    last = n_tiles - 1
    region = TILE_OCT * OCT_ROWS * TILE_SUBLANES
    return pl.pallas_call(
        _localsort_kernel,
        grid=(n_tiles + 1,),
        in_specs=[pl.BlockSpec((None, 1, TOP_K * TOK_TILE), lambda i: (jnp.minimum(i, last), 0, 0),
                               memory_space=pltpu.SMEM),
                  pl.BlockSpec((TOK_TILE * TILE_SUBLANES, LANES), lambda i: (jnp.minimum(i, last), 0))],
        out_specs=pl.BlockSpec((region, LANES), lambda i: (i, 0)),
        out_shape=jax.ShapeDtypeStruct(((n_tiles + 1) * region, LANES), F32),
        compiler_params=pltpu.CompilerParams(dimension_semantics=("parallel",), vmem_limit_bytes=VMEM_LIMIT),
        name="localsort",
    )(q_tiles, h3t)


def _tile_at(ref, sublane_row):
    return ref.at[pl.ds(pl.multiple_of(sublane_row, TILE_SUBLANES), TILE_SUBLANES), :]


def _tile_rows(ref, row, n_rows=1):
    return ref.at[pl.ds(pl.multiple_of(row * TILE_SUBLANES, TILE_SUBLANES), n_rows * TILE_SUBLANES), :]


def _expert_kernel(be_ref, nu_ref, wnext_ref, wpar_ref, src_ref, dst_ref,
                   bgu_ref, bd_ref, xs_ref, wgu_hbm, wd_hbm,
                   y_ref,
                   xbuf0, xbuf1, obuf0, obuf1, wgu_f32, wd_f32, wgu_bf, wd_bf, gsem, ssem, wsem):
    j = pl.program_id(0)
    nu = nu_ref[0]
    xbuf = (xbuf0, xbuf1)
    obuf = (obuf0, obuf1)
    octet = lambda ref, o: _tile_rows(ref, o * OCT_ROWS, OCT_ROWS)

    def gather_octet(block, o, dst):
        return pltpu.make_async_copy(octet(xs_ref, src_ref[(block + 1) * OCT_PER_BLOCK + o]), octet(dst, o), gsem)

    def scatter_octet(block, o, src):
        return pltpu.make_async_copy(octet(src, o), octet(y_ref, dst_ref[(block + 1) * OCT_PER_BLOCK + o]), ssem)

    def weight_copies(e, p):
        return (pltpu.make_async_copy(wgu_hbm.at[e], wgu_f32.at[p], wsem.at[p, 0]),
                pltpu.make_async_copy(wd_hbm.at[e], wd_f32.at[p], wsem.at[p, 1]))

    def wait_gather(dst):
        for _ in range(OCT_PER_BLOCK):
            pltpu.make_async_copy(octet(xs_ref, 0), octet(dst, 0), gsem).wait()

    def wait_scatter(src):
        for _ in range(OCT_PER_BLOCK):
            pltpu.make_async_copy(octet(src, 0), octet(y_ref, 0), ssem).wait()

    @pl.when(j == 0)
    def _():
        obuf1[...] = jnp.zeros_like(obuf1)
        for c in weight_copies(be_ref[0], 0):
            c.start(priority=WEIGHT_DMA_PRIORITY)
        for o in range(OCT_PER_BLOCK):
            gather_octet(0, o, xbuf0).start(priority=GATHER_DMA_PRIORITY)

    first_of_group = jnp.logical_or(j == 0, be_ref[j] != be_ref[jnp.maximum(j - 1, 0)])

    @pl.when(jnp.logical_and(j < nu, first_of_group))
    def _():
        p = wpar_ref[j]
        for c in weight_copies(be_ref[j], p):
            c.wait()
        wgu_bf[...] = wgu_f32[p].astype(BF16)
        wd_bf[...] = wd_f32[p].astype(BF16)

        @pl.when(wnext_ref[j] >= 0)
        def _():
            for c in weight_copies(wnext_ref[j], 1 - p):
                c.start(priority=WEIGHT_DMA_PRIORITY)

    def step(s, compute):
        wait_gather(xbuf[s])

        @pl.when(j >= 1)
        def _():
            wait_scatter(obuf[s])

        def start_scatter():
            for o in range(OCT_PER_BLOCK):
                scatter_octet(j - 1, o, obuf[1 - s]).start(priority=SCATTER_DMA_PRIORITY)

        if compute:
            for o in range(OCT_PER_BLOCK):
                gather_octet(j + 1, o, xbuf[1 - s]).start(priority=GATHER_DMA_PRIORITY)
            x = _load_token_tiles(xbuf[s], ROW_BLOCK).astype(BF16)
            gu = _dot(x, wgu_bf[...]) + bgu_ref[pl.ds(be_ref[j], 1), :]
            gate = jnp.minimum(gu[:, :D_EXPERT], SWIGLU_LIMIT)
            up = jnp.clip(gu[:, D_EXPERT:], -SWIGLU_LIMIT, SWIGLU_LIMIT)
            glu = gate * (1.0 / (1.0 + jnp.exp(-SWIGLU_ALPHA * gate)))
            h = ((up + 1.0) * glu).astype(BF16)
            start_scatter()
            _store_token_tiles(obuf[s], _dot(h, wd_bf[...]) + bd_ref[pl.ds(be_ref[j], 1), :], ROW_BLOCK)
        else:
            start_scatter()
            wait_scatter(obuf[1 - s])

    for s in range(2):
        parity = (j & 1) == s
        pl.when(jnp.logical_and(j < nu, parity))(functools.partial(step, s, True))
        pl.when(jnp.logical_and(j == nu, parity))(functools.partial(step, s, False))


def _experts(block_e, n_used, w_next, w_par, src, dst, xs, wgu, bgu, wd, bd):
    n_steps = block_e.shape[0]
    blk_rows = ROW_BLOCK * TILE_SUBLANES
    resident = lambda shape: pl.BlockSpec(shape, lambda j, *prefetch: (0,) * len(shape))
    hbm = pl.BlockSpec(memory_space=pl.ANY)
    grid_spec = pltpu.PrefetchScalarGridSpec(
        num_scalar_prefetch=6,
        grid=(n_steps,),
        in_specs=[resident(bgu.shape), resident(bd.shape), hbm, hbm, hbm],
        out_specs=hbm,
        scratch_shapes=[pltpu.VMEM((blk_rows, LANES), F32)] * 4
        + [pltpu.VMEM((2, D_MODEL, 2 * D_EXPERT), F32), pltpu.VMEM((2, D_EXPERT, D_MODEL), F32),
           pltpu.VMEM((D_MODEL, 2 * D_EXPERT), BF16), pltpu.VMEM((D_EXPERT, D_MODEL), BF16),
           pltpu.SemaphoreType.DMA, pltpu.SemaphoreType.DMA, pltpu.SemaphoreType.DMA((2, 2))],
    )
    return pl.pallas_call(
        _expert_kernel,
        grid_spec=grid_spec,
        out_shape=jax.ShapeDtypeStruct(xs.shape, F32),
        input_output_aliases={8: 0},
        compiler_params=pltpu.CompilerParams(dimension_semantics=("arbitrary",), vmem_limit_bytes=VMEM_LIMIT),
        name="experts",
    )(block_e, n_used, w_next, w_par, src, dst, bgu, bd, xs, wgu, wd)


def _combine_kernel(q_ref, w_ref, y_ref, x2_ref, g_ref, out_ref, acc_ref):
    def group(grp, c):
        for u in range(TILE_SUBLANES):
            t = grp * TILE_SUBLANES + u
            acc = w_ref[0, t * TOP_K] * _tile_at(y_ref, q_ref[0, t * TOP_K])[...]
            for k in range(1, TOP_K):
                acc = acc + w_ref[0, t * TOP_K + k] * _tile_at(y_ref, q_ref[0, t * TOP_K + k])[...]
            _tile_rows(acc_ref, t)[...] = acc
        return c

    lax.fori_loop(0, TOK_TILE // TILE_SUBLANES, group, 0)
    out_ref[...] = _rms(x2_ref[...] + _load_token_tiles(acc_ref, TOK_TILE), g_ref[...])


def _combine(q_tiles, w_tiles, ybuf, x2, g):
    n_tiles = q_tiles.shape[0]
    region = TILE_OCT * OCT_ROWS * TILE_SUBLANES
    tok4 = pl.BlockSpec((None, 1, TOP_K * TOK_TILE), lambda i: (i, 0, 0), memory_space=pltpu.SMEM)
    return pl.pallas_call(
        _combine_kernel,
        grid=(n_tiles,),
        in_specs=[tok4, tok4,
                  pl.BlockSpec((region, LANES), lambda i: (i, 0)),
                  pl.BlockSpec((TOK_TILE, D_MODEL), lambda i: (i, 0)),
                  pl.BlockSpec((1, D_MODEL), lambda i: (0, 0))],
        out_specs=pl.BlockSpec((TOK_TILE, D_MODEL), lambda i: (i, 0)),
        out_shape=jax.ShapeDtypeStruct(x2.shape, F32),
        scratch_shapes=[pltpu.VMEM((TOK_TILE * TILE_SUBLANES, LANES), F32)],
        compiler_params=pltpu.CompilerParams(dimension_semantics=("parallel",), vmem_limit_bytes=VMEM_LIMIT),
        name="combine",
    )(q_tiles, w_tiles, ybuf, x2, g)


def _layer(x, mem, norm_mix, w_in, conv_w, g_conv_out, g_fft_out, w_out, norm_xattn, norm_mem,
           w_q, w_k, w_v, w_o, norm_ffn, w_router, b_router, w_gate_up, b_gate_up, w_down, b_down, tables):
    bsz, seq, d = x.shape
    t = bsz * seq
    cs, m1, m2 = tables
    row = lambda v: v.reshape(1, -1)

    kmem, vmem_ = _memkv(mem, row(norm_mem), w_k.astype(BF16), w_v.astype(BF16))

    bg, cv, a, s = _inproj(x.reshape(t, d), row(norm_mix), w_in.astype(BF16), cs)
    cols = FFT_N2 * FFT_WIDTH
    g = _fft1(a.reshape(bsz, FFT_N1, cols), s.reshape(bsz, FFT_N1, cols), m1)
    yf = _fft2(g.reshape(bsz, 2, FFT_N1, FFT_N2, FFT_WIDTH), m2).reshape(bsz, seq, FFT_WIDTH)

    w_out_b = w_out.astype(BF16)
    wr_t = w_router.T
    wr_hi = wr_t.astype(BF16)
    wr_lo = (wr_t - wr_hi.astype(F32)).astype(BF16)
    tri = (jnp.arange(TOK_TILE)[:, None] < jnp.arange(TOK_TILE)[None, :]).astype(BF16)
    tri32 = (jnp.arange(N_EXPERTS)[None, :] < jnp.arange(N_EXPERTS)[:, None]).astype(BF16)
    x2, h3, w_tiles, q_tiles, cnt = _post(
        x, bg.reshape(bsz, seq, CONV_WIDTH), cv.reshape(bsz, seq, CONV_WIDTH), yf,
        conv_w, row(g_conv_out), row(g_fft_out), w_out_b[:CONV_WIDTH], w_out_b[CONV_WIDTH:],
        row(norm_xattn), w_q.astype(BF16), kmem, vmem_, w_o.astype(BF16), row(norm_ffn),
        wr_hi, wr_lo, jnp.broadcast_to(b_router[:, None], (N_EXPERTS, 128)), tri, tri32)

    n_tiles = t // TOK_TILE
    n_blocks = (t * TOP_K + n_tiles * N_EXPERTS * (OCT_ROWS - 1)) // ROW_BLOCK + N_EXPERTS
    counts = cnt[:, :, 0].astype(I32).reshape(-1)
    src, dst, block_e, n_used, w_next, w_par = _plan(counts, n_tiles, n_blocks)
    by_token = lambda v: v.transpose(0, 2, 1).reshape(n_tiles, 1, TOK_TILE * TOP_K)
    q_tiles, w_tiles = by_token(q_tiles), by_token(w_tiles)
    xs = _localsort(q_tiles, h3)
    ybuf = _experts(block_e, n_used, w_next, w_par, src, dst, xs, w_gate_up, b_gate_up, w_down, b_down)
    return x2.reshape(t, d), ybuf, q_tiles, w_tiles


def kernel(x, mem, norm_mix, w_in, conv_w, g_conv_out, g_fft_out, w_out, norm_xattn, norm_mem, w_q, w_k, w_v, w_o,
           norm_ffn, w_router, b_router, w_gate_up, b_gate_up, w_down, b_down, norm_final):
    bsz, seq, d = x.shape
    depth = norm_mix.shape[0]
    assert depth == 1, "final norm is fused into the combine step of the single layer"
    tables = _dft_tables(seq)
    x2, ybuf, q_tiles, w_tiles = _layer(
        x, mem, norm_mix[0], w_in[0], conv_w[0], g_conv_out[0], g_fft_out[0], w_out[0], norm_xattn[0],
        norm_mem[0], w_q[0], w_k[0], w_v[0], w_o[0], norm_ffn[0], w_router[0], b_router[0],
        w_gate_up[0], b_gate_up[0], w_down[0], b_down[0], tables)
    out = _combine(q_tiles, w_tiles, ybuf, x2, norm_final.reshape(1, -1))
    return out.reshape(bsz, seq, d)
```

```python
import functools
import math

import numpy as np
import jax
import jax.numpy as jnp
from jax import lax
from jax.experimental import pallas as pl
from jax.experimental.pallas import tpu as pltpu

F32 = jnp.float32
BF16 = jnp.bfloat16
I32 = jnp.int32

D_MODEL = 1024
CONV_WIDTH = 512
FFT_WIDTH = 512
GROUP_DIM = 64
IN_PROJ_WIDTH = 3 * CONV_WIDTH + FFT_WIDTH
MEM_LEN = 256
XATTN_HEADS = 4
XATTN_HEAD_DIM = D_MODEL // XATTN_HEADS
N_EXPERTS = 32
TOP_K = 4
D_EXPERT = D_MODEL
SWIGLU_LIMIT = 7.0
SWIGLU_ALPHA = 1.702
EPS = 1e-5

FFT_N1 = 64
FFT_N2 = 128

TOK_TILE = 512
POST_SUB = 2
INPROJ_TILE = 1024
ROW_BLOCK = 512
GATHER_DMA_PRIORITY = 0
SCATTER_DMA_PRIORITY = 1
WEIGHT_DMA_PRIORITY = 1
OCT_ROWS = 8
OCT_PER_BLOCK = ROW_BLOCK // OCT_ROWS
PLAN_UNROLL = 8
TILE_OCT = (TOK_TILE * TOP_K) // OCT_ROWS + N_EXPERTS
FFT1_LANES = 16384
FFT2_K1 = 16
BF16_SUBLANES = 16
TILE_SUBLANES = 8
LANES = 128
assert D_MODEL == TILE_SUBLANES * LANES
VMEM_LIMIT = 56 * 1024 * 1024


def _rms(x, g):
    return x * lax.rsqrt(jnp.mean(x * x, axis=-1, keepdims=True) + EPS) * g


def _dot(a, b):
    return jnp.dot(a, b, preferred_element_type=F32)


def _dot_nt(a, b):
    return lax.dot_general(a, b, (((1,), (1,)), ((), ())), preferred_element_type=F32)


def _load_token_tiles(ref, rows, base=0):
    return jnp.concatenate(
        [ref[pl.ds(base + s, rows, stride=TILE_SUBLANES), :] for s in range(TILE_SUBLANES)], axis=-1)


def _store_token_tiles(ref, val, rows, base=0):
    for s in range(TILE_SUBLANES):
        ref[pl.ds(base + s, rows, stride=TILE_SUBLANES), :] = val[:, s * LANES:(s + 1) * LANES]


def _dft_tables(seq):
    assert seq == FFT_N1 * FFT_N2
    c = np.arange(GROUP_DIM)
    ang = 2.0 * np.pi * ((c[:, None] * c[None, :]) % GROUP_DIM) / GROUP_DIM
    groups = FFT_WIDTH // GROUP_DIM
    eye = np.eye(groups)
    cs = np.concatenate([np.kron(eye, np.cos(ang)), np.kron(eye, np.sin(ang))], axis=1) / math.sqrt(GROUP_DIM)
    n1 = np.arange(FFT_N1)
    a1 = 2.0 * np.pi * ((n1[:, None] * n1[None, :]) % FFT_N1) / FFT_N1
    c1, s1 = np.cos(a1), np.sin(a1)
    m1 = np.block([[c1, -s1], [s1, c1]]) / math.sqrt(FFT_N1)
    k1 = np.arange(FFT_N1)[:, None, None]
    k2 = np.arange(FFT_N2)[None, :, None]
    n2 = np.arange(FFT_N2)[None, None, :]
    a2 = 2.0 * np.pi * ((n2 * (k1 + FFT_N1 * k2)) % seq) / seq
    m2 = np.concatenate([np.cos(a2), -np.sin(a2)], axis=2) / math.sqrt(FFT_N2)
    return (jnp.asarray(cs, F32).astype(BF16), jnp.asarray(m1, F32).astype(BF16),
            jnp.asarray(m2, F32).astype(BF16))


def _memkv_kernel(mem_ref, g_ref, wk_ref, wv_ref, k_ref, v_ref):
    m = _rms(mem_ref[0], g_ref[...]).astype(BF16)
    k_ref[0] = _dot(m, wk_ref[...]).astype(BF16)
    v_ref[0] = _dot(m, wv_ref[...]).astype(BF16)


def _memkv(mem, g, wk, wv):
    bsz = mem.shape[0]
    full = lambda shape: pl.BlockSpec(shape, lambda b: (0,) * len(shape))
    per_b = pl.BlockSpec((1, MEM_LEN, D_MODEL), lambda b: (b, 0, 0))
    return pl.pallas_call(
        _memkv_kernel,
        grid=(bsz,),
        in_specs=[per_b, full((1, D_MODEL)), full((D_MODEL, D_MODEL)), full((D_MODEL, D_MODEL))],
        out_specs=[per_b, per_b],
        out_shape=[jax.ShapeDtypeStruct((bsz, MEM_LEN, D_MODEL), BF16)] * 2,
        compiler_params=pltpu.CompilerParams(dimension_semantics=("arbitrary",), vmem_limit_bytes=VMEM_LIMIT),
        name="memkv",
    )(mem, g, wk, wv)


def _inproj_kernel(x_ref, g_ref, win_ref, cs_ref, b_ref, cv_ref, a_ref, s_ref):
    h = _rms(x_ref[...], g_ref[...]).astype(BF16)
    z = _dot(h, win_ref[...])
    b_ref[...] = z[:, :CONV_WIDTH].astype(BF16)
    cv_ref[...] = (z[:, CONV_WIDTH:2 * CONV_WIDTH] * z[:, 2 * CONV_WIDTH:3 * CONV_WIDTH]).astype(BF16)
    u = z[:, 3 * CONV_WIDTH:].astype(BF16)
    ab = _dot(u, cs_ref[...])
    a_ref[...] = ab[:, :FFT_WIDTH].astype(BF16)
    s_ref[...] = ab[:, FFT_WIDTH:].astype(BF16)


def _inproj(x2d, g, w_in, cs):
    t = x2d.shape[0]
    tile = pl.BlockSpec((INPROJ_TILE, D_MODEL), lambda i: (i, 0))
    half = pl.BlockSpec((INPROJ_TILE, CONV_WIDTH), lambda i: (i, 0))
    full = lambda shape: pl.BlockSpec(shape, lambda i: (0,) * len(shape))
    return pl.pallas_call(
        _inproj_kernel,
        grid=(t // INPROJ_TILE,),
        in_specs=[tile, full((1, D_MODEL)), full((D_MODEL, IN_PROJ_WIDTH)), full((FFT_WIDTH, 2 * FFT_WIDTH))],
        out_specs=[half] * 4,
        out_shape=[jax.ShapeDtypeStruct((t, CONV_WIDTH), BF16)] * 4,
        compiler_params=pltpu.CompilerParams(dimension_semantics=("parallel",), vmem_limit_bytes=VMEM_LIMIT),
        name="inproj",
    )(x2d, g, w_in, cs)


def _fft1_kernel(a_ref, s_ref, m1_ref, g_ref):
    x = jnp.concatenate([a_ref[0], s_ref[0]], axis=0)
    g_ref[0] = _dot(m1_ref[...], x).astype(BF16)


def _fft1(a3, s3, m1):
    bsz, _, cols = a3.shape
    blk = pl.BlockSpec((1, FFT_N1, FFT1_LANES), lambda b, j: (b, 0, j))
    return pl.pallas_call(
        _fft1_kernel,
        grid=(bsz, cols // FFT1_LANES),
        in_specs=[blk, blk, pl.BlockSpec((2 * FFT_N1, 2 * FFT_N1), lambda b, j: (0, 0))],
        out_specs=pl.BlockSpec((1, 2 * FFT_N1, FFT1_LANES), lambda b, j: (b, 0, j)),
        out_shape=jax.ShapeDtypeStruct((bsz, 2 * FFT_N1, cols), BF16),
        compiler_params=pltpu.CompilerParams(dimension_semantics=("parallel", "parallel"),
                                             vmem_limit_bytes=VMEM_LIMIT),
        name="fft1",
    )(a3, s3, m1)


def _fft2_kernel(g_ref, m2_ref, y_ref):
    for j in range(FFT2_K1):
        x = jnp.concatenate([g_ref[0, 0, j], g_ref[0, 1, j]], axis=0)
        y_ref[0, :, j * FFT_WIDTH:(j + 1) * FFT_WIDTH] = _dot(m2_ref[j], x).astype(BF16)


def _fft2(g5, m2):
    bsz = g5.shape[0]
    return pl.pallas_call(
        _fft2_kernel,
        grid=(bsz, FFT_N1 // FFT2_K1),
        in_specs=[pl.BlockSpec((1, 2, FFT2_K1, FFT_N2, FFT_WIDTH), lambda b, j: (b, 0, j, 0, 0)),
                  pl.BlockSpec((FFT2_K1, FFT_N2, 2 * FFT_N2), lambda b, j: (j, 0, 0))],
        out_specs=pl.BlockSpec((1, FFT_N2, FFT2_K1 * FFT_WIDTH), lambda b, j: (b, 0, j)),
        out_shape=jax.ShapeDtypeStruct((bsz, FFT_N2, FFT_N1 * FFT_WIDTH), BF16),
        compiler_params=pltpu.CompilerParams(dimension_semantics=("parallel", "parallel"),
                                             vmem_limit_bytes=VMEM_LIMIT),
        name="fft2",
    )(g5, m2)


def _post_kernel(x_ref, bg_ref, cv_ref, cvp_ref, cvn_ref, yf_ref, convw_ref, gc_ref, gf_ref,
                 wot_ref, wob_ref, nx_ref, wq_ref, k_ref, v_ref, wo_ref, nf_ref,
                 wrh_ref, wrl_ref, br_ref, tri_ref, tri32_ref,
                 x2_ref, h3_ref, w_ref, q_ref, cnt_ref):
    tiles = [_post_tile(sub, x_ref, bg_ref, cv_ref, cvp_ref, cvn_ref, yf_ref, convw_ref, gc_ref, gf_ref,
                        wot_ref, wob_ref, nx_ref, wq_ref, k_ref, v_ref, wo_ref, nf_ref,
                        wrh_ref, wrl_ref, br_ref, tri_ref, tri32_ref,
                        x2_ref, h3_ref, w_ref, q_ref, cnt_ref) for sub in range(POST_SUB)]
    running = []
    while tiles or running:
        if tiles:
            running.append(tiles.pop(0))
        for tile in list(running):
            if next(tile, "done") == "done":
                running.remove(tile)


def _post_tile(sub, x_ref, bg_ref, cv_ref, cvp_ref, cvn_ref, yf_ref, convw_ref, gc_ref, gf_ref,
               wot_ref, wob_ref, nx_ref, wq_ref, k_ref, v_ref, wo_ref, nf_ref,
               wrh_ref, wrl_ref, br_ref, tri_ref, tri32_ref,
               x2_ref, h3_ref, w_ref, q_ref, cnt_ref):
    i = pl.program_id(1)
    last = pl.num_programs(1) - 1
    r0 = sub * TOK_TILE
    tile = slice(r0, r0 + TOK_TILE)

    cv = cv_ref[0, tile].astype(F32)
    if sub == 0:
        prev_row = jnp.where(i > 0, cvp_ref[0].astype(F32)[BF16_SUBLANES - 1:BF16_SUBLANES, :], 0.0)
    else:
        prev_row = cv_ref[0, r0 - BF16_SUBLANES:r0].astype(F32)[BF16_SUBLANES - 1:BF16_SUBLANES, :]
    if sub == POST_SUB - 1:
        next_row = jnp.where(i < last, cvn_ref[0].astype(F32)[0:1, :], 0.0)
    else:
        next_row = cv_ref[0, r0 + TOK_TILE:r0 + TOK_TILE + BF16_SUBLANES].astype(F32)[0:1, :]
    rows = lax.broadcasted_iota(I32, cv.shape, 0)
    cvm1 = jnp.where(rows == 0, prev_row, pltpu.roll(cv, 1, axis=0))
    cvp1 = jnp.where(rows == TOK_TILE - 1, next_row, pltpu.roll(cv, TOK_TILE - 1, axis=0))
    cw = convw_ref[...]
    y_conv = bg_ref[0, tile].astype(F32) * (cw[0:1] * cvm1 + cw[1:2] * cv + cw[2:3] * cvp1)
    yc_n = _rms(y_conv, gc_ref[...]).astype(BF16)
    yf_n = _rms(yf_ref[0, tile].astype(F32), gf_ref[...]).astype(BF16)
    yield
    x1 = x_ref[0, tile] + _dot(yc_n, wot_ref[...]) + _dot(yf_n, wob_ref[...])
    yield

    h2 = _rms(x1, nx_ref[...]).astype(BF16)
    yield
    q = _dot(h2, wq_ref[...]).astype(BF16)
    yield
    heads = []
    for hd in range(XATTN_HEADS):
        sl = slice(hd * XATTN_HEAD_DIM, (hd + 1) * XATTN_HEAD_DIM)
        s = _dot_nt(q[:, sl], k_ref[0, :, sl]) * (XATTN_HEAD_DIM ** -0.5)
        e = jnp.exp(s - jnp.max(s, axis=-1, keepdims=True))
        p = e * (1.0 / jnp.sum(e, axis=-1, keepdims=True))
        heads.append(_dot(p.astype(BF16), v_ref[0, :, sl]).astype(BF16))
    yield
    x2 = x1 + _dot(jnp.concatenate(heads, axis=-1), wo_ref[...])
    x2_ref[0, tile] = x2
    yield

    h3 = _rms(x2, nf_ref[...])
    _store_token_tiles(h3_ref, h3, TOK_TILE, base=r0 * TILE_SUBLANES)
    h3h = h3.astype(BF16)
    h3l = (h3 - h3h.astype(F32)).astype(BF16)
    yield
    logits = (_dot_nt(wrh_ref[...], h3h) + _dot_nt(wrh_ref[...], h3l) + _dot_nt(wrl_ref[...], h3h)
              + br_ref[:, 0:1])
    yield
    eidx = lax.broadcasted_iota(I32, logits.shape, 0)
    work = logits
    vals, idxs = [], []
    for _ in range(TOP_K):
        m = jnp.max(work, axis=0, keepdims=True)
        ik = jnp.min(jnp.where(work == m, eidx, N_EXPERTS), axis=0, keepdims=True)
        vals.append(m)
        idxs.append(ik)
        work = jnp.where(eidx == ik, -jnp.inf, work)
    ex = [jnp.exp(v - vals[0]) for v in vals]
    inv_den = 1.0 / (ex[0] + ex[1] + ex[2] + ex[3])
    w_ref[sub] = jnp.concatenate([e * inv_den for e in ex], axis=0)

    sel = jnp.zeros(logits.shape, F32)
    for ik in idxs:
        sel = sel + jnp.where(eidx == ik, 1.0, 0.0)
    cnt = jnp.broadcast_to(jnp.sum(sel, axis=1, keepdims=True), (N_EXPERTS, LANES))
    cnt_ref[sub] = cnt
    seg_rows = jnp.floor((cnt + (OCT_ROWS - 1)) * (1.0 / OCT_ROWS)) * OCT_ROWS
    seg_start = _dot(tri32_ref[...], seg_rows.astype(BF16))
    pos = seg_start[:, 0:1] + _dot(sel.astype(BF16), tri_ref[...])
    q_ref[sub] = jnp.concatenate(
        [jnp.sum(jnp.where(eidx == ik, pos, 0.0), axis=0, keepdims=True) for ik in idxs],
        axis=0).astype(I32) * TILE_SUBLANES


def _post(x3, bg, cv, yf, conv_w, gc, gf, wo_top, wo_bot, nx, wq, kmem, vmem_, wo, nf, wrh, wrl, br, tri, tri32):
    bsz, seq, _ = x3.shape
    step_rows = POST_SUB * TOK_TILE
    nt = seq // step_rows
    t = bsz * seq
    n_tiles = t // TOK_TILE
    halo_per_tile = step_rows // BF16_SUBLANES
    n_halo = seq // BF16_SUBLANES
    full = lambda shape: pl.BlockSpec(shape, lambda b, i: (0,) * len(shape))
    tile_d = pl.BlockSpec((1, step_rows, D_MODEL), lambda b, i: (b, i, 0))
    tile_h = pl.BlockSpec((1, step_rows, CONV_WIDTH), lambda b, i: (b, i, 0))
    halo_prev = pl.BlockSpec((1, BF16_SUBLANES, CONV_WIDTH),
                             lambda b, i: (b, jnp.maximum(i * halo_per_tile - 1, 0), 0))
    halo_next = pl.BlockSpec((1, BF16_SUBLANES, CONV_WIDTH),
                             lambda b, i: (b, jnp.minimum((i + 1) * halo_per_tile, n_halo - 1), 0))
    mem_blk = pl.BlockSpec((1, MEM_LEN, D_MODEL), lambda b, i: (b, 0, 0))
    tok4 = pl.BlockSpec((POST_SUB, TOP_K, TOK_TILE), lambda b, i: (b * nt + i, 0, 0))
    per_tile = pl.BlockSpec((POST_SUB, N_EXPERTS, LANES), lambda b, i: (b * nt + i, 0, 0))
    tok_tiles = pl.BlockSpec((step_rows * TILE_SUBLANES, LANES), lambda b, i: (b * nt + i, 0))
    return pl.pallas_call(
        _post_kernel,
        grid=(bsz, nt),
        in_specs=[tile_d, tile_h, tile_h, halo_prev, halo_next, tile_h,
                  full((3, CONV_WIDTH)), full((1, CONV_WIDTH)), full((1, FFT_WIDTH)),
                  full((CONV_WIDTH, D_MODEL)), full((FFT_WIDTH, D_MODEL)), full((1, D_MODEL)),
                  full((D_MODEL, D_MODEL)), mem_blk, mem_blk, full((D_MODEL, D_MODEL)), full((1, D_MODEL)),
                  full((N_EXPERTS, D_MODEL)), full((N_EXPERTS, D_MODEL)), full((N_EXPERTS, 128)),
                  full((TOK_TILE, TOK_TILE)), full((N_EXPERTS, N_EXPERTS))],
        out_specs=[tile_d, tok_tiles, tok4, tok4, per_tile],
        out_shape=[jax.ShapeDtypeStruct((bsz, seq, D_MODEL), F32),
                   jax.ShapeDtypeStruct((t * TILE_SUBLANES, LANES), F32),
                   jax.ShapeDtypeStruct((n_tiles, TOP_K, TOK_TILE), F32),
                   jax.ShapeDtypeStruct((n_tiles, TOP_K, TOK_TILE), I32),
                   jax.ShapeDtypeStruct((n_tiles, N_EXPERTS, LANES), F32)],
        compiler_params=pltpu.CompilerParams(dimension_semantics=("parallel", "parallel"),
                                             vmem_limit_bytes=VMEM_LIMIT),
        name="post",
    )(x3, bg, cv, cv, cv, yf, conv_w, gc, gf, wo_top, wo_bot, nx, wq, kmem, vmem_, wo, nf, wrh, wrl, br, tri, tri32)


def _plan_kernel(cnt_ref, src_ref, dst_ref, be_ref, nu_ref, wnext_ref, wpar_ref, off_ref, nxt_ref, *, n_tiles):
    n_sched = be_ref.shape[0]
    dump0 = n_tiles * TILE_OCT

    def fill_pad(lo, hi):
        def body(p, c):
            o = p & (OCT_PER_BLOCK - 1)
            parity = lax.shift_right_logical(p, OCT_PER_BLOCK.bit_length() - 1) & 1
            src_ref[p] = dump0 + 2 * OCT_PER_BLOCK + o
            dst_ref[p] = dump0 + parity * OCT_PER_BLOCK + o
            return c
        lax.fori_loop(lo, hi, body, 0)

    def clear(tile, c):
        off_ref[tile] = 0
        return c

    lax.fori_loop(0, n_tiles, clear, 0)

    def find_next(i, nxt):
        e = N_EXPERTS - 1 - i
        nxt_ref[e] = nxt
        total = lax.fori_loop(0, n_tiles, lambda tile, s: s + cnt_ref[tile * N_EXPERTS + e], jnp.int32(0))
        return jnp.where(total > 0, e, nxt)

    lax.fori_loop(0, N_EXPERTS, find_next, jnp.int32(-1))
    fill_pad(0, OCT_PER_BLOCK)

    def per_expert(e, carry):
        pos0, blk0, group = carry

        def per_tile(tile, pos):
            n_oct = lax.shift_right_logical(cnt_ref[tile * N_EXPERTS + e] + (OCT_ROWS - 1), OCT_ROWS.bit_length() - 1)
            base = tile * TILE_OCT + off_ref[tile]
            off_ref[tile] = off_ref[tile] + n_oct

            def per_group(grp, c):
                for u in range(PLAN_UNROLL):
                    o = grp * PLAN_UNROLL + u
                    src_ref[pos + o] = base + o
                    dst_ref[pos + o] = base + o
                return c

            per_group(0, 0)
            per_group(1, 0)
            lax.fori_loop(2, lax.shift_right_logical(n_oct + (PLAN_UNROLL - 1), PLAN_UNROLL.bit_length() - 1),
                          per_group, 0)
            return pos + n_oct

        pos1 = lax.fori_loop(0, n_tiles, per_tile, pos0)
        nb = lax.shift_right_logical(pos1 - pos0 + (OCT_PER_BLOCK - 1), OCT_PER_BLOCK.bit_length() - 1)
        pos2 = pos0 + nb * OCT_PER_BLOCK
        fill_pad(pos1, pos2)

        def fill(j, c):
            be_ref[blk0 + j] = e
            wnext_ref[blk0 + j] = nxt_ref[e]
            wpar_ref[blk0 + j] = group & 1
            return c

        lax.fori_loop(0, nb, fill, 0)
        return pos2, blk0 + nb, group + jnp.where(nb > 0, 1, 0)

    pos, n_used, _ = lax.fori_loop(0, N_EXPERTS, per_expert,
                                   (jnp.int32(OCT_PER_BLOCK), jnp.int32(0), jnp.int32(0)))
    nu_ref[0] = n_used
    tail_e = be_ref[n_used - 1]

    def tail(j, c):
        be_ref[j] = tail_e
        wnext_ref[j] = -1
        wpar_ref[j] = 0
        return c

    lax.fori_loop(n_used, n_sched, tail, 0)
    fill_pad(pos, src_ref.shape[0])


def _plan(counts, n_tiles, n_blocks):
    smem = pl.BlockSpec(memory_space=pltpu.SMEM)
    grid_spec = pltpu.PrefetchScalarGridSpec(
        num_scalar_prefetch=1,
        grid=(1,),
        in_specs=[],
        out_specs=[smem] * 6,
        scratch_shapes=[pltpu.SMEM((n_tiles,), I32), pltpu.SMEM((N_EXPERTS,), I32)],
    )
    octs = jax.ShapeDtypeStruct(((n_blocks + 2) * OCT_PER_BLOCK,), I32)
    sched = jax.ShapeDtypeStruct((n_blocks + 1,), I32)
    return pl.pallas_call(
        functools.partial(_plan_kernel, n_tiles=n_tiles),
        grid_spec=grid_spec,
        out_shape=[octs, octs, sched, jax.ShapeDtypeStruct((1,), I32), sched, sched],
        compiler_params=pltpu.CompilerParams(dimension_semantics=("arbitrary",), vmem_limit_bytes=VMEM_LIMIT),
        name="plan",
    )(counts)


def _localsort_kernel(q_ref, h3_ref, xs_ref):
    xs_ref[...] = jnp.zeros_like(xs_ref)

    @pl.when(pl.program_id(0) < pl.num_programs(0) - 1)
    def _():
        def group(g, c):
            for u in range(TILE_SUBLANES):
                t = g * TILE_SUBLANES + u
                row = _tile_rows(h3_ref, t)[...]
                for k in range(TOP_K):
                    _tile_at(xs_ref, q_ref[0, t * TOP_K + k])[...] = row
            return c

        lax.fori_loop(0, TOK_TILE // TILE_SUBLANES, group, 0)


def _localsort(q_tiles, h3t):
    n_tiles = q_tiles.shape[0]
    last = n_tiles - 1
    region = TILE_OCT * OCT_ROWS * TILE_SUBLANES
    return pl.pallas_call(
        _localsort_kernel,
        grid=(n_tiles + 1,),
        in_specs=[pl.BlockSpec((None, 1, TOP_K * TOK_TILE), lambda i: (jnp.minimum(i, last), 0, 0),
                               memory_space=pltpu.SMEM),
                  pl.BlockSpec((TOK_TILE * TILE_SUBLANES, LANES), lambda i: (jnp.minimum(i, last), 0))],
        out_specs=pl.BlockSpec((region, LANES), lambda i: (i, 0)),
        out_shape=jax.ShapeDtypeStruct(((n_tiles + 1) * region, LANES), F32),
        compiler_params=pltpu.CompilerParams(dimension_semantics=("parallel",), vmem_limit_bytes=VMEM_LIMIT),
        name="localsort",
    )(q_tiles, h3t)


def _tile_at(ref, sublane_row):
    return ref.at[pl.ds(pl.multiple_of(sublane_row, TILE_SUBLANES), TILE_SUBLANES), :]


def _tile_rows(ref, row, n_rows=1):
    return ref.at[pl.ds(pl.multiple_of(row * TILE_SUBLANES, TILE_SUBLANES), n_rows * TILE_SUBLANES), :]


def _expert_kernel(be_ref, nu_ref, wnext_ref, wpar_ref,
                   src_cur_ref, src_next_ref, dst_prev_ref, bgu_ref, bd_ref, xs_ref, wgu_hbm, wd_hbm,
                   y_ref,
                   xbuf0, xbuf1, obuf0, obuf1, wgu_f32, wd_f32, wgu_bf, wd_bf, gsem, ssem, wsem):
    j = pl.program_id(0)
    nu = nu_ref[0]
    xbuf = (xbuf0, xbuf1)
    obuf = (obuf0, obuf1)
    octet = lambda ref, o: _tile_rows(ref, o * OCT_ROWS, OCT_ROWS)

    def gather_octet(src_ref, o, dst):
        return pltpu.make_async_copy(octet(xs_ref, src_ref[0, o]), octet(dst, o), gsem)

    def scatter_octet(dst_ref, o, src):
        return pltpu.make_async_copy(octet(src, o), octet(y_ref, dst_ref[0, o]), ssem)

    def weight_copies(e, p):
        return (pltpu.make_async_copy(wgu_hbm.at[e], wgu_f32.at[p], wsem.at[p, 0]),
                pltpu.make_async_copy(wd_hbm.at[e], wd_f32.at[p], wsem.at[p, 1]))

    def wait_gather(dst):
        for _ in range(OCT_PER_BLOCK):
            pltpu.make_async_copy(octet(xs_ref, 0), octet(dst, 0), gsem).wait()

    def wait_scatter(src):
        for _ in range(OCT_PER_BLOCK):
            pltpu.make_async_copy(octet(src, 0), octet(y_ref, 0), ssem).wait()

    @pl.when(j == 0)
    def _():
        obuf1[...] = jnp.zeros_like(obuf1)
        for c in weight_copies(be_ref[0], 0):
            c.start(priority=WEIGHT_DMA_PRIORITY)
        for o in range(OCT_PER_BLOCK):
            gather_octet(src_cur_ref, o, xbuf0).start(priority=GATHER_DMA_PRIORITY)

    first_of_group = jnp.logical_or(j == 0, be_ref[j] != be_ref[jnp.maximum(j - 1, 0)])

    @pl.when(jnp.logical_and(j < nu, first_of_group))
    def _():
        p = wpar_ref[j]
        for c in weight_copies(be_ref[j], p):
            c.wait()
        wgu_bf[...] = wgu_f32[p].astype(BF16)
        wd_bf[...] = wd_f32[p].astype(BF16)

        @pl.when(wnext_ref[j] >= 0)
        def _():
            for c in weight_copies(wnext_ref[j], 1 - p):
                c.start(priority=WEIGHT_DMA_PRIORITY)

    def step(s, compute):
        wait_gather(xbuf[s])

        @pl.when(j >= 1)
        def _():
            wait_scatter(obuf[s])

        def start_scatter():
            for o in range(OCT_PER_BLOCK):
                scatter_octet(dst_prev_ref, o, obuf[1 - s]).start(priority=SCATTER_DMA_PRIORITY)

        if compute:
            for o in range(OCT_PER_BLOCK):
                gather_octet(src_next_ref, o, xbuf[1 - s]).start(priority=GATHER_DMA_PRIORITY)
            x = _load_token_tiles(xbuf[s], ROW_BLOCK).astype(BF16)
            gu = _dot(x, wgu_bf[...]) + bgu_ref[0]
            gate = jnp.minimum(gu[:, :D_EXPERT], SWIGLU_LIMIT)
            up = jnp.clip(gu[:, D_EXPERT:], -SWIGLU_LIMIT, SWIGLU_LIMIT)
            glu = gate * (1.0 / (1.0 + jnp.exp(-SWIGLU_ALPHA * gate)))
            h = ((up + 1.0) * glu).astype(BF16)
            start_scatter()
            _store_token_tiles(obuf[s], _dot(h, wd_bf[...]) + bd_ref[0], ROW_BLOCK)
        else:
            start_scatter()
            wait_scatter(obuf[1 - s])

    for s in range(2):
        parity = (j & 1) == s
        pl.when(jnp.logical_and(j < nu, parity))(functools.partial(step, s, True))
        pl.when(jnp.logical_and(j == nu, parity))(functools.partial(step, s, False))


def _experts(block_e, n_used, w_next, w_par, src3, dst3, xs, wgu, bgu, wd, bd):
    n_steps = block_e.shape[0]
    blk_rows = ROW_BLOCK * TILE_SUBLANES
    last_blk = src3.shape[0] - 1
    exp_map = lambda j, be, nu, wn, wp: (be[j], 0, 0)
    oct_spec = lambda off: pl.BlockSpec((None, 1, OCT_PER_BLOCK),
                                        lambda j, be, nu, wn, wp: (jnp.minimum(j + off, last_blk), 0, 0),
                                        memory_space=pltpu.SMEM)
    hbm = pl.BlockSpec(memory_space=pl.ANY)
    grid_spec = pltpu.PrefetchScalarGridSpec(
        num_scalar_prefetch=4,
        grid=(n_steps,),
        in_specs=[oct_spec(1), oct_spec(2), oct_spec(0),
                  pl.BlockSpec((1, 1, 2 * D_EXPERT), exp_map), pl.BlockSpec((1, 1, D_MODEL), exp_map),
                  hbm, hbm, hbm],
        out_specs=hbm,
        scratch_shapes=[pltpu.VMEM((blk_rows, LANES), F32)] * 4
        + [pltpu.VMEM((2, D_MODEL, 2 * D_EXPERT), F32), pltpu.VMEM((2, D_EXPERT, D_MODEL), F32),
           pltpu.VMEM((D_MODEL, 2 * D_EXPERT), BF16), pltpu.VMEM((D_EXPERT, D_MODEL), BF16),
           pltpu.SemaphoreType.DMA, pltpu.SemaphoreType.DMA, pltpu.SemaphoreType.DMA((2, 2))],
    )
    return pl.pallas_call(
        _expert_kernel,
        grid_spec=grid_spec,
        out_shape=jax.ShapeDtypeStruct(xs.shape, F32),
        input_output_aliases={9: 0},
        compiler_params=pltpu.CompilerParams(dimension_semantics=("arbitrary",), vmem_limit_bytes=VMEM_LIMIT),
        name="experts",
    )(block_e, n_used, w_next, w_par, src3, src3, dst3, bgu, bd, xs, wgu, wd)


def _combine_kernel(q_ref, w_ref, y_ref, x2_ref, g_ref, out_ref, acc_ref):
    def group(grp, c):
        for u in range(TILE_SUBLANES):
            t = grp * TILE_SUBLANES + u
            acc = w_ref[0, t * TOP_K] * _tile_at(y_ref, q_ref[0, t * TOP_K])[...]
            for k in range(1, TOP_K):
                acc = acc + w_ref[0, t * TOP_K + k] * _tile_at(y_ref, q_ref[0, t * TOP_K + k])[...]
            _tile_rows(acc_ref, t)[...] = acc
        return c

    lax.fori_loop(0, TOK_TILE // TILE_SUBLANES, group, 0)
    out_ref[...] = _rms(x2_ref[...] + _load_token_tiles(acc_ref, TOK_TILE), g_ref[...])


def _combine(q_tiles, w_tiles, ybuf, x2, g):
    n_tiles = q_tiles.shape[0]
    region = TILE_OCT * OCT_ROWS * TILE_SUBLANES
    tok4 = pl.BlockSpec((None, 1, TOP_K * TOK_TILE), lambda i: (i, 0, 0), memory_space=pltpu.SMEM)
    return pl.pallas_call(
        _combine_kernel,
        grid=(n_tiles,),
        in_specs=[tok4, tok4,
                  pl.BlockSpec((region, LANES), lambda i: (i, 0)),
                  pl.BlockSpec((TOK_TILE, D_MODEL), lambda i: (i, 0)),
                  pl.BlockSpec((1, D_MODEL), lambda i: (0, 0))],
        out_specs=pl.BlockSpec((TOK_TILE, D_MODEL), lambda i: (i, 0)),
        out_shape=jax.ShapeDtypeStruct(x2.shape, F32),
        scratch_shapes=[pltpu.VMEM((TOK_TILE * TILE_SUBLANES, LANES), F32)],
        compiler_params=pltpu.CompilerParams(dimension_semantics=("parallel",), vmem_limit_bytes=VMEM_LIMIT),
        name="combine",
    )(q_tiles, w_tiles, ybuf, x2, g)


def _layer(x, mem, norm_mix, w_in, conv_w, g_conv_out, g_fft_out, w_out, norm_xattn, norm_mem,
           w_q, w_k, w_v, w_o, norm_ffn, w_router, b_router, w_gate_up, b_gate_up, w_down, b_down, tables):
    bsz, seq, d = x.shape
    t = bsz * seq
    cs, m1, m2 = tables
    row = lambda v: v.reshape(1, -1)

    kmem, vmem_ = _memkv(mem, row(norm_mem), w_k.astype(BF16), w_v.astype(BF16))

    bg, cv, a, s = _inproj(x.reshape(t, d), row(norm_mix), w_in.astype(BF16), cs)
    cols = FFT_N2 * FFT_WIDTH
    g = _fft1(a.reshape(bsz, FFT_N1, cols), s.reshape(bsz, FFT_N1, cols), m1)
    yf = _fft2(g.reshape(bsz, 2, FFT_N1, FFT_N2, FFT_WIDTH), m2).reshape(bsz, seq, FFT_WIDTH)

    w_out_b = w_out.astype(BF16)
    wr_t = w_router.T
    wr_hi = wr_t.astype(BF16)
    wr_lo = (wr_t - wr_hi.astype(F32)).astype(BF16)
    tri = (jnp.arange(TOK_TILE)[:, None] < jnp.arange(TOK_TILE)[None, :]).astype(BF16)
    tri32 = (jnp.arange(N_EXPERTS)[None, :] < jnp.arange(N_EXPERTS)[:, None]).astype(BF16)
    x2, h3, w_tiles, q_tiles, cnt = _post(
        x, bg.reshape(bsz, seq, CONV_WIDTH), cv.reshape(bsz, seq, CONV_WIDTH), yf,
        conv_w, row(g_conv_out), row(g_fft_out), w_out_b[:CONV_WIDTH], w_out_b[CONV_WIDTH:],
        row(norm_xattn), w_q.astype(BF16), kmem, vmem_, w_o.astype(BF16), row(norm_ffn),
        wr_hi, wr_lo, jnp.broadcast_to(b_router[:, None], (N_EXPERTS, 128)), tri, tri32)

    n_tiles = t // TOK_TILE
    n_blocks = (t * TOP_K + n_tiles * N_EXPERTS * (OCT_ROWS - 1)) // ROW_BLOCK + N_EXPERTS
    counts = cnt[:, :, 0].astype(I32).reshape(-1)
    src, dst, block_e, n_used, w_next, w_par = _plan(counts, n_tiles, n_blocks)
    per_block = lambda v: v.reshape(n_blocks + 2, 1, OCT_PER_BLOCK)
    by_token = lambda v: v.transpose(0, 2, 1).reshape(n_tiles, 1, TOK_TILE * TOP_K)
    q_tiles, w_tiles = by_token(q_tiles), by_token(w_tiles)
    xs = _localsort(q_tiles, h3)
    ybuf = _experts(block_e, n_used, w_next, w_par, per_block(src), per_block(dst), xs,
                    w_gate_up, b_gate_up[:, None, :], w_down, b_down[:, None, :])
    return x2.reshape(t, d), ybuf, q_tiles, w_tiles


def kernel(x, mem, norm_mix, w_in, conv_w, g_conv_out, g_fft_out, w_out, norm_xattn, norm_mem, w_q, w_k, w_v, w_o,
           norm_ffn, w_router, b_router, w_gate_up, b_gate_up, w_down, b_down, norm_final):
    bsz, seq, d = x.shape
    depth = norm_mix.shape[0]
    assert depth == 1, "final norm is fused into the combine step of the single layer"
    tables = _dft_tables(seq)
    x2, ybuf, q_tiles, w_tiles = _layer(
        x, mem, norm_mix[0], w_in[0], conv_w[0], g_conv_out[0], g_fft_out[0], w_out[0], norm_xattn[0],
        norm_mem[0], w_q[0], w_k[0], w_v[0], w_o[0], norm_ffn[0], w_router[0], b_router[0],
        w_gate_up[0], b_gate_up[0], w_down[0], b_down[0], tables)
    out = _combine(q_tiles, w_tiles, ybuf, x2, norm_final.reshape(1, -1))
    return out.reshape(bsz, seq, d)
```

```python
import functools
import math

import numpy as np
import jax
import jax.numpy as jnp
from jax import lax
from jax.experimental import pallas as pl
from jax.experimental.pallas import tpu as pltpu

F32 = jnp.float32
BF16 = jnp.bfloat16
I32 = jnp.int32

D_MODEL = 1024
CONV_WIDTH = 512
FFT_WIDTH = 512
GROUP_DIM = 64
IN_PROJ_WIDTH = 3 * CONV_WIDTH + FFT_WIDTH
MEM_LEN = 256
XATTN_HEADS = 4
XATTN_HEAD_DIM = D_MODEL // XATTN_HEADS
N_EXPERTS = 32
TOP_K = 4
D_EXPERT = D_MODEL
SWIGLU_LIMIT = 7.0
SWIGLU_ALPHA = 1.702
EPS = 1e-5

FFT_N1 = 64
FFT_N2 = 128

TOK_TILE = 512
POST_SUB = 2
INPROJ_TILE = 1024
ROW_BLOCK = 512
GATHER_DMA_PRIORITY = 0
SCATTER_DMA_PRIORITY = 1
WEIGHT_DMA_PRIORITY = 1
OCT_ROWS = 8
OCT_PER_BLOCK = ROW_BLOCK // OCT_ROWS
PLAN_UNROLL = 8
TILE_OCT = (TOK_TILE * TOP_K) // OCT_ROWS + N_EXPERTS
DFT_BLOCK = 256
FFT1_LANES = 16384
FFT2_K1 = 16
BF16_SUBLANES = 16
TILE_SUBLANES = 8
LANES = 128
assert D_MODEL == TILE_SUBLANES * LANES
VMEM_LIMIT = 56 * 1024 * 1024


def _rms(x, g):
    return x * lax.rsqrt(jnp.mean(x * x, axis=-1, keepdims=True) + EPS) * g


def _dot(a, b):
    return jnp.dot(a, b, preferred_element_type=F32)


def _dot_nt(a, b):
    return lax.dot_general(a, b, (((1,), (1,)), ((), ())), preferred_element_type=F32)


def _load_token_tiles(ref, rows, base=0):
    return jnp.concatenate(
        [ref[pl.ds(base + s, rows, stride=TILE_SUBLANES), :] for s in range(TILE_SUBLANES)], axis=-1)


def _store_token_tiles(ref, val, rows, base=0):
    for s in range(TILE_SUBLANES):
        ref[pl.ds(base + s, rows, stride=TILE_SUBLANES), :] = val[:, s * LANES:(s + 1) * LANES]


def _dft_tables(seq):
    assert seq == FFT_N1 * FFT_N2
    c = np.arange(GROUP_DIM)
    ang = 2.0 * np.pi * ((c[:, None] * c[None, :]) % GROUP_DIM) / GROUP_DIM
    eye = np.eye(DFT_BLOCK // GROUP_DIM)
    cs = np.concatenate([np.kron(eye, np.cos(ang)), np.kron(eye, np.sin(ang))], axis=1) / math.sqrt(GROUP_DIM)
    n1 = np.arange(FFT_N1)
    a1 = 2.0 * np.pi * ((n1[:, None] * n1[None, :]) % FFT_N1) / FFT_N1
    c1, s1 = np.cos(a1), np.sin(a1)
    m1 = np.block([[c1, -s1], [s1, c1]]) / math.sqrt(FFT_N1)
    k1 = np.arange(FFT_N1)[:, None, None]
    k2 = np.arange(FFT_N2)[None, :, None]
    n2 = np.arange(FFT_N2)[None, None, :]
    a2 = 2.0 * np.pi * ((n2 * (k1 + FFT_N1 * k2)) % seq) / seq
    m2 = np.concatenate([np.cos(a2), -np.sin(a2)], axis=2) / math.sqrt(FFT_N2)
    return (jnp.asarray(cs, F32).astype(BF16), jnp.asarray(m1, F32).astype(BF16),
            jnp.asarray(m2, F32).astype(BF16))


def _memkv_kernel(mem_ref, g_ref, wk_ref, wv_ref, k_ref, v_ref):
    m = _rms(mem_ref[0], g_ref[...]).astype(BF16)
    k_ref[0] = _dot(m, wk_ref[...]).astype(BF16)
    v_ref[0] = _dot(m, wv_ref[...]).astype(BF16)


def _memkv(mem, g, wk, wv):
    bsz = mem.shape[0]
    full = lambda shape: pl.BlockSpec(shape, lambda b: (0,) * len(shape))
    per_b = pl.BlockSpec((1, MEM_LEN, D_MODEL), lambda b: (b, 0, 0))
    return pl.pallas_call(
        _memkv_kernel,
        grid=(bsz,),
        in_specs=[per_b, full((1, D_MODEL)), full((D_MODEL, D_MODEL)), full((D_MODEL, D_MODEL))],
        out_specs=[per_b, per_b],
        out_shape=[jax.ShapeDtypeStruct((bsz, MEM_LEN, D_MODEL), BF16)] * 2,
        compiler_params=pltpu.CompilerParams(dimension_semantics=("arbitrary",), vmem_limit_bytes=VMEM_LIMIT),
        name="memkv",
    )(mem, g, wk, wv)


def _inproj_kernel(x_ref, g_ref, win_ref, cs_ref, b_ref, cv_ref, a_ref, s_ref):
    h = _rms(x_ref[...], g_ref[...]).astype(BF16)
    z = _dot(h, win_ref[...])
    b_ref[...] = z[:, :CONV_WIDTH].astype(BF16)
    cv_ref[...] = (z[:, CONV_WIDTH:2 * CONV_WIDTH] * z[:, 2 * CONV_WIDTH:3 * CONV_WIDTH]).astype(BF16)
    u = z[:, 3 * CONV_WIDTH:].astype(BF16)
    for blk in range(FFT_WIDTH // DFT_BLOCK):
        cols = slice(blk * DFT_BLOCK, (blk + 1) * DFT_BLOCK)
        ab = _dot(u[:, cols], cs_ref[...])
        a_ref[:, cols] = ab[:, :DFT_BLOCK].astype(BF16)
        s_ref[:, cols] = ab[:, DFT_BLOCK:].astype(BF16)


def _inproj(x2d, g, w_in, cs):
    t = x2d.shape[0]
    tile = pl.BlockSpec((INPROJ_TILE, D_MODEL), lambda i: (i, 0))
    half = pl.BlockSpec((INPROJ_TILE, CONV_WIDTH), lambda i: (i, 0))
    full = lambda shape: pl.BlockSpec(shape, lambda i: (0,) * len(shape))
    return pl.pallas_call(
        _inproj_kernel,
        grid=(t // INPROJ_TILE,),
        in_specs=[tile, full((1, D_MODEL)), full((D_MODEL, IN_PROJ_WIDTH)), full((DFT_BLOCK, 2 * DFT_BLOCK))],
        out_specs=[half] * 4,
        out_shape=[jax.ShapeDtypeStruct((t, CONV_WIDTH), BF16)] * 4,
        compiler_params=pltpu.CompilerParams(dimension_semantics=("parallel",), vmem_limit_bytes=VMEM_LIMIT),
        name="inproj",
    )(x2d, g, w_in, cs)


def _fft1_kernel(a_ref, s_ref, m1_ref, g_ref):
    x = jnp.concatenate([a_ref[0], s_ref[0]], axis=0)
    g_ref[0] = _dot(m1_ref[...], x).astype(BF16)


def _fft1(a3, s3, m1):
    bsz, _, cols = a3.shape
    blk = pl.BlockSpec((1, FFT_N1, FFT1_LANES), lambda b, j: (b, 0, j))
    return pl.pallas_call(
        _fft1_kernel,
        grid=(bsz, cols // FFT1_LANES),
        in_specs=[blk, blk, pl.BlockSpec((2 * FFT_N1, 2 * FFT_N1), lambda b, j: (0, 0))],
        out_specs=pl.BlockSpec((1, 2 * FFT_N1, FFT1_LANES), lambda b, j: (b, 0, j)),
        out_shape=jax.ShapeDtypeStruct((bsz, 2 * FFT_N1, cols), BF16),
        compiler_params=pltpu.CompilerParams(dimension_semantics=("parallel", "parallel"),
                                             vmem_limit_bytes=VMEM_LIMIT),
        name="fft1",
    )(a3, s3, m1)


def _fft2_kernel(g_ref, m2_ref, y_ref):
    for j in range(FFT2_K1):
        x = jnp.concatenate([g_ref[0, 0, j], g_ref[0, 1, j]], axis=0)
        y_ref[0, :, j * FFT_WIDTH:(j + 1) * FFT_WIDTH] = _dot(m2_ref[j], x).astype(BF16)


def _fft2(g5, m2):
    bsz = g5.shape[0]
    return pl.pallas_call(
        _fft2_kernel,
        grid=(bsz, FFT_N1 // FFT2_K1),
        in_specs=[pl.BlockSpec((1, 2, FFT2_K1, FFT_N2, FFT_WIDTH), lambda b, j: (b, 0, j, 0, 0)),
                  pl.BlockSpec((FFT2_K1, FFT_N2, 2 * FFT_N2), lambda b, j: (j, 0, 0))],
        out_specs=pl.BlockSpec((1, FFT_N2, FFT2_K1 * FFT_WIDTH), lambda b, j: (b, 0, j)),
        out_shape=jax.ShapeDtypeStruct((bsz, FFT_N2, FFT_N1 * FFT_WIDTH), BF16),
        compiler_params=pltpu.CompilerParams(dimension_semantics=("parallel", "parallel"),
                                             vmem_limit_bytes=VMEM_LIMIT),
        name="fft2",
    )(g5, m2)


def _post_kernel(x_ref, bg_ref, cv_ref, cvp_ref, cvn_ref, yf_ref, convw_ref, gc_ref, gf_ref,
                 wot_ref, wob_ref, nx_ref, wq_ref, k_ref, v_ref, wo_ref, nf_ref,
                 wrh_ref, wrl_ref, br_ref, tri_ref, tri32_ref,
                 x2_ref, h3_ref, w_ref, q_ref, cnt_ref):
    tiles = [_post_tile(sub, x_ref, bg_ref, cv_ref, cvp_ref, cvn_ref, yf_ref, convw_ref, gc_ref, gf_ref,
                        wot_ref, wob_ref, nx_ref, wq_ref, k_ref, v_ref, wo_ref, nf_ref,
                        wrh_ref, wrl_ref, br_ref, tri_ref, tri32_ref,
                        x2_ref, h3_ref, w_ref, q_ref, cnt_ref) for sub in range(POST_SUB)]
    running = []
    while tiles or running:
        if tiles:
            running.append(tiles.pop(0))
        for tile in list(running):
            if next(tile, "done") == "done":
                running.remove(tile)


def _post_tile(sub, x_ref, bg_ref, cv_ref, cvp_ref, cvn_ref, yf_ref, convw_ref, gc_ref, gf_ref,
               wot_ref, wob_ref, nx_ref, wq_ref, k_ref, v_ref, wo_ref, nf_ref,
               wrh_ref, wrl_ref, br_ref, tri_ref, tri32_ref,
               x2_ref, h3_ref, w_ref, q_ref, cnt_ref):
    i = pl.program_id(1)
    last = pl.num_programs(1) - 1
    r0 = sub * TOK_TILE
    tile = slice(r0, r0 + TOK_TILE)

    cv = cv_ref[0, tile].astype(F32)
    if sub == 0:
        prev_row = jnp.where(i > 0, cvp_ref[0].astype(F32)[BF16_SUBLANES - 1:BF16_SUBLANES, :], 0.0)
    else:
        prev_row = cv_ref[0, r0 - BF16_SUBLANES:r0].astype(F32)[BF16_SUBLANES - 1:BF16_SUBLANES, :]
    if sub == POST_SUB - 1:
        next_row = jnp.where(i < last, cvn_ref[0].astype(F32)[0:1, :], 0.0)
    else:
        next_row = cv_ref[0, r0 + TOK_TILE:r0 + TOK_TILE + BF16_SUBLANES].astype(F32)[0:1, :]
    rows = lax.broadcasted_iota(I32, cv.shape, 0)
    cvm1 = jnp.where(rows == 0, prev_row, pltpu.roll(cv, 1, axis=0))
    cvp1 = jnp.where(rows == TOK_TILE - 1, next_row, pltpu.roll(cv, TOK_TILE - 1, axis=0))
    cw = convw_ref[...]
    y_conv = bg_ref[0, tile].astype(F32) * (cw[0:1] * cvm1 + cw[1:2] * cv + cw[2:3] * cvp1)
    yc_n = _rms(y_conv, gc_ref[...]).astype(BF16)
    yf_n = _rms(yf_ref[0, tile].astype(F32), gf_ref[...]).astype(BF16)
    yield
    x1 = x_ref[0, tile] + _dot(yc_n, wot_ref[...]) + _dot(yf_n, wob_ref[...])
    yield

    h2 = _rms(x1, nx_ref[...]).astype(BF16)
    yield
    q = _dot(h2, wq_ref[...]).astype(BF16)
    yield
    heads = []
    for hd in range(XATTN_HEADS):
        sl = slice(hd * XATTN_HEAD_DIM, (hd + 1) * XATTN_HEAD_DIM)
        s = _dot_nt(q[:, sl], k_ref[0, :, sl]) * (XATTN_HEAD_DIM ** -0.5)
        yield
        e = jnp.exp(s - jnp.max(s, axis=-1, keepdims=True))
        p = e * (1.0 / jnp.sum(e, axis=-1, keepdims=True))
        yield
        heads.append(_dot(p.astype(BF16), v_ref[0, :, sl]).astype(BF16))
    yield
    x2 = x1 + _dot(jnp.concatenate(heads, axis=-1), wo_ref[...])
    x2_ref[0, tile] = x2
    yield

    h3 = _rms(x2, nf_ref[...])
    _store_token_tiles(h3_ref, h3, TOK_TILE, base=r0 * TILE_SUBLANES)
    h3h = h3.astype(BF16)
    h3l = (h3 - h3h.astype(F32)).astype(BF16)
    yield
    logits = (_dot_nt(wrh_ref[...], h3h) + _dot_nt(wrh_ref[...], h3l) + _dot_nt(wrl_ref[...], h3h)
              + br_ref[:, 0:1])
    yield
    eidx = lax.broadcasted_iota(I32, logits.shape, 0)
    work = logits
    vals, idxs = [], []
    for _ in range(TOP_K):
        m = jnp.max(work, axis=0, keepdims=True)
        ik = jnp.min(jnp.where(work == m, eidx, N_EXPERTS), axis=0, keepdims=True)
        vals.append(m)
        idxs.append(ik)
        work = jnp.where(eidx == ik, -jnp.inf, work)
    ex = [jnp.exp(v - vals[0]) for v in vals]
    inv_den = 1.0 / (ex[0] + ex[1] + ex[2] + ex[3])
    w_ref[sub] = jnp.concatenate([e * inv_den for e in ex], axis=0)

    sel = jnp.zeros(logits.shape, F32)
    for ik in idxs:
        sel = sel + jnp.where(eidx == ik, 1.0, 0.0)
    cnt = jnp.broadcast_to(jnp.sum(sel, axis=1, keepdims=True), (N_EXPERTS, LANES))
    cnt_ref[sub] = cnt
    seg_rows = jnp.floor((cnt + (OCT_ROWS - 1)) * (1.0 / OCT_ROWS)) * OCT_ROWS
    seg_start = _dot(tri32_ref[...], seg_rows.astype(BF16))
    pos = seg_start[:, 0:1] + _dot(sel.astype(BF16), tri_ref[...])
    q_ref[sub] = jnp.concatenate(
        [jnp.sum(jnp.where(eidx == ik, pos, 0.0), axis=0, keepdims=True) for ik in idxs],
        axis=0).astype(I32) * TILE_SUBLANES


def _post(x3, bg, cv, yf, conv_w, gc, gf, wo_top, wo_bot, nx, wq, kmem, vmem_, wo, nf, wrh, wrl, br, tri, tri32):
    bsz, seq, _ = x3.shape
    step_rows = POST_SUB * TOK_TILE
    nt = seq // step_rows
    t = bsz * seq
    n_tiles = t // TOK_TILE
    halo_per_tile = step_rows // BF16_SUBLANES
    n_halo = seq // BF16_SUBLANES
    full = lambda shape: pl.BlockSpec(shape, lambda b, i: (0,) * len(shape))
    tile_d = pl.BlockSpec((1, step_rows, D_MODEL), lambda b, i: (b, i, 0))
    tile_h = pl.BlockSpec((1, step_rows, CONV_WIDTH), lambda b, i: (b, i, 0))
    halo_prev = pl.BlockSpec((1, BF16_SUBLANES, CONV_WIDTH),
                             lambda b, i: (b, jnp.maximum(i * halo_per_tile - 1, 0), 0))
    halo_next = pl.BlockSpec((1, BF16_SUBLANES, CONV_WIDTH),
                             lambda b, i: (b, jnp.minimum((i + 1) * halo_per_tile, n_halo - 1), 0))
    mem_blk = pl.BlockSpec((1, MEM_LEN, D_MODEL), lambda b, i: (b, 0, 0))
    tok4 = pl.BlockSpec((POST_SUB, TOP_K, TOK_TILE), lambda b, i: (b * nt + i, 0, 0))
    per_tile = pl.BlockSpec((POST_SUB, N_EXPERTS, LANES), lambda b, i: (b * nt + i, 0, 0))
    tok_tiles = pl.BlockSpec((step_rows * TILE_SUBLANES, LANES), lambda b, i: (b * nt + i, 0))
    return pl.pallas_call(
        _post_kernel,
        grid=(bsz, nt),
        in_specs=[tile_d, tile_h, tile_h, halo_prev, halo_next, tile_h,
                  full((3, CONV_WIDTH)), full((1, CONV_WIDTH)), full((1, FFT_WIDTH)),
                  full((CONV_WIDTH, D_MODEL)), full((FFT_WIDTH, D_MODEL)), full((1, D_MODEL)),
                  full((D_MODEL, D_MODEL)), mem_blk, mem_blk, full((D_MODEL, D_MODEL)), full((1, D_MODEL)),
                  full((N_EXPERTS, D_MODEL)), full((N_EXPERTS, D_MODEL)), full((N_EXPERTS, 128)),
                  full((TOK_TILE, TOK_TILE)), full((N_EXPERTS, N_EXPERTS))],
        out_specs=[tile_d, tok_tiles, tok4, tok4, per_tile],
        out_shape=[jax.ShapeDtypeStruct((bsz, seq, D_MODEL), F32),
                   jax.ShapeDtypeStruct((t * TILE_SUBLANES, LANES), F32),
                   jax.ShapeDtypeStruct((n_tiles, TOP_K, TOK_TILE), F32),
                   jax.ShapeDtypeStruct((n_tiles, TOP_K, TOK_TILE), I32),
                   jax.ShapeDtypeStruct((n_tiles, N_EXPERTS, LANES), F32)],
        compiler_params=pltpu.CompilerParams(dimension_semantics=("parallel", "parallel"),
                                             vmem_limit_bytes=VMEM_LIMIT),
        name="post",
    )(x3, bg, cv, cv, cv, yf, conv_w, gc, gf, wo_top, wo_bot, nx, wq, kmem, vmem_, wo, nf, wrh, wrl, br, tri, tri32)


def _plan_kernel(cnt_ref, src_ref, dst_ref, be_ref, nu_ref, wnext_ref, wpar_ref, off_ref, nxt_ref, *, n_tiles):
    n_sched = be_ref.shape[0]
    dump0 = n_tiles * TILE_OCT

    def fill_pad(lo, hi):
        def body(p, c):
            o = p & (OCT_PER_BLOCK - 1)
            parity = lax.shift_right_logical(p, OCT_PER_BLOCK.bit_length() - 1) & 1
            src_ref[p] = dump0 + 2 * OCT_PER_BLOCK + o
            dst_ref[p] = dump0 + parity * OCT_PER_BLOCK + o
            return c
        lax.fori_loop(lo, hi, body, 0)

    def clear(tile, c):
        off_ref[tile] = 0
        return c

    lax.fori_loop(0, n_tiles, clear, 0)

    def find_next(i, nxt):
        e = N_EXPERTS - 1 - i
        nxt_ref[e] = nxt
        total = lax.fori_loop(0, n_tiles, lambda tile, s: s + cnt_ref[tile * N_EXPERTS + e], jnp.int32(0))
        return jnp.where(total > 0, e, nxt)

    lax.fori_loop(0, N_EXPERTS, find_next, jnp.int32(-1))
    fill_pad(0, OCT_PER_BLOCK)

    def per_expert(e, carry):
        pos0, blk0, group = carry

        def per_tile(tile, pos):
            n_oct = lax.shift_right_logical(cnt_ref[tile * N_EXPERTS + e] + (OCT_ROWS - 1), OCT_ROWS.bit_length() - 1)
            base = tile * TILE_OCT + off_ref[tile]
            off_ref[tile] = off_ref[tile] + n_oct

            def per_group(grp, c):
                for u in range(PLAN_UNROLL):
                    o = grp * PLAN_UNROLL + u
                    src_ref[pos + o] = base + o
                    dst_ref[pos + o] = base + o
                return c

            per_group(0, 0)
            per_group(1, 0)
            lax.fori_loop(2, lax.shift_right_logical(n_oct + (PLAN_UNROLL - 1), PLAN_UNROLL.bit_length() - 1),
                          per_group, 0)
            return pos + n_oct

        pos1 = lax.fori_loop(0, n_tiles, per_tile, pos0)
        nb = lax.shift_right_logical(pos1 - pos0 + (OCT_PER_BLOCK - 1), OCT_PER_BLOCK.bit_length() - 1)
        pos2 = pos0 + nb * OCT_PER_BLOCK
        fill_pad(pos1, pos2)

        def fill(j, c):
            be_ref[blk0 + j] = e
            wnext_ref[blk0 + j] = nxt_ref[e]
            wpar_ref[blk0 + j] = group & 1
            return c

        lax.fori_loop(0, nb, fill, 0)
        return pos2, blk0 + nb, group + jnp.where(nb > 0, 1, 0)

    pos, n_used, _ = lax.fori_loop(0, N_EXPERTS, per_expert,
                                   (jnp.int32(OCT_PER_BLOCK), jnp.int32(0), jnp.int32(0)))
    nu_ref[0] = n_used
    tail_e = be_ref[n_used - 1]

    def tail(j, c):
        be_ref[j] = tail_e
        wnext_ref[j] = -1
        wpar_ref[j] = 0
        return c

    lax.fori_loop(n_used, n_sched, tail, 0)
    fill_pad(pos, src_ref.shape[0])


def _plan(counts, n_tiles, n_blocks):
    smem = pl.BlockSpec(memory_space=pltpu.SMEM)
    grid_spec = pltpu.PrefetchScalarGridSpec(
        num_scalar_prefetch=1,
        grid=(1,),
        in_specs=[],
        out_specs=[smem] * 6,
        scratch_shapes=[pltpu.SMEM((n_tiles,), I32), pltpu.SMEM((N_EXPERTS,), I32)],
    )
    octs = jax.ShapeDtypeStruct(((n_blocks + 2) * OCT_PER_BLOCK,), I32)
    sched = jax.ShapeDtypeStruct((n_blocks + 1,), I32)
    return pl.pallas_call(
        functools.partial(_plan_kernel, n_tiles=n_tiles),
        grid_spec=grid_spec,
        out_shape=[octs, octs, sched, jax.ShapeDtypeStruct((1,), I32), sched, sched],
        compiler_params=pltpu.CompilerParams(dimension_semantics=("arbitrary",), vmem_limit_bytes=VMEM_LIMIT),
        name="plan",
    )(counts)


def _localsort_kernel(q_ref, h3_ref, xs_ref):
    xs_ref[...] = jnp.zeros_like(xs_ref)

    @pl.when(pl.program_id(0) < pl.num_programs(0) - 1)
    def _():
        def group(g, c):
            for u in range(TILE_SUBLANES):
                t = g * TILE_SUBLANES + u
                row = _tile_rows(h3_ref, t)[...]
                for k in range(TOP_K):
                    _tile_at(xs_ref, q_ref[0, t * TOP_K + k])[...] = row
            return c

        lax.fori_loop(0, TOK_TILE // TILE_SUBLANES, group, 0)


def _localsort(q_tiles, h3t):
    n_tiles = q_tiles.shape[0]
    last = n_tiles - 1
    region = TILE_OCT * OCT_ROWS * TILE_SUBLANES
    return pl.pallas_call(
        _localsort_kernel,
        grid=(n_tiles + 1,),
        in_specs=[pl.BlockSpec((None, 1, TOP_K * TOK_TILE), lambda i: (jnp.minimum(i, last), 0, 0),
                               memory_space=pltpu.SMEM),
                  pl.BlockSpec((TOK_TILE * TILE_SUBLANES, LANES), lambda i: (jnp.minimum(i, last), 0))],
        out_specs=pl.BlockSpec((region, LANES), lambda i: (i, 0)),
        out_shape=jax.ShapeDtypeStruct(((n_tiles + 1) * region, LANES), F32),
        compiler_params=pltpu.CompilerParams(dimension_semantics=("parallel",), vmem_limit_bytes=VMEM_LIMIT),
        name="localsort",
    )(q_tiles, h3t)


def _tile_at(ref, sublane_row):
    return ref.at[pl.ds(pl.multiple_of(sublane_row, TILE_SUBLANES), TILE_SUBLANES), :]


def _tile_rows(ref, row, n_rows=1):
    return ref.at[pl.ds(pl.multiple_of(row * TILE_SUBLANES, TILE_SUBLANES), n_rows * TILE_SUBLANES), :]


def _expert_kernel(be_ref, nu_ref, wnext_ref, wpar_ref,
                   src_cur_ref, src_next_ref, dst_prev_ref, bgu_ref, bd_ref, xs_ref, wgu_hbm, wd_hbm,
                   y_ref,
                   xbuf0, xbuf1, obuf0, obuf1, wgu_f32, wd_f32, wgu_bf, wd_bf, gsem, ssem, wsem):
    j = pl.program_id(0)
    nu = nu_ref[0]
    xbuf = (xbuf0, xbuf1)
    obuf = (obuf0, obuf1)
    octet = lambda ref, o: _tile_rows(ref, o * OCT_ROWS, OCT_ROWS)

    def gather_octet(src_ref, o, dst):
        return pltpu.make_async_copy(octet(xs_ref, src_ref[0, o]), octet(dst, o), gsem)

    def scatter_octet(dst_ref, o, src):
        return pltpu.make_async_copy(octet(src, o), octet(y_ref, dst_ref[0, o]), ssem)

    def weight_copies(e, p):
        return (pltpu.make_async_copy(wgu_hbm.at[e], wgu_f32.at[p], wsem.at[p, 0]),
                pltpu.make_async_copy(wd_hbm.at[e], wd_f32.at[p], wsem.at[p, 1]))

    def wait_gather(dst):
        for _ in range(OCT_PER_BLOCK):
            pltpu.make_async_copy(octet(xs_ref, 0), octet(dst, 0), gsem).wait()

    def wait_scatter(src):
        for _ in range(OCT_PER_BLOCK):
            pltpu.make_async_copy(octet(src, 0), octet(y_ref, 0), ssem).wait()

    @pl.when(j == 0)
    def _():
        obuf1[...] = jnp.zeros_like(obuf1)
        for c in weight_copies(be_ref[0], 0):
            c.start(priority=WEIGHT_DMA_PRIORITY)
        for o in range(OCT_PER_BLOCK):
            gather_octet(src_cur_ref, o, xbuf0).start(priority=GATHER_DMA_PRIORITY)

    first_of_group = jnp.logical_or(j == 0, be_ref[j] != be_ref[jnp.maximum(j - 1, 0)])

    @pl.when(jnp.logical_and(j < nu, first_of_group))
    def _():
        p = wpar_ref[j]
        for c in weight_copies(be_ref[j], p):
            c.wait()
        wgu_bf[...] = wgu_f32[p].astype(BF16)
        wd_bf[...] = wd_f32[p].astype(BF16)

        @pl.when(wnext_ref[j] >= 0)
        def _():
            for c in weight_copies(wnext_ref[j], 1 - p):
                c.start(priority=WEIGHT_DMA_PRIORITY)

    def step(s, compute):
        wait_gather(xbuf[s])

        @pl.when(j >= 1)
        def _():
            wait_scatter(obuf[s])

        def start_scatter():
            for o in range(OCT_PER_BLOCK):
                scatter_octet(dst_prev_ref, o, obuf[1 - s]).start(priority=SCATTER_DMA_PRIORITY)

        if compute:
            for o in range(OCT_PER_BLOCK):
                gather_octet(src_next_ref, o, xbuf[1 - s]).start(priority=GATHER_DMA_PRIORITY)
            x = _load_token_tiles(xbuf[s], ROW_BLOCK).astype(BF16)
            gu = _dot(x, wgu_bf[...]) + bgu_ref[0]
            gate = jnp.minimum(gu[:, :D_EXPERT], SWIGLU_LIMIT)
            up = jnp.clip(gu[:, D_EXPERT:], -SWIGLU_LIMIT, SWIGLU_LIMIT)
            glu = gate * (1.0 / (1.0 + jnp.exp(-SWIGLU_ALPHA * gate)))
            h = ((up + 1.0) * glu).astype(BF16)
            start_scatter()
            _store_token_tiles(obuf[s], _dot(h, wd_bf[...]) + bd_ref[0], ROW_BLOCK)
        else:
            start_scatter()
            wait_scatter(obuf[1 - s])

    for s in range(2):
        parity = (j & 1) == s
        pl.when(jnp.logical_and(j < nu, parity))(functools.partial(step, s, True))
        pl.when(jnp.logical_and(j == nu, parity))(functools.partial(step, s, False))


def _experts(block_e, n_used, w_next, w_par, src3, dst3, xs, wgu, bgu, wd, bd):
    n_steps = block_e.shape[0]
    blk_rows = ROW_BLOCK * TILE_SUBLANES
    last_blk = src3.shape[0] - 1
    exp_map = lambda j, be, nu, wn, wp: (be[j], 0, 0)
    oct_spec = lambda off: pl.BlockSpec((None, 1, OCT_PER_BLOCK),
                                        lambda j, be, nu, wn, wp: (jnp.minimum(j + off, last_blk), 0, 0),
                                        memory_space=pltpu.SMEM)
    hbm = pl.BlockSpec(memory_space=pl.ANY)
    grid_spec = pltpu.PrefetchScalarGridSpec(
        num_scalar_prefetch=4,
        grid=(n_steps,),
        in_specs=[oct_spec(1), oct_spec(2), oct_spec(0),
                  pl.BlockSpec((1, 1, 2 * D_EXPERT), exp_map), pl.BlockSpec((1, 1, D_MODEL), exp_map),
                  hbm, hbm, hbm],
        out_specs=hbm,
        scratch_shapes=[pltpu.VMEM((blk_rows, LANES), F32)] * 4
        + [pltpu.VMEM((2, D_MODEL, 2 * D_EXPERT), F32), pltpu.VMEM((2, D_EXPERT, D_MODEL), F32),
           pltpu.VMEM((D_MODEL, 2 * D_EXPERT), BF16), pltpu.VMEM((D_EXPERT, D_MODEL), BF16),
           pltpu.SemaphoreType.DMA, pltpu.SemaphoreType.DMA, pltpu.SemaphoreType.DMA((2, 2))],
    )
    return pl.pallas_call(
        _expert_kernel,
        grid_spec=grid_spec,
        out_shape=jax.ShapeDtypeStruct(xs.shape, F32),
        input_output_aliases={9: 0},
        compiler_params=pltpu.CompilerParams(dimension_semantics=("arbitrary",), vmem_limit_bytes=VMEM_LIMIT),
        name="experts",
    )(block_e, n_used, w_next, w_par, src3, src3, dst3, bgu, bd, xs, wgu, wd)


def _combine_kernel(q_ref, w_ref, y_ref, x2_ref, g_ref, out_ref, acc_ref):
    def group(grp, c):
        for u in range(TILE_SUBLANES):
            t = grp * TILE_SUBLANES + u
            acc = w_ref[0, t * TOP_K] * _tile_at(y_ref, q_ref[0, t * TOP_K])[...]
            for k in range(1, TOP_K):
                acc = acc + w_ref[0, t * TOP_K + k] * _tile_at(y_ref, q_ref[0, t * TOP_K + k])[...]
            _tile_rows(acc_ref, t)[...] = acc
        return c

    lax.fori_loop(0, TOK_TILE // TILE_SUBLANES, group, 0)
    out_ref[...] = _rms(x2_ref[...] + _load_token_tiles(acc_ref, TOK_TILE), g_ref[...])


def _combine(q_tiles, w_tiles, ybuf, x2, g):
    n_tiles = q_tiles.shape[0]
    region = TILE_OCT * OCT_ROWS * TILE_SUBLANES
    tok4 = pl.BlockSpec((None, 1, TOP_K * TOK_TILE), lambda i: (i, 0, 0), memory_space=pltpu.SMEM)
    return pl.pallas_call(
        _combine_kernel,
        grid=(n_tiles,),
        in_specs=[tok4, tok4,
                  pl.BlockSpec((region, LANES), lambda i: (i, 0)),
                  pl.BlockSpec((TOK_TILE, D_MODEL), lambda i: (i, 0)),
                  pl.BlockSpec((1, D_MODEL), lambda i: (0, 0))],
        out_specs=pl.BlockSpec((TOK_TILE, D_MODEL), lambda i: (i, 0)),
        out_shape=jax.ShapeDtypeStruct(x2.shape, F32),
        scratch_shapes=[pltpu.VMEM((TOK_TILE * TILE_SUBLANES, LANES), F32)],
        compiler_params=pltpu.CompilerParams(dimension_semantics=("parallel",), vmem_limit_bytes=VMEM_LIMIT),
        name="combine",
    )(q_tiles, w_tiles, ybuf, x2, g)


def _layer(x, mem, norm_mix, w_in, conv_w, g_conv_out, g_fft_out, w_out, norm_xattn, norm_mem,
           w_q, w_k, w_v, w_o, norm_ffn, w_router, b_router, w_gate_up, b_gate_up, w_down, b_down, tables):
    bsz, seq, d = x.shape
    t = bsz * seq
    cs, m1, m2 = tables
    row = lambda v: v.reshape(1, -1)

    kmem, vmem_ = _memkv(mem, row(norm_mem), w_k.astype(BF16), w_v.astype(BF16))

    bg, cv, a, s = _inproj(x.reshape(t, d), row(norm_mix), w_in.astype(BF16), cs)
    cols = FFT_N2 * FFT_WIDTH
    g = _fft1(a.reshape(bsz, FFT_N1, cols), s.reshape(bsz, FFT_N1, cols), m1)
    yf = _fft2(g.reshape(bsz, 2, FFT_N1, FFT_N2, FFT_WIDTH), m2).reshape(bsz, seq, FFT_WIDTH)

    w_out_b = w_out.astype(BF16)
    wr_t = w_router.T
    wr_hi = wr_t.astype(BF16)
    wr_lo = (wr_t - wr_hi.astype(F32)).astype(BF16)
    tri = (jnp.arange(TOK_TILE)[:, None] < jnp.arange(TOK_TILE)[None, :]).astype(BF16)
    tri32 = (jnp.arange(N_EXPERTS)[None, :] < jnp.arange(N_EXPERTS)[:, None]).astype(BF16)
    x2, h3, w_tiles, q_tiles, cnt = _post(
        x, bg.reshape(bsz, seq, CONV_WIDTH), cv.reshape(bsz, seq, CONV_WIDTH), yf,
        conv_w, row(g_conv_out), row(g_fft_out), w_out_b[:CONV_WIDTH], w_out_b[CONV_WIDTH:],
        row(norm_xattn), w_q.astype(BF16), kmem, vmem_, w_o.astype(BF16), row(norm_ffn),
        wr_hi, wr_lo, jnp.broadcast_to(b_router[:, None], (N_EXPERTS, 128)), tri, tri32)

    n_tiles = t // TOK_TILE
    n_blocks = (t * TOP_K + n_tiles * N_EXPERTS * (OCT_ROWS - 1)) // ROW_BLOCK + N_EXPERTS
    counts = cnt[:, :, 0].astype(I32).reshape(-1)
    src, dst, block_e, n_used, w_next, w_par = _plan(counts, n_tiles, n_blocks)
    per_block = lambda v: v.reshape(n_blocks + 2, 1, OCT_PER_BLOCK)
    by_token = lambda v: v.transpose(0, 2, 1).reshape(n_tiles, 1, TOK_TILE * TOP_K)
    q_tiles, w_tiles = by_token(q_tiles), by_token(w_tiles)
    xs = _localsort(q_tiles, h3)
    ybuf = _experts(block_e, n_used, w_next, w_par, per_block(src), per_block(dst), xs,
                    w_gate_up, b_gate_up[:, None, :], w_down, b_down[:, None, :])
    return x2.reshape(t, d), ybuf, q_tiles, w_tiles


def kernel(x, mem, norm_mix, w_in, conv_w, g_conv_out, g_fft_out, w_out, norm_xattn, norm_mem, w_q, w_k, w_v, w_o,
           norm_ffn, w_router, b_router, w_gate_up, b_gate_up, w_down, b_down, norm_final):
    bsz, seq, d = x.shape
    depth = norm_mix.shape[0]
    assert depth == 1, "final norm is fused into the combine step of the single layer"
    tables = _dft_tables(seq)
    x2, ybuf, q_tiles, w_tiles = _layer(
        x, mem, norm_mix[0], w_in[0], conv_w[0], g_conv_out[0], g_fft_out[0], w_out[0], norm_xattn[0],
        norm_mem[0], w_q[0], w_k[0], w_v[0], w_o[0], norm_ffn[0], w_router[0], b_router[0],
        w_gate_up[0], b_gate_up[0], w_down[0], b_down[0], tables)
    out = _combine(q_tiles, w_tiles, ybuf, x2, norm_final.reshape(1, -1))
    return out.reshape(bsz, seq, d)
```

```python
import functools
import math

import numpy as np
import jax
import jax.numpy as jnp
from jax import lax
from jax.experimental import pallas as pl
from jax.experimental.pallas import tpu as pltpu

F32 = jnp.float32
BF16 = jnp.bfloat16
I32 = jnp.int32

D_MODEL = 1024
CONV_WIDTH = 512
FFT_WIDTH = 512
GROUP_DIM = 64
IN_PROJ_WIDTH = 3 * CONV_WIDTH + FFT_WIDTH
MEM_LEN = 256
XATTN_HEADS = 4
XATTN_HEAD_DIM = D_MODEL // XATTN_HEADS
N_EXPERTS = 32
TOP_K = 4
D_EXPERT = D_MODEL
SWIGLU_LIMIT = 7.0
SWIGLU_ALPHA = 1.702
EPS = 1e-5

FFT_N1 = 64
FFT_N2 = 128

TOK_TILE = 512
POST_SUB = 2
INPROJ_TILE = 1024
ROW_BLOCK = 512
GATHER_DMA_PRIORITY = 0
SCATTER_DMA_PRIORITY = 1
WEIGHT_DMA_PRIORITY = 1
OCT_ROWS = 8
OCT_PER_BLOCK = ROW_BLOCK // OCT_ROWS
PLAN_UNROLL = 8
TILE_OCT = (TOK_TILE * TOP_K) // OCT_ROWS + N_EXPERTS
DFT_BLOCK = 256
FFT1_LANES = 16384
FFT2_K1 = 16
BF16_SUBLANES = 16
TILE_SUBLANES = 8
LANES = 128
assert D_MODEL == TILE_SUBLANES * LANES
VMEM_LIMIT = 56 * 1024 * 1024


def _rms(x, g):
    return x * lax.rsqrt(jnp.mean(x * x, axis=-1, keepdims=True) + EPS) * g


def _dot(a, b):
    return jnp.dot(a, b, preferred_element_type=F32)


def _dot_nt(a, b):
    return lax.dot_general(a, b, (((1,), (1,)), ((), ())), preferred_element_type=F32)


def _load_token_tiles(ref, rows, base=0):
    return jnp.concatenate(
        [ref[pl.ds(base + s, rows, stride=TILE_SUBLANES), :] for s in range(TILE_SUBLANES)], axis=-1)


def _store_token_tiles(ref, val, rows, base=0):
    for s in range(TILE_SUBLANES):
        ref[pl.ds(base + s, rows, stride=TILE_SUBLANES), :] = val[:, s * LANES:(s + 1) * LANES]


def _dft_tables(seq):
    assert seq == FFT_N1 * FFT_N2
    c = np.arange(GROUP_DIM)
    ang = 2.0 * np.pi * ((c[:, None] * c[None, :]) % GROUP_DIM) / GROUP_DIM
    eye = np.eye(DFT_BLOCK // GROUP_DIM)
    cs = np.concatenate([np.kron(eye, np.cos(ang)), np.kron(eye, np.sin(ang))], axis=1) / math.sqrt(GROUP_DIM)
    n1 = np.arange(FFT_N1)
    a1 = 2.0 * np.pi * ((n1[:, None] * n1[None, :]) % FFT_N1) / FFT_N1
    c1, s1 = np.cos(a1), np.sin(a1)
    m1 = np.block([[c1, -s1], [s1, c1]]) / math.sqrt(FFT_N1)
    k1 = np.arange(FFT_N1)[:, None, None]
    k2 = np.arange(FFT_N2)[None, :, None]
    n2 = np.arange(FFT_N2)[None, None, :]
    a2 = 2.0 * np.pi * ((n2 * (k1 + FFT_N1 * k2)) % seq) / seq
    m2 = np.concatenate([np.cos(a2), -np.sin(a2)], axis=2) / math.sqrt(FFT_N2)
    return (jnp.asarray(cs, F32).astype(BF16), jnp.asarray(m1, F32).astype(BF16),
            jnp.asarray(m2, F32).astype(BF16))


def _memkv_kernel(mem_ref, g_ref, wk_ref, wv_ref, k_ref, v_ref):
    m = _rms(mem_ref[0], g_ref[...]).astype(BF16)
    k_ref[0] = _dot(m, wk_ref[...]).astype(BF16)
    v_ref[0] = _dot(m, wv_ref[...]).astype(BF16)


def _memkv(mem, g, wk, wv):
    bsz = mem.shape[0]
    full = lambda shape: pl.BlockSpec(shape, lambda b: (0,) * len(shape))
    per_b = pl.BlockSpec((1, MEM_LEN, D_MODEL), lambda b: (b, 0, 0))
    return pl.pallas_call(
        _memkv_kernel,
        grid=(bsz,),
        in_specs=[per_b, full((1, D_MODEL)), full((D_MODEL, D_MODEL)), full((D_MODEL, D_MODEL))],
        out_specs=[per_b, per_b],
        out_shape=[jax.ShapeDtypeStruct((bsz, MEM_LEN, D_MODEL), BF16)] * 2,
        compiler_params=pltpu.CompilerParams(dimension_semantics=("arbitrary",), vmem_limit_bytes=VMEM_LIMIT),
        name="memkv",
    )(mem, g, wk, wv)


def _inproj_kernel(x_ref, g_ref, win_ref, cs_ref, b_ref, cv_ref, a_ref, s_ref):
    h = _rms(x_ref[...], g_ref[...]).astype(BF16)
    z = _dot(h, win_ref[...])
    b_ref[...] = z[:, :CONV_WIDTH].astype(BF16)
    cv_ref[...] = (z[:, CONV_WIDTH:2 * CONV_WIDTH] * z[:, 2 * CONV_WIDTH:3 * CONV_WIDTH]).astype(BF16)
    u = z[:, 3 * CONV_WIDTH:].astype(BF16)
    for blk in range(FFT_WIDTH // DFT_BLOCK):
        cols = slice(blk * DFT_BLOCK, (blk + 1) * DFT_BLOCK)
        ab = _dot(u[:, cols], cs_ref[...])
        a_ref[:, cols] = ab[:, :DFT_BLOCK].astype(BF16)
        s_ref[:, cols] = ab[:, DFT_BLOCK:].astype(BF16)


def _inproj(x2d, g, w_in, cs):
    t = x2d.shape[0]
    tile = pl.BlockSpec((INPROJ_TILE, D_MODEL), lambda i: (i, 0))
    half = pl.BlockSpec((INPROJ_TILE, CONV_WIDTH), lambda i: (i, 0))
    full = lambda shape: pl.BlockSpec(shape, lambda i: (0,) * len(shape))
    return pl.pallas_call(
        _inproj_kernel,
        grid=(t // INPROJ_TILE,),
        in_specs=[tile, full((1, D_MODEL)), full((D_MODEL, IN_PROJ_WIDTH)), full((DFT_BLOCK, 2 * DFT_BLOCK))],
        out_specs=[half] * 4,
        out_shape=[jax.ShapeDtypeStruct((t, CONV_WIDTH), BF16)] * 4,
        compiler_params=pltpu.CompilerParams(dimension_semantics=("parallel",), vmem_limit_bytes=VMEM_LIMIT),
        name="inproj",
    )(x2d, g, w_in, cs)


def _fft1_kernel(a_ref, s_ref, m1_ref, g_ref):
    x = jnp.concatenate([a_ref[0], s_ref[0]], axis=0)
    g_ref[0] = _dot(m1_ref[...], x).astype(BF16)


def _fft1(a3, s3, m1):
    bsz, _, cols = a3.shape
    blk = pl.BlockSpec((1, FFT_N1, FFT1_LANES), lambda b, j: (b, 0, j))
    return pl.pallas_call(
        _fft1_kernel,
        grid=(bsz, cols // FFT1_LANES),
        in_specs=[blk, blk, pl.BlockSpec((2 * FFT_N1, 2 * FFT_N1), lambda b, j: (0, 0))],
        out_specs=pl.BlockSpec((1, 2 * FFT_N1, FFT1_LANES), lambda b, j: (b, 0, j)),
        out_shape=jax.ShapeDtypeStruct((bsz, 2 * FFT_N1, cols), BF16),
        compiler_params=pltpu.CompilerParams(dimension_semantics=("parallel", "parallel"),
                                             vmem_limit_bytes=VMEM_LIMIT),
        name="fft1",
    )(a3, s3, m1)


def _fft2_kernel(g_ref, m2_ref, y_ref):
    for j in range(FFT2_K1):
        x = jnp.concatenate([g_ref[0, 0, j], g_ref[0, 1, j]], axis=0)
        y_ref[0, :, j * FFT_WIDTH:(j + 1) * FFT_WIDTH] = _dot(m2_ref[j], x).astype(BF16)


def _fft2(g5, m2):
    bsz = g5.shape[0]
    return pl.pallas_call(
        _fft2_kernel,
        grid=(bsz, FFT_N1 // FFT2_K1),
        in_specs=[pl.BlockSpec((1, 2, FFT2_K1, FFT_N2, FFT_WIDTH), lambda b, j: (b, 0, j, 0, 0)),
                  pl.BlockSpec((FFT2_K1, FFT_N2, 2 * FFT_N2), lambda b, j: (j, 0, 0))],
        out_specs=pl.BlockSpec((1, FFT_N2, FFT2_K1 * FFT_WIDTH), lambda b, j: (b, 0, j)),
        out_shape=jax.ShapeDtypeStruct((bsz, FFT_N2, FFT_N1 * FFT_WIDTH), BF16),
        compiler_params=pltpu.CompilerParams(dimension_semantics=("parallel", "parallel"),
                                             vmem_limit_bytes=VMEM_LIMIT),
        name="fft2",
    )(g5, m2)


def _post_kernel(x_ref, bg_ref, cv_ref, cvp_ref, cvn_ref, yf_ref, convw_ref, gc_ref, gf_ref,
                 wot_ref, wob_ref, nx_ref, wq_ref, k_ref, v_ref, wo_ref, nf_ref,
                 wrh_ref, wrl_ref, br_ref, tri_ref, tri32_ref,
                 x2_ref, h3_ref, w_ref, q_ref, cnt_ref):
    tiles = [_post_tile(sub, x_ref, bg_ref, cv_ref, cvp_ref, cvn_ref, yf_ref, convw_ref, gc_ref, gf_ref,
                        wot_ref, wob_ref, nx_ref, wq_ref, k_ref, v_ref, wo_ref, nf_ref,
                        wrh_ref, wrl_ref, br_ref, tri_ref, tri32_ref,
                        x2_ref, h3_ref, w_ref, q_ref, cnt_ref) for sub in range(POST_SUB)]
    running = []
    while tiles or running:
        if tiles:
            running.append(tiles.pop(0))
        for tile in list(running):
            if next(tile, "done") == "done":
                running.remove(tile)


def _post_tile(sub, x_ref, bg_ref, cv_ref, cvp_ref, cvn_ref, yf_ref, convw_ref, gc_ref, gf_ref,
               wot_ref, wob_ref, nx_ref, wq_ref, k_ref, v_ref, wo_ref, nf_ref,
               wrh_ref, wrl_ref, br_ref, tri_ref, tri32_ref,
               x2_ref, h3_ref, w_ref, q_ref, cnt_ref):
    i = pl.program_id(1)
    last = pl.num_programs(1) - 1
    r0 = sub * TOK_TILE
    tile = slice(r0, r0 + TOK_TILE)

    cv = cv_ref[0, tile].astype(F32)
    if sub == 0:
        prev_row = jnp.where(i > 0, cvp_ref[0].astype(F32)[BF16_SUBLANES - 1:BF16_SUBLANES, :], 0.0)
    else:
        prev_row = cv_ref[0, r0 - BF16_SUBLANES:r0].astype(F32)[BF16_SUBLANES - 1:BF16_SUBLANES, :]
    if sub == POST_SUB - 1:
        next_row = jnp.where(i < last, cvn_ref[0].astype(F32)[0:1, :], 0.0)
    else:
        next_row = cv_ref[0, r0 + TOK_TILE:r0 + TOK_TILE + BF16_SUBLANES].astype(F32)[0:1, :]
    rows = lax.broadcasted_iota(I32, cv.shape, 0)
    cvm1 = jnp.where(rows == 0, prev_row, pltpu.roll(cv, 1, axis=0))
    cvp1 = jnp.where(rows == TOK_TILE - 1, next_row, pltpu.roll(cv, TOK_TILE - 1, axis=0))
    cw = convw_ref[...]
    y_conv = bg_ref[0, tile].astype(F32) * (cw[0:1] * cvm1 + cw[1:2] * cv + cw[2:3] * cvp1)
    yc_n = _rms(y_conv, gc_ref[...]).astype(BF16)
    yf_n = _rms(yf_ref[0, tile].astype(F32), gf_ref[...]).astype(BF16)
    yield
    x1 = x_ref[0, tile] + _dot(yc_n, wot_ref[...]) + _dot(yf_n, wob_ref[...])
    yield

    h2 = _rms(x1, nx_ref[...]).astype(BF16)
    yield
    q = _dot(h2, wq_ref[...]).astype(BF16)
    yield
    heads = []
    for hd in range(XATTN_HEADS):
        sl = slice(hd * XATTN_HEAD_DIM, (hd + 1) * XATTN_HEAD_DIM)
        s = _dot_nt(q[:, sl], k_ref[0, :, sl]) * (XATTN_HEAD_DIM ** -0.5)
        yield
        e = jnp.exp(s - jnp.max(s, axis=-1, keepdims=True))
        p = e * (1.0 / jnp.sum(e, axis=-1, keepdims=True))
        yield
        heads.append(_dot(p.astype(BF16), v_ref[0, :, sl]).astype(BF16))
    yield
    x2 = x1 + _dot(jnp.concatenate(heads, axis=-1), wo_ref[...])
    x2_ref[0, tile] = x2
    yield

    h3 = _rms(x2, nf_ref[...])
    _store_token_tiles(h3_ref, h3, TOK_TILE, base=r0 * TILE_SUBLANES)
    h3h = h3.astype(BF16)
    h3l = (h3 - h3h.astype(F32)).astype(BF16)
    yield
    logits = (_dot_nt(wrh_ref[...], h3h) + _dot_nt(wrh_ref[...], h3l) + _dot_nt(wrl_ref[...], h3h)
              + br_ref[:, 0:1])
    yield
    eidx = lax.broadcasted_iota(I32, logits.shape, 0)
    work = logits
    vals, idxs = [], []
    for _ in range(TOP_K):
        m = jnp.max(work, axis=0, keepdims=True)
        ik = jnp.min(jnp.where(work == m, eidx, N_EXPERTS), axis=0, keepdims=True)
        vals.append(m)
        idxs.append(ik)
        work = jnp.where(eidx == ik, -jnp.inf, work)
    ex = [jnp.exp(v - vals[0]) for v in vals]
    inv_den = 1.0 / (ex[0] + ex[1] + ex[2] + ex[3])
    w_ref[sub] = jnp.concatenate([e * inv_den for e in ex], axis=0)

    sel = jnp.zeros(logits.shape, F32)
    for ik in idxs:
        sel = sel + jnp.where(eidx == ik, 1.0, 0.0)
    cnt = jnp.broadcast_to(jnp.sum(sel, axis=1, keepdims=True), (N_EXPERTS, LANES))
    cnt_ref[sub] = cnt
    seg_rows = jnp.floor((cnt + (OCT_ROWS - 1)) * (1.0 / OCT_ROWS)) * OCT_ROWS
    seg_start = _dot(tri32_ref[...], seg_rows.astype(BF16))
    pos = seg_start[:, 0:1] + _dot(sel.astype(BF16), tri_ref[...])
    q_ref[sub] = jnp.concatenate(
        [jnp.sum(jnp.where(eidx == ik, pos, 0.0), axis=0, keepdims=True) for ik in idxs],
        axis=0).astype(I32) * TILE_SUBLANES


def _post(x3, bg, cv, yf, conv_w, gc, gf, wo_top, wo_bot, nx, wq, kmem, vmem_, wo, nf, wrh, wrl, br, tri, tri32):
    bsz, seq, _ = x3.shape
    step_rows = POST_SUB * TOK_TILE
    nt = seq // step_rows
    t = bsz * seq
    n_tiles = t // TOK_TILE
    halo_per_tile = step_rows // BF16_SUBLANES
    n_halo = seq // BF16_SUBLANES
    full = lambda shape: pl.BlockSpec(shape, lambda b, i: (0,) * len(shape))
    tile_d = pl.BlockSpec((1, step_rows, D_MODEL), lambda b, i: (b, i, 0))
    tile_h = pl.BlockSpec((1, step_rows, CONV_WIDTH), lambda b, i: (b, i, 0))
    halo_prev = pl.BlockSpec((1, BF16_SUBLANES, CONV_WIDTH),
                             lambda b, i: (b, jnp.maximum(i * halo_per_tile - 1, 0), 0))
    halo_next = pl.BlockSpec((1, BF16_SUBLANES, CONV_WIDTH),
                             lambda b, i: (b, jnp.minimum((i + 1) * halo_per_tile, n_halo - 1), 0))
    mem_blk = pl.BlockSpec((1, MEM_LEN, D_MODEL), lambda b, i: (b, 0, 0))
    tok4 = pl.BlockSpec((POST_SUB, TOP_K, TOK_TILE), lambda b, i: (b * nt + i, 0, 0))
    per_tile = pl.BlockSpec((POST_SUB, N_EXPERTS, LANES), lambda b, i: (b * nt + i, 0, 0))
    tok_tiles = pl.BlockSpec((step_rows * TILE_SUBLANES, LANES), lambda b, i: (b * nt + i, 0))
    return pl.pallas_call(
        _post_kernel,
        grid=(bsz, nt),
        in_specs=[tile_d, tile_h, tile_h, halo_prev, halo_next, tile_h,
                  full((3, CONV_WIDTH)), full((1, CONV_WIDTH)), full((1, FFT_WIDTH)),
                  full((CONV_WIDTH, D_MODEL)), full((FFT_WIDTH, D_MODEL)), full((1, D_MODEL)),
                  full((D_MODEL, D_MODEL)), mem_blk, mem_blk, full((D_MODEL, D_MODEL)), full((1, D_MODEL)),
                  full((N_EXPERTS, D_MODEL)), full((N_EXPERTS, D_MODEL)), full((N_EXPERTS, 128)),
                  full((TOK_TILE, TOK_TILE)), full((N_EXPERTS, N_EXPERTS))],
        out_specs=[tile_d, tok_tiles, tok4, tok4, per_tile],
        out_shape=[jax.ShapeDtypeStruct((bsz, seq, D_MODEL), F32),
                   jax.ShapeDtypeStruct((t * TILE_SUBLANES, LANES), F32),
                   jax.ShapeDtypeStruct((n_tiles, TOP_K, TOK_TILE), F32),
                   jax.ShapeDtypeStruct((n_tiles, TOP_K, TOK_TILE), I32),
                   jax.ShapeDtypeStruct((n_tiles, N_EXPERTS, LANES), F32)],
        compiler_params=pltpu.CompilerParams(dimension_semantics=("parallel", "parallel"),
                                             vmem_limit_bytes=VMEM_LIMIT),
        name="post",
    )(x3, bg, cv, cv, cv, yf, conv_w, gc, gf, wo_top, wo_bot, nx, wq, kmem, vmem_, wo, nf, wrh, wrl, br, tri, tri32)


def _plan_kernel(cnt_ref, src_ref, dst_ref, be_ref, nu_ref, wnext_ref, wpar_ref, off_ref, nxt_ref, *, n_tiles):
    n_sched = be_ref.shape[0]
    dump0 = n_tiles * TILE_OCT

    def fill_pad(lo, hi):
        def body(p, c):
            o = p & (OCT_PER_BLOCK - 1)
            parity = lax.shift_right_logical(p, OCT_PER_BLOCK.bit_length() - 1) & 1
            src_ref[p] = dump0 + 2 * OCT_PER_BLOCK + o
            dst_ref[p] = dump0 + parity * OCT_PER_BLOCK + o
            return c
        lax.fori_loop(lo, hi, body, 0)

    def clear(tile, c):
        off_ref[tile] = 0
        return c

    lax.fori_loop(0, n_tiles, clear, 0)

    def find_next(i, nxt):
        e = N_EXPERTS - 1 - i
        nxt_ref[e] = nxt
        total = lax.fori_loop(0, n_tiles, lambda tile, s: s + cnt_ref[tile * N_EXPERTS + e], jnp.int32(0))
        return jnp.where(total > 0, e, nxt)

    lax.fori_loop(0, N_EXPERTS, find_next, jnp.int32(-1))
    fill_pad(0, OCT_PER_BLOCK)

    def per_expert(e, carry):
        pos0, blk0, group = carry

        def per_tile(tile, pos):
            n_oct = lax.shift_right_logical(cnt_ref[tile * N_EXPERTS + e] + (OCT_ROWS - 1), OCT_ROWS.bit_length() - 1)
            base = tile * TILE_OCT + off_ref[tile]
            off_ref[tile] = off_ref[tile] + n_oct

            def per_group(grp, c):
                for u in range(PLAN_UNROLL):
                    o = grp * PLAN_UNROLL + u
                    src_ref[pos + o] = base + o
                    dst_ref[pos + o] = base + o
                return c

            per_group(0, 0)
            per_group(1, 0)
            lax.fori_loop(2, lax.shift_right_logical(n_oct + (PLAN_UNROLL - 1), PLAN_UNROLL.bit_length() - 1),
                          per_group, 0)
            return pos + n_oct

        pos1 = lax.fori_loop(0, n_tiles, per_tile, pos0)
        nb = lax.shift_right_logical(pos1 - pos0 + (OCT_PER_BLOCK - 1), OCT_PER_BLOCK.bit_length() - 1)
        pos2 = pos0 + nb * OCT_PER_BLOCK
        fill_pad(pos1, pos2)

        def fill(j, c):
            be_ref[blk0 + j] = e
            wnext_ref[blk0 + j] = nxt_ref[e]
            wpar_ref[blk0 + j] = group & 1
            return c

        lax.fori_loop(0, nb, fill, 0)
        return pos2, blk0 + nb, group + jnp.where(nb > 0, 1, 0)

    pos, n_used, _ = lax.fori_loop(0, N_EXPERTS, per_expert,
                                   (jnp.int32(OCT_PER_BLOCK), jnp.int32(0), jnp.int32(0)))
    nu_ref[0] = n_used
    tail_e = be_ref[n_used - 1]

    def tail(j, c):
        be_ref[j] = tail_e
        wnext_ref[j] = -1
        wpar_ref[j] = 0
        return c

    lax.fori_loop(n_used, n_sched, tail, 0)
    fill_pad(pos, src_ref.shape[0])


def _plan(counts, n_tiles, n_blocks):
    smem = pl.BlockSpec(memory_space=pltpu.SMEM)
    grid_spec = pltpu.PrefetchScalarGridSpec(
        num_scalar_prefetch=1,
        grid=(1,),
        in_specs=[],
        out_specs=[smem] * 6,
        scratch_shapes=[pltpu.SMEM((n_tiles,), I32), pltpu.SMEM((N_EXPERTS,), I32)],
    )
    octs = jax.ShapeDtypeStruct(((n_blocks + 2) * OCT_PER_BLOCK,), I32)
    sched = jax.ShapeDtypeStruct((n_blocks + 1,), I32)
    return pl.pallas_call(
        functools.partial(_plan_kernel, n_tiles=n_tiles),
        grid_spec=grid_spec,
        out_shape=[octs, octs, sched, jax.ShapeDtypeStruct((1,), I32), sched, sched],
        compiler_params=pltpu.CompilerParams(dimension_semantics=("arbitrary",), vmem_limit_bytes=VMEM_LIMIT),
        name="plan",
    )(counts)


def _localsort_kernel(q_ref, h3_ref, xs_ref):
    xs_ref[...] = jnp.zeros_like(xs_ref)

    @pl.when(pl.program_id(0) < pl.num_programs(0) - 1)
    def _():
        def group(g, c):
            for u in range(TILE_SUBLANES):
                t = g * TILE_SUBLANES + u
                row = _tile_rows(h3_ref, t)[...]
                for k in range(TOP_K):
                    _tile_at(xs_ref, q_ref[0, t * TOP_K + k])[...] = row
            return c

        lax.fori_loop(0, TOK_TILE // TILE_SUBLANES, group, 0)


def _localsort(q_tiles, h3t):
    n_tiles = q_tiles.shape[0]
    last = n_tiles - 1
    region = TILE_OCT * OCT_ROWS * TILE_SUBLANES
    return pl.pallas_call(
        _localsort_kernel,
        grid=(n_tiles + 1,),
        in_specs=[pl.BlockSpec((None, 1, TOP_K * TOK_TILE), lambda i: (jnp.minimum(i, last), 0, 0),
                               memory_space=pltpu.SMEM),
                  pl.BlockSpec((TOK_TILE * TILE_SUBLANES, LANES), lambda i: (jnp.minimum(i, last), 0))],
        out_specs=pl.BlockSpec((region, LANES), lambda i: (i, 0)),
        out_shape=jax.ShapeDtypeStruct(((n_tiles + 1) * region, LANES), F32),
        compiler_params=pltpu.CompilerParams(dimension_semantics=("parallel",), vmem_limit_bytes=VMEM_LIMIT),
        name="localsort",
    )(q_tiles, h3t)


def _tile_at(ref, sublane_row):
    return ref.at[pl.ds(pl.multiple_of(sublane_row, TILE_SUBLANES), TILE_SUBLANES), :]


def _tile_rows(ref, row, n_rows=1):
    return ref.at[pl.ds(pl.multiple_of(row * TILE_SUBLANES, TILE_SUBLANES), n_rows * TILE_SUBLANES), :]


def _expert_kernel(be_ref, nu_ref, wnext_ref, wpar_ref,
                   src_cur_ref, src_next_ref, dst_prev_ref, bgu_ref, bd_ref, xs_ref, wgu_hbm, wd_hbm,
                   y_ref,
                   xbuf0, xbuf1, obuf0, obuf1, wgu_f32, wd_f32, wgu_bf, wd_bf, rsem, wsem):
    j = pl.program_id(0)
    nu = nu_ref[0]
    xbuf = (xbuf0, xbuf1)
    obuf = (obuf0, obuf1)
    octet = lambda ref, o: _tile_rows(ref, o * OCT_ROWS, OCT_ROWS)
    gsem = ssem = rsem

    def gather_octet(src_ref, o, dst):
        return pltpu.make_async_copy(octet(xs_ref, src_ref[0, o]), octet(dst, o), gsem)

    def scatter_octet(dst_ref, o, src):
        return pltpu.make_async_copy(octet(src, o), octet(y_ref, dst_ref[0, o]), ssem)

    def weight_copies(e, p):
        return (pltpu.make_async_copy(wgu_hbm.at[e], wgu_f32.at[p], wsem.at[p, 0]),
                pltpu.make_async_copy(wd_hbm.at[e], wd_f32.at[p], wsem.at[p, 1]))

    def wait_gather(dst):
        for _ in range(OCT_PER_BLOCK):
            pltpu.make_async_copy(octet(xs_ref, 0), octet(dst, 0), gsem).wait()

    def wait_scatter(src):
        for _ in range(OCT_PER_BLOCK):
            pltpu.make_async_copy(octet(src, 0), octet(y_ref, 0), ssem).wait()

    @pl.when(j == 0)
    def _():
        obuf0[...] = jnp.zeros_like(obuf0)
        obuf1[...] = jnp.zeros_like(obuf1)
        first_dump = y_ref.shape[0] // (OCT_ROWS * TILE_SUBLANES) - TILE_OCT
        for c in weight_copies(be_ref[0], 0):
            c.start(priority=WEIGHT_DMA_PRIORITY)
        for o in range(OCT_PER_BLOCK):
            gather_octet(src_cur_ref, o, xbuf0).start(priority=GATHER_DMA_PRIORITY)
            pltpu.make_async_copy(octet(obuf0, o), octet(y_ref, first_dump + o), ssem).start(
                priority=SCATTER_DMA_PRIORITY)

    first_of_group = jnp.logical_or(j == 0, be_ref[j] != be_ref[jnp.maximum(j - 1, 0)])

    @pl.when(jnp.logical_and(j < nu, first_of_group))
    def _():
        p = wpar_ref[j]
        for c in weight_copies(be_ref[j], p):
            c.wait()
        wgu_bf[...] = wgu_f32[p].astype(BF16)
        wd_bf[...] = wd_f32[p].astype(BF16)

        @pl.when(wnext_ref[j] >= 0)
        def _():
            for c in weight_copies(wnext_ref[j], 1 - p):
                c.start(priority=WEIGHT_DMA_PRIORITY)

    def step(s, compute):
        wait_gather(xbuf[s])
        wait_scatter(obuf[s])

        def start_scatter():
            for o in range(OCT_PER_BLOCK):
                scatter_octet(dst_prev_ref, o, obuf[1 - s]).start(priority=SCATTER_DMA_PRIORITY)

        if compute:
            for o in range(OCT_PER_BLOCK):
                gather_octet(src_next_ref, o, xbuf[1 - s]).start(priority=GATHER_DMA_PRIORITY)
            x = _load_token_tiles(xbuf[s], ROW_BLOCK).astype(BF16)
            gu = _dot(x, wgu_bf[...]) + bgu_ref[0]
            gate = jnp.minimum(gu[:, :D_EXPERT], SWIGLU_LIMIT)
            up = jnp.clip(gu[:, D_EXPERT:], -SWIGLU_LIMIT, SWIGLU_LIMIT)
            glu = gate * (1.0 / (1.0 + jnp.exp(-SWIGLU_ALPHA * gate)))
            h = ((up + 1.0) * glu).astype(BF16)
            start_scatter()
            _store_token_tiles(obuf[s], _dot(h, wd_bf[...]) + bd_ref[0], ROW_BLOCK)
        else:
            start_scatter()
            wait_scatter(obuf[1 - s])

    for s in range(2):
        parity = (j & 1) == s
        pl.when(jnp.logical_and(j < nu, parity))(functools.partial(step, s, True))
        pl.when(jnp.logical_and(j == nu, parity))(functools.partial(step, s, False))


def _experts(block_e, n_used, w_next, w_par, src3, dst3, xs, wgu, bgu, wd, bd):
    n_steps = block_e.shape[0]
    blk_rows = ROW_BLOCK * TILE_SUBLANES
    last_blk = src3.shape[0] - 1
    exp_map = lambda j, be, nu, wn, wp: (be[j], 0, 0)
    oct_spec = lambda off: pl.BlockSpec((None, 1, OCT_PER_BLOCK),
                                        lambda j, be, nu, wn, wp: (jnp.minimum(j + off, last_blk), 0, 0),
                                        memory_space=pltpu.SMEM)
    hbm = pl.BlockSpec(memory_space=pl.ANY)
    grid_spec = pltpu.PrefetchScalarGridSpec(
        num_scalar_prefetch=4,
        grid=(n_steps,),
        in_specs=[oct_spec(1), oct_spec(2), oct_spec(0),
                  pl.BlockSpec((1, 1, 2 * D_EXPERT), exp_map), pl.BlockSpec((1, 1, D_MODEL), exp_map),
                  hbm, hbm, hbm],
        out_specs=hbm,
        scratch_shapes=[pltpu.VMEM((blk_rows, LANES), F32)] * 4
        + [pltpu.VMEM((2, D_MODEL, 2 * D_EXPERT), F32), pltpu.VMEM((2, D_EXPERT, D_MODEL), F32),
           pltpu.VMEM((D_MODEL, 2 * D_EXPERT), BF16), pltpu.VMEM((D_EXPERT, D_MODEL), BF16),
           pltpu.SemaphoreType.DMA, pltpu.SemaphoreType.DMA((2, 2))],
    )
    return pl.pallas_call(
        _expert_kernel,
        grid_spec=grid_spec,
        out_shape=jax.ShapeDtypeStruct(xs.shape, F32),
        input_output_aliases={9: 0},
        compiler_params=pltpu.CompilerParams(dimension_semantics=("arbitrary",), vmem_limit_bytes=VMEM_LIMIT),
        name="experts",
    )(block_e, n_used, w_next, w_par, src3, src3, dst3, bgu, bd, xs, wgu, wd)


def _combine_kernel(q_ref, w_ref, y_ref, x2_ref, g_ref, out_ref, acc_ref):
    def group(grp, c):
        for u in range(TILE_SUBLANES):
            t = grp * TILE_SUBLANES + u
            acc = w_ref[0, t * TOP_K] * _tile_at(y_ref, q_ref[0, t * TOP_K])[...]
            for k in range(1, TOP_K):
                acc = acc + w_ref[0, t * TOP_K + k] * _tile_at(y_ref, q_ref[0, t * TOP_K + k])[...]
            _tile_rows(acc_ref, t)[...] = acc
        return c

    lax.fori_loop(0, TOK_TILE // TILE_SUBLANES, group, 0)
    out_ref[...] = _rms(x2_ref[...] + _load_token_tiles(acc_ref, TOK_TILE), g_ref[...])


def _combine(q_tiles, w_tiles, ybuf, x2, g):
    n_tiles = q_tiles.shape[0]
    region = TILE_OCT * OCT_ROWS * TILE_SUBLANES
    tok4 = pl.BlockSpec((None, 1, TOP_K * TOK_TILE), lambda i: (i, 0, 0), memory_space=pltpu.SMEM)
    return pl.pallas_call(
        _combine_kernel,
        grid=(n_tiles,),
        in_specs=[tok4, tok4,
                  pl.BlockSpec((region, LANES), lambda i: (i, 0)),
                  pl.BlockSpec((TOK_TILE, D_MODEL), lambda i: (i, 0)),
                  pl.BlockSpec((1, D_MODEL), lambda i: (0, 0))],
        out_specs=pl.BlockSpec((TOK_TILE, D_MODEL), lambda i: (i, 0)),
        out_shape=jax.ShapeDtypeStruct(x2.shape, F32),
        scratch_shapes=[pltpu.VMEM((TOK_TILE * TILE_SUBLANES, LANES), F32)],
        compiler_params=pltpu.CompilerParams(dimension_semantics=("parallel",), vmem_limit_bytes=VMEM_LIMIT),
        name="combine",
    )(q_tiles, w_tiles, ybuf, x2, g)


def _layer(x, mem, norm_mix, w_in, conv_w, g_conv_out, g_fft_out, w_out, norm_xattn, norm_mem,
           w_q, w_k, w_v, w_o, norm_ffn, w_router, b_router, w_gate_up, b_gate_up, w_down, b_down, tables):
    bsz, seq, d = x.shape
    t = bsz * seq
    cs, m1, m2 = tables
    row = lambda v: v.reshape(1, -1)

    kmem, vmem_ = _memkv(mem, row(norm_mem), w_k.astype(BF16), w_v.astype(BF16))

    bg, cv, a, s = _inproj(x.reshape(t, d), row(norm_mix), w_in.astype(BF16), cs)
    cols = FFT_N2 * FFT_WIDTH
    g = _fft1(a.reshape(bsz, FFT_N1, cols), s.reshape(bsz, FFT_N1, cols), m1)
    yf = _fft2(g.reshape(bsz, 2, FFT_N1, FFT_N2, FFT_WIDTH), m2).reshape(bsz, seq, FFT_WIDTH)

    w_out_b = w_out.astype(BF16)
    wr_t = w_router.T
    wr_hi = wr_t.astype(BF16)
    wr_lo = (wr_t - wr_hi.astype(F32)).astype(BF16)
    tri = (jnp.arange(TOK_TILE)[:, None] < jnp.arange(TOK_TILE)[None, :]).astype(BF16)
    tri32 = (jnp.arange(N_EXPERTS)[None, :] < jnp.arange(N_EXPERTS)[:, None]).astype(BF16)
    x2, h3, w_tiles, q_tiles, cnt = _post(
        x, bg.reshape(bsz, seq, CONV_WIDTH), cv.reshape(bsz, seq, CONV_WIDTH), yf,
        conv_w, row(g_conv_out), row(g_fft_out), w_out_b[:CONV_WIDTH], w_out_b[CONV_WIDTH:],
        row(norm_xattn), w_q.astype(BF16), kmem, vmem_, w_o.astype(BF16), row(norm_ffn),
        wr_hi, wr_lo, jnp.broadcast_to(b_router[:, None], (N_EXPERTS, 128)), tri, tri32)

    n_tiles = t // TOK_TILE
    n_blocks = (t * TOP_K + n_tiles * N_EXPERTS * (OCT_ROWS - 1)) // ROW_BLOCK + N_EXPERTS
    counts = cnt[:, :, 0].astype(I32).reshape(-1)
    src, dst, block_e, n_used, w_next, w_par = _plan(counts, n_tiles, n_blocks)
    per_block = lambda v: v.reshape(n_blocks + 2, 1, OCT_PER_BLOCK)
    by_token = lambda v: v.transpose(0, 2, 1).reshape(n_tiles, 1, TOK_TILE * TOP_K)
    q_tiles, w_tiles = by_token(q_tiles), by_token(w_tiles)
    xs = _localsort(q_tiles, h3)
    ybuf = _experts(block_e, n_used, w_next, w_par, per_block(src), per_block(dst), xs,
                    w_gate_up, b_gate_up[:, None, :], w_down, b_down[:, None, :])
    return x2.reshape(t, d), ybuf, q_tiles, w_tiles


def kernel(x, mem, norm_mix, w_in, conv_w, g_conv_out, g_fft_out, w_out, norm_xattn, norm_mem, w_q, w_k, w_v, w_o,
           norm_ffn, w_router, b_router, w_gate_up, b_gate_up, w_down, b_down, norm_final):
    bsz, seq, d = x.shape
    depth = norm_mix.shape[0]
    assert depth == 1, "final norm is fused into the combine step of the single layer"
    tables = _dft_tables(seq)
    x2, ybuf, q_tiles, w_tiles = _layer(
        x, mem, norm_mix[0], w_in[0], conv_w[0], g_conv_out[0], g_fft_out[0], w_out[0], norm_xattn[0],
        norm_mem[0], w_q[0], w_k[0], w_v[0], w_o[0], norm_ffn[0], w_router[0], b_router[0],
        w_gate_up[0], b_gate_up[0], w_down[0], b_down[0], tables)
    out = _combine(q_tiles, w_tiles, ybuf, x2, norm_final.reshape(1, -1))
    return out.reshape(bsz, seq, d)
```

```python
import functools
import math

import numpy as np
import jax
import jax.numpy as jnp
from jax import lax
from jax.experimental import pallas as pl
from jax.experimental.pallas import tpu as pltpu

F32 = jnp.float32
BF16 = jnp.bfloat16
I32 = jnp.int32

D_MODEL = 1024
CONV_WIDTH = 512
FFT_WIDTH = 512
GROUP_DIM = 64
IN_PROJ_WIDTH = 3 * CONV_WIDTH + FFT_WIDTH
MEM_LEN = 256
XATTN_HEADS = 4
XATTN_HEAD_DIM = D_MODEL // XATTN_HEADS
N_EXPERTS = 32
TOP_K = 4
D_EXPERT = D_MODEL
SWIGLU_LIMIT = 7.0
SWIGLU_ALPHA = 1.702
EPS = 1e-5

FFT_N1 = 64
FFT_N2 = 128

TOK_TILE = 512
POST_SUB = 2
INPROJ_TILE = 1024
ROW_BLOCK = 512
GATHER_DMA_PRIORITY = 0
SCATTER_DMA_PRIORITY = 1
WEIGHT_DMA_PRIORITY = 1
OCT_ROWS = 8
OCT_PER_BLOCK = ROW_BLOCK // OCT_ROWS
PLAN_UNROLL = 8
TILE_OCT = (TOK_TILE * TOP_K) // OCT_ROWS + N_EXPERTS
DFT_BLOCK = 256
FFT1_LANES = 16384
FFT2_K1 = 16
BF16_SUBLANES = 16
TILE_SUBLANES = 8
LANES = 128
assert D_MODEL == TILE_SUBLANES * LANES
VMEM_LIMIT = 56 * 1024 * 1024


def _rms(x, g):
    return x * lax.rsqrt(jnp.mean(x * x, axis=-1, keepdims=True) + EPS) * g


def _dot(a, b):
    return jnp.dot(a, b, preferred_element_type=F32)


def _dot_nt(a, b):
    return lax.dot_general(a, b, (((1,), (1,)), ((), ())), preferred_element_type=F32)


def _load_token_tiles(ref, rows, base=0):
    return jnp.concatenate(
        [ref[pl.ds(base + s, rows, stride=TILE_SUBLANES), :] for s in range(TILE_SUBLANES)], axis=-1)


def _store_token_tiles(ref, val, rows, base=0):
    for s in range(TILE_SUBLANES):
        ref[pl.ds(base + s, rows, stride=TILE_SUBLANES), :] = val[:, s * LANES:(s + 1) * LANES]


def _dft_tables(seq):
    assert seq == FFT_N1 * FFT_N2
    c = np.arange(GROUP_DIM)
    ang = 2.0 * np.pi * ((c[:, None] * c[None, :]) % GROUP_DIM) / GROUP_DIM
    eye = np.eye(DFT_BLOCK // GROUP_DIM)
    cs = np.concatenate([np.kron(eye, np.cos(ang)), np.kron(eye, np.sin(ang))], axis=1) / math.sqrt(GROUP_DIM)
    n1 = np.arange(FFT_N1)
    a1 = 2.0 * np.pi * ((n1[:, None] * n1[None, :]) % FFT_N1) / FFT_N1
    c1, s1 = np.cos(a1), np.sin(a1)
    m1 = np.block([[c1, -s1], [s1, c1]]) / math.sqrt(FFT_N1)
    k1 = np.arange(FFT_N1)[:, None, None]
    k2 = np.arange(FFT_N2)[None, :, None]
    n2 = np.arange(FFT_N2)[None, None, :]
    a2 = 2.0 * np.pi * ((n2 * (k1 + FFT_N1 * k2)) % seq) / seq
    m2 = np.concatenate([np.cos(a2), -np.sin(a2)], axis=2) / math.sqrt(FFT_N2)
    return (jnp.asarray(cs, F32).astype(BF16), jnp.asarray(m1, F32).astype(BF16),
            jnp.asarray(m2, F32).astype(BF16))


def _memkv_kernel(mem_ref, g_ref, wk_ref, wv_ref, k_ref, v_ref, wk_bf, wv_bf):
    @pl.when(pl.program_id(0) == 0)
    def _():
        wk_bf[...] = wk_ref[...].astype(BF16)
        wv_bf[...] = wv_ref[...].astype(BF16)

    m = _rms(mem_ref[0], g_ref[...]).astype(BF16)
    k_ref[0] = _dot(m, wk_bf[...]).astype(BF16)
    v_ref[0] = _dot(m, wv_bf[...]).astype(BF16)


def _memkv(mem, g, wk, wv):
    bsz = mem.shape[0]
    full = lambda shape: pl.BlockSpec(shape, lambda b: (0,) * len(shape))
    per_b = pl.BlockSpec((1, MEM_LEN, D_MODEL), lambda b: (b, 0, 0))
    return pl.pallas_call(
        _memkv_kernel,
        grid=(bsz,),
        in_specs=[per_b, full((1, D_MODEL)), full((D_MODEL, D_MODEL)), full((D_MODEL, D_MODEL))],
        out_specs=[per_b, per_b],
        out_shape=[jax.ShapeDtypeStruct((bsz, MEM_LEN, D_MODEL), BF16)] * 2,
        scratch_shapes=[pltpu.VMEM((D_MODEL, D_MODEL), BF16)] * 2,
        compiler_params=pltpu.CompilerParams(dimension_semantics=("arbitrary",), vmem_limit_bytes=VMEM_LIMIT),
        name="memkv",
    )(mem, g, wk, wv)


def _inproj_kernel(x_ref, g_ref, win_ref, cs_ref, b_ref, cv_ref, a_ref, s_ref, win_bf):
    @pl.when(pl.program_id(0) == 0)
    def _():
        win_bf[...] = win_ref[...].astype(BF16)

    h = _rms(x_ref[...], g_ref[...]).astype(BF16)
    z = _dot(h, win_bf[...])
    b_ref[...] = z[:, :CONV_WIDTH].astype(BF16)
    cv_ref[...] = (z[:, CONV_WIDTH:2 * CONV_WIDTH] * z[:, 2 * CONV_WIDTH:3 * CONV_WIDTH]).astype(BF16)
    u = z[:, 3 * CONV_WIDTH:].astype(BF16)
    for blk in range(FFT_WIDTH // DFT_BLOCK):
        cols = slice(blk * DFT_BLOCK, (blk + 1) * DFT_BLOCK)
        ab = _dot(u[:, cols], cs_ref[...])
        a_ref[:, cols] = ab[:, :DFT_BLOCK].astype(BF16)
        s_ref[:, cols] = ab[:, DFT_BLOCK:].astype(BF16)


def _inproj(x2d, g, w_in, cs):
    t = x2d.shape[0]
    tile = pl.BlockSpec((INPROJ_TILE, D_MODEL), lambda i: (i, 0))
    half = pl.BlockSpec((INPROJ_TILE, CONV_WIDTH), lambda i: (i, 0))
    full = lambda shape: pl.BlockSpec(shape, lambda i: (0,) * len(shape))
    return pl.pallas_call(
        _inproj_kernel,
        grid=(t // INPROJ_TILE,),
        in_specs=[tile, full((1, D_MODEL)), full((D_MODEL, IN_PROJ_WIDTH)), full((DFT_BLOCK, 2 * DFT_BLOCK))],
        out_specs=[half] * 4,
        out_shape=[jax.ShapeDtypeStruct((t, CONV_WIDTH), BF16)] * 4,
        scratch_shapes=[pltpu.VMEM((D_MODEL, IN_PROJ_WIDTH), BF16)],
        compiler_params=pltpu.CompilerParams(dimension_semantics=("arbitrary",), vmem_limit_bytes=VMEM_LIMIT),
        name="inproj",
    )(x2d, g, w_in, cs)


def _fft1_kernel(a_ref, s_ref, m1_ref, g_ref):
    x = jnp.concatenate([a_ref[0], s_ref[0]], axis=0)
    g_ref[0] = _dot(m1_ref[...], x).astype(BF16)


def _fft1(a3, s3, m1):
    bsz, _, cols = a3.shape
    blk = pl.BlockSpec((1, FFT_N1, FFT1_LANES), lambda b, j: (b, 0, j))
    return pl.pallas_call(
        _fft1_kernel,
        grid=(bsz, cols // FFT1_LANES),
        in_specs=[blk, blk, pl.BlockSpec((2 * FFT_N1, 2 * FFT_N1), lambda b, j: (0, 0))],
        out_specs=pl.BlockSpec((1, 2 * FFT_N1, FFT1_LANES), lambda b, j: (b, 0, j)),
        out_shape=jax.ShapeDtypeStruct((bsz, 2 * FFT_N1, cols), BF16),
        compiler_params=pltpu.CompilerParams(dimension_semantics=("parallel", "parallel"),
                                             vmem_limit_bytes=VMEM_LIMIT),
        name="fft1",
    )(a3, s3, m1)


def _fft2_kernel(g_ref, m2_ref, y_ref):
    for j in range(FFT2_K1):
        x = jnp.concatenate([g_ref[0, 0, j], g_ref[0, 1, j]], axis=0)
        y_ref[0, :, j * FFT_WIDTH:(j + 1) * FFT_WIDTH] = _dot(m2_ref[j], x).astype(BF16)


def _fft2(g5, m2):
    bsz = g5.shape[0]
    return pl.pallas_call(
        _fft2_kernel,
        grid=(bsz, FFT_N1 // FFT2_K1),
        in_specs=[pl.BlockSpec((1, 2, FFT2_K1, FFT_N2, FFT_WIDTH), lambda b, j: (b, 0, j, 0, 0)),
                  pl.BlockSpec((FFT2_K1, FFT_N2, 2 * FFT_N2), lambda b, j: (j, 0, 0))],
        out_specs=pl.BlockSpec((1, FFT_N2, FFT2_K1 * FFT_WIDTH), lambda b, j: (b, 0, j)),
        out_shape=jax.ShapeDtypeStruct((bsz, FFT_N2, FFT_N1 * FFT_WIDTH), BF16),
        compiler_params=pltpu.CompilerParams(dimension_semantics=("parallel", "parallel"),
                                             vmem_limit_bytes=VMEM_LIMIT),
        name="fft2",
    )(g5, m2)


def _post_kernel(x_ref, bg_ref, cv_ref, cvp_ref, cvn_ref, yf_ref, convw_ref, gc_ref, gf_ref,
                 wot_ref, wob_ref, nx_ref, wq_ref, k_ref, v_ref, wo_ref, nf_ref,
                 wrh_ref, wrl_ref, br_ref, tri_ref, tri32_ref,
                 x2_ref, h3_ref, w_ref, q_ref, cnt_ref):
    tiles = [_post_tile(sub, x_ref, bg_ref, cv_ref, cvp_ref, cvn_ref, yf_ref, convw_ref, gc_ref, gf_ref,
                        wot_ref, wob_ref, nx_ref, wq_ref, k_ref, v_ref, wo_ref, nf_ref,
                        wrh_ref, wrl_ref, br_ref, tri_ref, tri32_ref,
                        x2_ref, h3_ref, w_ref, q_ref, cnt_ref) for sub in range(POST_SUB)]
    running = []
    while tiles or running:
        if tiles:
            running.append(tiles.pop(0))
        for tile in list(running):
            if next(tile, "done") == "done":
                running.remove(tile)


def _post_tile(sub, x_ref, bg_ref, cv_ref, cvp_ref, cvn_ref, yf_ref, convw_ref, gc_ref, gf_ref,
               wot_ref, wob_ref, nx_ref, wq_ref, k_ref, v_ref, wo_ref, nf_ref,
               wrh_ref, wrl_ref, br_ref, tri_ref, tri32_ref,
               x2_ref, h3_ref, w_ref, q_ref, cnt_ref):
    i = pl.program_id(1)
    last = pl.num_programs(1) - 1
    r0 = sub * TOK_TILE
    tile = slice(r0, r0 + TOK_TILE)

    cv = cv_ref[0, tile].astype(F32)
    if sub == 0:
        prev_row = jnp.where(i > 0, cvp_ref[0].astype(F32)[BF16_SUBLANES - 1:BF16_SUBLANES, :], 0.0)
    else:
        prev_row = cv_ref[0, r0 - BF16_SUBLANES:r0].astype(F32)[BF16_SUBLANES - 1:BF16_SUBLANES, :]
    if sub == POST_SUB - 1:
        next_row = jnp.where(i < last, cvn_ref[0].astype(F32)[0:1, :], 0.0)
    else:
        next_row = cv_ref[0, r0 + TOK_TILE:r0 + TOK_TILE + BF16_SUBLANES].astype(F32)[0:1, :]
    rows = lax.broadcasted_iota(I32, cv.shape, 0)
    cvm1 = jnp.where(rows == 0, prev_row, pltpu.roll(cv, 1, axis=0))
    cvp1 = jnp.where(rows == TOK_TILE - 1, next_row, pltpu.roll(cv, TOK_TILE - 1, axis=0))
    cw = convw_ref[...]
    y_conv = bg_ref[0, tile].astype(F32) * (cw[0:1] * cvm1 + cw[1:2] * cv + cw[2:3] * cvp1)
    yc_n = _rms(y_conv, gc_ref[...]).astype(BF16)
    yf_n = _rms(yf_ref[0, tile].astype(F32), gf_ref[...]).astype(BF16)
    yield
    x1 = x_ref[0, tile] + _dot(yc_n, wot_ref[...]) + _dot(yf_n, wob_ref[...])
    yield

    h2 = _rms(x1, nx_ref[...]).astype(BF16)
    yield
    q = _dot(h2, wq_ref[...]).astype(BF16)
    yield
    heads = []
    for hd in range(XATTN_HEADS):
        sl = slice(hd * XATTN_HEAD_DIM, (hd + 1) * XATTN_HEAD_DIM)
        s = _dot_nt(q[:, sl], k_ref[0, :, sl]) * (XATTN_HEAD_DIM ** -0.5)
        yield
        e = jnp.exp(s - jnp.max(s, axis=-1, keepdims=True))
        p = e * (1.0 / jnp.sum(e, axis=-1, keepdims=True))
        yield
        heads.append(_dot(p.astype(BF16), v_ref[0, :, sl]).astype(BF16))
    yield
    x2 = x1 + _dot(jnp.concatenate(heads, axis=-1), wo_ref[...])
    x2_ref[0, tile] = x2
    yield

    h3 = _rms(x2, nf_ref[...])
    _store_token_tiles(h3_ref, h3, TOK_TILE, base=r0 * TILE_SUBLANES)
    h3h = h3.astype(BF16)
    h3l = (h3 - h3h.astype(F32)).astype(BF16)
    yield
    logits = (_dot_nt(wrh_ref[...], h3h) + _dot_nt(wrh_ref[...], h3l) + _dot_nt(wrl_ref[...], h3h)
              + br_ref[:, 0:1])
    yield
    eidx = lax.broadcasted_iota(I32, logits.shape, 0)
    work = logits
    vals, idxs = [], []
    for _ in range(TOP_K):
        m = jnp.max(work, axis=0, keepdims=True)
        ik = jnp.min(jnp.where(work == m, eidx, N_EXPERTS), axis=0, keepdims=True)
        vals.append(m)
        idxs.append(ik)
        work = jnp.where(eidx == ik, -jnp.inf, work)
    ex = [jnp.exp(v - vals[0]) for v in vals]
    inv_den = 1.0 / (ex[0] + ex[1] + ex[2] + ex[3])
    w_ref[sub] = jnp.concatenate([e * inv_den for e in ex], axis=0)

    sel = jnp.zeros(logits.shape, F32)
    for ik in idxs:
        sel = sel + jnp.where(eidx == ik, 1.0, 0.0)
    cnt = jnp.broadcast_to(jnp.sum(sel, axis=1, keepdims=True), (N_EXPERTS, LANES))
    cnt_ref[sub] = cnt
    seg_rows = jnp.floor((cnt + (OCT_ROWS - 1)) * (1.0 / OCT_ROWS)) * OCT_ROWS
    seg_start = _dot(tri32_ref[...], seg_rows.astype(BF16))
    pos = seg_start[:, 0:1] + _dot(sel.astype(BF16), tri_ref[...])
    q_ref[sub] = jnp.concatenate(
        [jnp.sum(jnp.where(eidx == ik, pos, 0.0), axis=0, keepdims=True) for ik in idxs],
        axis=0).astype(I32) * TILE_SUBLANES


def _post(x3, bg, cv, yf, conv_w, gc, gf, wo_top, wo_bot, nx, wq, kmem, vmem_, wo, nf, wrh, wrl, br, tri, tri32):
    bsz, seq, _ = x3.shape
    step_rows = POST_SUB * TOK_TILE
    nt = seq // step_rows
    t = bsz * seq
    n_tiles = t // TOK_TILE
    halo_per_tile = step_rows // BF16_SUBLANES
    n_halo = seq // BF16_SUBLANES
    full = lambda shape: pl.BlockSpec(shape, lambda b, i: (0,) * len(shape))
    tile_d = pl.BlockSpec((1, step_rows, D_MODEL), lambda b, i: (b, i, 0))
    tile_h = pl.BlockSpec((1, step_rows, CONV_WIDTH), lambda b, i: (b, i, 0))
    halo_prev = pl.BlockSpec((1, BF16_SUBLANES, CONV_WIDTH),
                             lambda b, i: (b, jnp.maximum(i * halo_per_tile - 1, 0), 0))
    halo_next = pl.BlockSpec((1, BF16_SUBLANES, CONV_WIDTH),
                             lambda b, i: (b, jnp.minimum((i + 1) * halo_per_tile, n_halo - 1), 0))
    mem_blk = pl.BlockSpec((1, MEM_LEN, D_MODEL), lambda b, i: (b, 0, 0))
    tok4 = pl.BlockSpec((POST_SUB, TOP_K, TOK_TILE), lambda b, i: (b * nt + i, 0, 0))
    per_tile = pl.BlockSpec((POST_SUB, N_EXPERTS, LANES), lambda b, i: (b * nt + i, 0, 0))
    tok_tiles = pl.BlockSpec((step_rows * TILE_SUBLANES, LANES), lambda b, i: (b * nt + i, 0))
    return pl.pallas_call(
        _post_kernel,
        grid=(bsz, nt),
        in_specs=[tile_d, tile_h, tile_h, halo_prev, halo_next, tile_h,
                  full((3, CONV_WIDTH)), full((1, CONV_WIDTH)), full((1, FFT_WIDTH)),
                  full((CONV_WIDTH, D_MODEL)), full((FFT_WIDTH, D_MODEL)), full((1, D_MODEL)),
                  full((D_MODEL, D_MODEL)), mem_blk, mem_blk, full((D_MODEL, D_MODEL)), full((1, D_MODEL)),
                  full((N_EXPERTS, D_MODEL)), full((N_EXPERTS, D_MODEL)), full((N_EXPERTS, 128)),
                  full((TOK_TILE, TOK_TILE)), full((N_EXPERTS, N_EXPERTS))],
        out_specs=[tile_d, tok_tiles, tok4, tok4, per_tile],
        out_shape=[jax.ShapeDtypeStruct((bsz, seq, D_MODEL), F32),
                   jax.ShapeDtypeStruct((t * TILE_SUBLANES, LANES), F32),
                   jax.ShapeDtypeStruct((n_tiles, TOP_K, TOK_TILE), F32),
                   jax.ShapeDtypeStruct((n_tiles, TOP_K, TOK_TILE), I32),
                   jax.ShapeDtypeStruct((n_tiles, N_EXPERTS, LANES), F32)],
        compiler_params=pltpu.CompilerParams(dimension_semantics=("parallel", "parallel"),
                                             vmem_limit_bytes=VMEM_LIMIT),
        name="post",
    )(x3, bg, cv, cv, cv, yf, conv_w, gc, gf, wo_top, wo_bot, nx, wq, kmem, vmem_, wo, nf, wrh, wrl, br, tri, tri32)


def _plan_kernel(cnt_ref, src_ref, dst_ref, be_ref, nu_ref, wnext_ref, wpar_ref, off_ref, nxt_ref, *, n_tiles):
    n_sched = be_ref.shape[0]
    dump0 = n_tiles * TILE_OCT

    def fill_pad(lo, hi):
        def body(p, c):
            o = p & (OCT_PER_BLOCK - 1)
            parity = lax.shift_right_logical(p, OCT_PER_BLOCK.bit_length() - 1) & 1
            src_ref[p] = dump0 + 2 * OCT_PER_BLOCK + o
            dst_ref[p] = dump0 + parity * OCT_PER_BLOCK + o
            return c
        lax.fori_loop(lo, hi, body, 0)

    def clear(tile, c):
        off_ref[tile] = 0
        return c

    lax.fori_loop(0, n_tiles, clear, 0)

    def find_next(i, nxt):
        e = N_EXPERTS - 1 - i
        nxt_ref[e] = nxt
        total = lax.fori_loop(0, n_tiles, lambda tile, s: s + cnt_ref[tile * N_EXPERTS + e], jnp.int32(0))
        return jnp.where(total > 0, e, nxt)

    lax.fori_loop(0, N_EXPERTS, find_next, jnp.int32(-1))
    fill_pad(0, OCT_PER_BLOCK)

    def per_expert(e, carry):
        pos0, blk0, group = carry

        def per_tile(tile, pos):
            n_oct = lax.shift_right_logical(cnt_ref[tile * N_EXPERTS + e] + (OCT_ROWS - 1), OCT_ROWS.bit_length() - 1)
            base = tile * TILE_OCT + off_ref[tile]
            off_ref[tile] = off_ref[tile] + n_oct

            def per_group(grp, c):
                for u in range(PLAN_UNROLL):
                    o = grp * PLAN_UNROLL + u
                    src_ref[pos + o] = base + o
                    dst_ref[pos + o] = base + o
                return c

            per_group(0, 0)
            per_group(1, 0)
            lax.fori_loop(2, lax.shift_right_logical(n_oct + (PLAN_UNROLL - 1), PLAN_UNROLL.bit_length() - 1),
                          per_group, 0)
            return pos + n_oct

        pos1 = lax.fori_loop(0, n_tiles, per_tile, pos0)
        nb = lax.shift_right_logical(pos1 - pos0 + (OCT_PER_BLOCK - 1), OCT_PER_BLOCK.bit_length() - 1)
        pos2 = pos0 + nb * OCT_PER_BLOCK
        fill_pad(pos1, pos2)

        def fill(j, c):
            be_ref[blk0 + j] = e
            wnext_ref[blk0 + j] = nxt_ref[e]
            wpar_ref[blk0 + j] = group & 1
            return c

        lax.fori_loop(0, nb, fill, 0)
        return pos2, blk0 + nb, group + jnp.where(nb > 0, 1, 0)

    pos, n_used, _ = lax.fori_loop(0, N_EXPERTS, per_expert,
                                   (jnp.int32(OCT_PER_BLOCK), jnp.int32(0), jnp.int32(0)))
    nu_ref[0] = n_used
    tail_e = be_ref[n_used - 1]

    def tail(j, c):
        be_ref[j] = tail_e
        wnext_ref[j] = -1
        wpar_ref[j] = 0
        return c

    lax.fori_loop(n_used, n_sched, tail, 0)
    fill_pad(pos, src_ref.shape[0])


def _plan(counts, n_tiles, n_blocks):
    smem = pl.BlockSpec(memory_space=pltpu.SMEM)
    grid_spec = pltpu.PrefetchScalarGridSpec(
        num_scalar_prefetch=1,
        grid=(1,),
        in_specs=[],
        out_specs=[smem] * 6,
        scratch_shapes=[pltpu.SMEM((n_tiles,), I32), pltpu.SMEM((N_EXPERTS,), I32)],
    )
    octs = jax.ShapeDtypeStruct(((n_blocks + 2) * OCT_PER_BLOCK,), I32)
    sched = jax.ShapeDtypeStruct((n_blocks + 1,), I32)
    return pl.pallas_call(
        functools.partial(_plan_kernel, n_tiles=n_tiles),
        grid_spec=grid_spec,
        out_shape=[octs, octs, sched, jax.ShapeDtypeStruct((1,), I32), sched, sched],
        compiler_params=pltpu.CompilerParams(dimension_semantics=("arbitrary",), vmem_limit_bytes=VMEM_LIMIT),
        name="plan",
    )(counts)


def _localsort_kernel(q_ref, h3_ref, xs_ref):
    xs_ref[...] = jnp.zeros_like(xs_ref)

    @pl.when(pl.program_id(0) < pl.num_programs(0) - 1)
    def _():
        def group(g, c):
            for u in range(TILE_SUBLANES):
                t = g * TILE_SUBLANES + u
                row = _tile_rows(h3_ref, t)[...]
                for k in range(TOP_K):
                    _tile_at(xs_ref, q_ref[0, t * TOP_K + k])[...] = row
            return c

        lax.fori_loop(0, TOK_TILE // TILE_SUBLANES, group, 0)


def _localsort(q_tiles, h3t):
    n_tiles = q_tiles.shape[0]
    last = n_tiles - 1
    region = TILE_OCT * OCT_ROWS * TILE_SUBLANES
    return pl.pallas_call(
        _localsort_kernel,
        grid=(n_tiles + 1,),
        in_specs=[pl.BlockSpec((None, 1, TOP_K * TOK_TILE), lambda i: (jnp.minimum(i, last), 0, 0),
                               memory_space=pltpu.SMEM),
                  pl.BlockSpec((TOK_TILE * TILE_SUBLANES, LANES), lambda i: (jnp.minimum(i, last), 0))],
        out_specs=pl.BlockSpec((region, LANES), lambda i: (i, 0)),
        out_shape=jax.ShapeDtypeStruct(((n_tiles + 1) * region, LANES), F32),
        compiler_params=pltpu.CompilerParams(dimension_semantics=("parallel",), vmem_limit_bytes=VMEM_LIMIT),
        name="localsort",
    )(q_tiles, h3t)


def _tile_at(ref, sublane_row):
    return ref.at[pl.ds(pl.multiple_of(sublane_row, TILE_SUBLANES), TILE_SUBLANES), :]


def _tile_rows(ref, row, n_rows=1):
    return ref.at[pl.ds(pl.multiple_of(row * TILE_SUBLANES, TILE_SUBLANES), n_rows * TILE_SUBLANES), :]


def _expert_kernel(be_ref, nu_ref, wnext_ref, wpar_ref,
                   src_cur_ref, src_next_ref, dst_prev_ref, bgu_ref, bd_ref, xs_ref, wgu_hbm, wd_hbm,
                   y_ref,
                   xbuf0, xbuf1, obuf0, obuf1, wgu_f32, wd_f32, wgu_bf, wd_bf, gsem, ssem, wsem):
    j = pl.program_id(0)
    nu = nu_ref[0]
    xbuf = (xbuf0, xbuf1)
    obuf = (obuf0, obuf1)
    octet = lambda ref, o: _tile_rows(ref, o * OCT_ROWS, OCT_ROWS)

    def gather_octet(src_ref, o, dst):
        return pltpu.make_async_copy(octet(xs_ref, src_ref[0, o]), octet(dst, o), gsem)

    def scatter_octet(dst_ref, o, src):
        return pltpu.make_async_copy(octet(src, o), octet(y_ref, dst_ref[0, o]), ssem)

    def weight_copies(e, p):
        return (pltpu.make_async_copy(wgu_hbm.at[e], wgu_f32.at[p], wsem.at[p, 0]),
                pltpu.make_async_copy(wd_hbm.at[e], wd_f32.at[p], wsem.at[p, 1]))

    def wait_gather(dst):
        for _ in range(OCT_PER_BLOCK):
            pltpu.make_async_copy(octet(xs_ref, 0), octet(dst, 0), gsem).wait()

    def wait_scatter(src):
        for _ in range(OCT_PER_BLOCK):
            pltpu.make_async_copy(octet(src, 0), octet(y_ref, 0), ssem).wait()

    @pl.when(j == 0)
    def _():
        obuf1[...] = jnp.zeros_like(obuf1)
        for c in weight_copies(be_ref[0], 0):
            c.start(priority=WEIGHT_DMA_PRIORITY)
        for o in range(OCT_PER_BLOCK):
            gather_octet(src_cur_ref, o, xbuf0).start(priority=GATHER_DMA_PRIORITY)

    first_of_group = jnp.logical_or(j == 0, be_ref[j] != be_ref[jnp.maximum(j - 1, 0)])

    @pl.when(jnp.logical_and(j < nu, first_of_group))
    def _():
        p = wpar_ref[j]
        for c in weight_copies(be_ref[j], p):
            c.wait()
        wgu_bf[...] = wgu_f32[p].astype(BF16)
        wd_bf[...] = wd_f32[p].astype(BF16)

        @pl.when(wnext_ref[j] >= 0)
        def _():
            for c in weight_copies(wnext_ref[j], 1 - p):
                c.start(priority=WEIGHT_DMA_PRIORITY)

    def step(s, compute):
        wait_gather(xbuf[s])

        @pl.when(j >= 1)
        def _():
            wait_scatter(obuf[s])

        def start_scatter():
            for o in range(OCT_PER_BLOCK):
                scatter_octet(dst_prev_ref, o, obuf[1 - s]).start(priority=SCATTER_DMA_PRIORITY)

        if compute:
            for o in range(OCT_PER_BLOCK):
                gather_octet(src_next_ref, o, xbuf[1 - s]).start(priority=GATHER_DMA_PRIORITY)
            x = _load_token_tiles(xbuf[s], ROW_BLOCK).astype(BF16)
            gu = _dot(x, wgu_bf[...]) + bgu_ref[0]
            gate = jnp.minimum(gu[:, :D_EXPERT], SWIGLU_LIMIT)
            up = jnp.clip(gu[:, D_EXPERT:], -SWIGLU_LIMIT, SWIGLU_LIMIT)
            glu = gate * (1.0 / (1.0 + jnp.exp(-SWIGLU_ALPHA * gate)))
            h = ((up + 1.0) * glu).astype(BF16)
            start_scatter()
            _store_token_tiles(obuf[s], _dot(h, wd_bf[...]) + bd_ref[0], ROW_BLOCK)
        else:
            start_scatter()
            wait_scatter(obuf[1 - s])

    for s in range(2):
        parity = (j & 1) == s
        pl.when(jnp.logical_and(j < nu, parity))(functools.partial(step, s, True))
        pl.when(jnp.logical_and(j == nu, parity))(functools.partial(step, s, False))


def _experts(block_e, n_used, w_next, w_par, src3, dst3, xs, wgu, bgu, wd, bd):
    n_steps = block_e.shape[0]
    blk_rows = ROW_BLOCK * TILE_SUBLANES
    last_blk = src3.shape[0] - 1
    exp_map = lambda j, be, nu, wn, wp: (be[j], 0, 0)
    oct_spec = lambda off: pl.BlockSpec((None, 1, OCT_PER_BLOCK),
                                        lambda j, be, nu, wn, wp: (jnp.minimum(j + off, last_blk), 0, 0),
                                        memory_space=pltpu.SMEM)
    hbm = pl.BlockSpec(memory_space=pl.ANY)
    grid_spec = pltpu.PrefetchScalarGridSpec(
        num_scalar_prefetch=4,
        grid=(n_steps,),
        in_specs=[oct_spec(1), oct_spec(2), oct_spec(0),
                  pl.BlockSpec((1, 1, 2 * D_EXPERT), exp_map), pl.BlockSpec((1, 1, D_MODEL), exp_map),
                  hbm, hbm, hbm],
        out_specs=hbm,
        scratch_shapes=[pltpu.VMEM((blk_rows, LANES), F32)] * 4
        + [pltpu.VMEM((2, D_MODEL, 2 * D_EXPERT), F32), pltpu.VMEM((2, D_EXPERT, D_MODEL), F32),
           pltpu.VMEM((D_MODEL, 2 * D_EXPERT), BF16), pltpu.VMEM((D_EXPERT, D_MODEL), BF16),
           pltpu.SemaphoreType.DMA, pltpu.SemaphoreType.DMA, pltpu.SemaphoreType.DMA((2, 2))],
    )
    return pl.pallas_call(
        _expert_kernel,
        grid_spec=grid_spec,
        out_shape=jax.ShapeDtypeStruct(xs.shape, F32),
        input_output_aliases={9: 0},
        compiler_params=pltpu.CompilerParams(dimension_semantics=("arbitrary",), vmem_limit_bytes=VMEM_LIMIT),
        name="experts",
    )(block_e, n_used, w_next, w_par, src3, src3, dst3, bgu, bd, xs, wgu, wd)


def _combine_kernel(q_ref, w_ref, y_ref, x2_ref, g_ref, out_ref, acc_ref):
    def group(grp, c):
        for u in range(TILE_SUBLANES):
            t = grp * TILE_SUBLANES + u
            acc = w_ref[0, t * TOP_K] * _tile_at(y_ref, q_ref[0, t * TOP_K])[...]
            for k in range(1, TOP_K):
                acc = acc + w_ref[0, t * TOP_K + k] * _tile_at(y_ref, q_ref[0, t * TOP_K + k])[...]
            _tile_rows(acc_ref, t)[...] = acc
        return c

    lax.fori_loop(0, TOK_TILE // TILE_SUBLANES, group, 0)
    out_ref[...] = _rms(x2_ref[...] + _load_token_tiles(acc_ref, TOK_TILE), g_ref[...])


def _combine(q_tiles, w_tiles, ybuf, x2, g):
    n_tiles = q_tiles.shape[0]
    region = TILE_OCT * OCT_ROWS * TILE_SUBLANES
    tok4 = pl.BlockSpec((None, 1, TOP_K * TOK_TILE), lambda i: (i, 0, 0), memory_space=pltpu.SMEM)
    return pl.pallas_call(
        _combine_kernel,
        grid=(n_tiles,),
        in_specs=[tok4, tok4,
                  pl.BlockSpec((region, LANES), lambda i: (i, 0)),
                  pl.BlockSpec((TOK_TILE, D_MODEL), lambda i: (i, 0)),
                  pl.BlockSpec((1, D_MODEL), lambda i: (0, 0))],
        out_specs=pl.BlockSpec((TOK_TILE, D_MODEL), lambda i: (i, 0)),
        out_shape=jax.ShapeDtypeStruct(x2.shape, F32),
        scratch_shapes=[pltpu.VMEM((TOK_TILE * TILE_SUBLANES, LANES), F32)],
        compiler_params=pltpu.CompilerParams(dimension_semantics=("parallel",), vmem_limit_bytes=VMEM_LIMIT),
        name="combine",
    )(q_tiles, w_tiles, ybuf, x2, g)


def _layer(x, mem, norm_mix, w_in, conv_w, g_conv_out, g_fft_out, w_out, norm_xattn, norm_mem,
           w_q, w_k, w_v, w_o, norm_ffn, w_router, b_router, w_gate_up, b_gate_up, w_down, b_down, tables):
    bsz, seq, d = x.shape
    t = bsz * seq
    cs, m1, m2 = tables
    row = lambda v: v.reshape(1, -1)

    kmem, vmem_ = _memkv(mem, row(norm_mem), w_k, w_v)

    bg, cv, a, s = _inproj(x.reshape(t, d), row(norm_mix), w_in, cs)
    cols = FFT_N2 * FFT_WIDTH
    g = _fft1(a.reshape(bsz, FFT_N1, cols), s.reshape(bsz, FFT_N1, cols), m1)
    yf = _fft2(g.reshape(bsz, 2, FFT_N1, FFT_N2, FFT_WIDTH), m2).reshape(bsz, seq, FFT_WIDTH)

    w_out_b = w_out.astype(BF16)
    wr_t = w_router.T
    wr_hi = wr_t.astype(BF16)
    wr_lo = (wr_t - wr_hi.astype(F32)).astype(BF16)
    tri = (jnp.arange(TOK_TILE)[:, None] < jnp.arange(TOK_TILE)[None, :]).astype(BF16)
    tri32 = (jnp.arange(N_EXPERTS)[None, :] < jnp.arange(N_EXPERTS)[:, None]).astype(BF16)
    x2, h3, w_tiles, q_tiles, cnt = _post(
        x, bg.reshape(bsz, seq, CONV_WIDTH), cv.reshape(bsz, seq, CONV_WIDTH), yf,
        conv_w, row(g_conv_out), row(g_fft_out), w_out_b[:CONV_WIDTH], w_out_b[CONV_WIDTH:],
        row(norm_xattn), w_q.astype(BF16), kmem, vmem_, w_o.astype(BF16), row(norm_ffn),
        wr_hi, wr_lo, jnp.broadcast_to(b_router[:, None], (N_EXPERTS, 128)), tri, tri32)

    n_tiles = t // TOK_TILE
    n_blocks = (t * TOP_K + n_tiles * N_EXPERTS * (OCT_ROWS - 1)) // ROW_BLOCK + N_EXPERTS
    counts = cnt[:, :, 0].astype(I32).reshape(-1)
    src, dst, block_e, n_used, w_next, w_par = _plan(counts, n_tiles, n_blocks)
    per_block = lambda v: v.reshape(n_blocks + 2, 1, OCT_PER_BLOCK)
    by_token = lambda v: v.transpose(0, 2, 1).reshape(n_tiles, 1, TOK_TILE * TOP_K)
    q_tiles, w_tiles = by_token(q_tiles), by_token(w_tiles)
    xs = _localsort(q_tiles, h3)
    ybuf = _experts(block_e, n_used, w_next, w_par, per_block(src), per_block(dst), xs,
                    w_gate_up, b_gate_up[:, None, :], w_down, b_down[:, None, :])
    return x2.reshape(t, d), ybuf, q_tiles, w_tiles


def kernel(x, mem, norm_mix, w_in, conv_w, g_conv_out, g_fft_out, w_out, norm_xattn, norm_mem, w_q, w_k, w_v, w_o,
           norm_ffn, w_router, b_router, w_gate_up, b_gate_up, w_down, b_down, norm_final):
    bsz, seq, d = x.shape
    depth = norm_mix.shape[0]
    assert depth == 1, "final norm is fused into the combine step of the single layer"
    tables = _dft_tables(seq)
    x2, ybuf, q_tiles, w_tiles = _layer(
        x, mem, norm_mix[0], w_in[0], conv_w[0], g_conv_out[0], g_fft_out[0], w_out[0], norm_xattn[0],
        norm_mem[0], w_q[0], w_k[0], w_v[0], w_o[0], norm_ffn[0], w_router[0], b_router[0],
        w_gate_up[0], b_gate_up[0], w_down[0], b_down[0], tables)
    out = _combine(q_tiles, w_tiles, ybuf, x2, norm_final.reshape(1, -1))
    return out.reshape(bsz, seq, d)
```

```python
import functools
import math

import numpy as np
import jax
import jax.numpy as jnp
from jax import lax
from jax.experimental import pallas as pl
from jax.experimental.pallas import tpu as pltpu

F32 = jnp.float32
BF16 = jnp.bfloat16
I32 = jnp.int32

D_MODEL = 1024
CONV_WIDTH = 512
FFT_WIDTH = 512
GROUP_DIM = 64
IN_PROJ_WIDTH = 3 * CONV_WIDTH + FFT_WIDTH
MEM_LEN = 256
XATTN_HEADS = 4
XATTN_HEAD_DIM = D_MODEL // XATTN_HEADS
N_EXPERTS = 32
TOP_K = 4
D_EXPERT = D_MODEL
SWIGLU_LIMIT = 7.0
SWIGLU_ALPHA = 1.702
EPS = 1e-5

FFT_N1 = 64
FFT_N2 = 128

TOK_TILE = 512
POST_SUB = 2
INPROJ_TILE = 1024
ROW_BLOCK = 512
GATHER_DMA_PRIORITY = 0
SCATTER_DMA_PRIORITY = 1
WEIGHT_DMA_PRIORITY = 1
OCT_ROWS = 8
OCT_PER_BLOCK = ROW_BLOCK // OCT_ROWS
PLAN_UNROLL = 8
TILE_OCT = (TOK_TILE * TOP_K) // OCT_ROWS + N_EXPERTS
DFT_BLOCK = 256
FFT1_LANES = 16384
FFT2_K1 = 16
BF16_SUBLANES = 16
TILE_SUBLANES = 8
LANES = 128
assert D_MODEL == TILE_SUBLANES * LANES
VMEM_LIMIT = 56 * 1024 * 1024


def _rms(x, g):
    return x * lax.rsqrt(jnp.mean(x * x, axis=-1, keepdims=True) + EPS) * g


def _dot(a, b):
    return jnp.dot(a, b, preferred_element_type=F32)


def _dot_nt(a, b):
    return lax.dot_general(a, b, (((1,), (1,)), ((), ())), preferred_element_type=F32)


def _load_token_tiles(ref, rows, base=0):
    return jnp.concatenate(
        [ref[pl.ds(base + s, rows, stride=TILE_SUBLANES), :] for s in range(TILE_SUBLANES)], axis=-1)


def _store_token_tiles(ref, val, rows, base=0):
    for s in range(TILE_SUBLANES):
        ref[pl.ds(base + s, rows, stride=TILE_SUBLANES), :] = val[:, s * LANES:(s + 1) * LANES]


def _dft_tables(seq):
    assert seq == FFT_N1 * FFT_N2
    c = np.arange(GROUP_DIM)
    ang = 2.0 * np.pi * ((c[:, None] * c[None, :]) % GROUP_DIM) / GROUP_DIM
    eye = np.eye(DFT_BLOCK // GROUP_DIM)
    cs = np.concatenate([np.kron(eye, np.cos(ang)), np.kron(eye, np.sin(ang))], axis=1) / math.sqrt(GROUP_DIM)
    n1 = np.arange(FFT_N1)
    a1 = 2.0 * np.pi * ((n1[:, None] * n1[None, :]) % FFT_N1) / FFT_N1
    c1, s1 = np.cos(a1), np.sin(a1)
    m1 = np.block([[c1, -s1], [s1, c1]]) / math.sqrt(FFT_N1)
    k1 = np.arange(FFT_N1)[:, None, None]
    k2 = np.arange(FFT_N2)[None, :, None]
    n2 = np.arange(FFT_N2)[None, None, :]
    a2 = 2.0 * np.pi * ((n2 * (k1 + FFT_N1 * k2)) % seq) / seq
    m2 = np.concatenate([np.cos(a2), -np.sin(a2)], axis=2) / math.sqrt(FFT_N2)
    return (jnp.asarray(cs, F32).astype(BF16), jnp.asarray(m1, F32).astype(BF16),
            jnp.asarray(m2, F32).astype(BF16))


def _memkv_kernel(mem_ref, g_ref, wk_ref, wv_ref, k_ref, v_ref, wk_bf, wv_bf):
    @pl.when(pl.program_id(0) == 0)
    def _():
        wk_bf[...] = wk_ref[...].astype(BF16)
        wv_bf[...] = wv_ref[...].astype(BF16)

    m = _rms(mem_ref[0], g_ref[...]).astype(BF16)
    k_ref[0] = _dot(m, wk_bf[...]).astype(BF16)
    v_ref[0] = _dot(m, wv_bf[...]).astype(BF16)


def _memkv(mem, g, wk, wv):
    bsz = mem.shape[0]
    full = lambda shape: pl.BlockSpec(shape, lambda b: (0,) * len(shape))
    per_b = pl.BlockSpec((1, MEM_LEN, D_MODEL), lambda b: (b, 0, 0))
    return pl.pallas_call(
        _memkv_kernel,
        grid=(bsz,),
        in_specs=[per_b, full((1, D_MODEL)), full((D_MODEL, D_MODEL)), full((D_MODEL, D_MODEL))],
        out_specs=[per_b, per_b],
        out_shape=[jax.ShapeDtypeStruct((bsz, MEM_LEN, D_MODEL), BF16)] * 2,
        scratch_shapes=[pltpu.VMEM((D_MODEL, D_MODEL), BF16)] * 2,
        compiler_params=pltpu.CompilerParams(dimension_semantics=("arbitrary",), vmem_limit_bytes=VMEM_LIMIT),
        name="memkv",
    )(mem, g, wk, wv)


def _inproj_kernel(x_ref, g_ref, win_ref, cs_ref, b_ref, cv_ref, a_ref, s_ref, win_bf):
    @pl.when(pl.program_id(0) == 0)
    def _():
        win_bf[...] = win_ref[...].astype(BF16)

    h = _rms(x_ref[...], g_ref[...]).astype(BF16)
    z = _dot(h, win_bf[...])
    b_ref[...] = z[:, :CONV_WIDTH].astype(BF16)
    cv_ref[...] = (z[:, CONV_WIDTH:2 * CONV_WIDTH] * z[:, 2 * CONV_WIDTH:3 * CONV_WIDTH]).astype(BF16)
    u = z[:, 3 * CONV_WIDTH:].astype(BF16)
    for blk in range(FFT_WIDTH // DFT_BLOCK):
        cols = slice(blk * DFT_BLOCK, (blk + 1) * DFT_BLOCK)
        ab = _dot(u[:, cols], cs_ref[...])
        a_ref[:, cols] = ab[:, :DFT_BLOCK].astype(BF16)
        s_ref[:, cols] = ab[:, DFT_BLOCK:].astype(BF16)


def _inproj(x2d, g, w_in, cs):
    t = x2d.shape[0]
    tile = pl.BlockSpec((INPROJ_TILE, D_MODEL), lambda i: (i, 0))
    half = pl.BlockSpec((INPROJ_TILE, CONV_WIDTH), lambda i: (i, 0))
    full = lambda shape: pl.BlockSpec(shape, lambda i: (0,) * len(shape))
    return pl.pallas_call(
        _inproj_kernel,
        grid=(t // INPROJ_TILE,),
        in_specs=[tile, full((1, D_MODEL)), full((D_MODEL, IN_PROJ_WIDTH)), full((DFT_BLOCK, 2 * DFT_BLOCK))],
        out_specs=[half] * 4,
        out_shape=[jax.ShapeDtypeStruct((t, CONV_WIDTH), BF16)] * 4,
        scratch_shapes=[pltpu.VMEM((D_MODEL, IN_PROJ_WIDTH), BF16)],
        compiler_params=pltpu.CompilerParams(dimension_semantics=("arbitrary",), vmem_limit_bytes=VMEM_LIMIT),
        name="inproj",
    )(x2d, g, w_in, cs)


def _fft1_kernel(a_ref, s_ref, m1_ref, g_ref):
    x = jnp.concatenate([a_ref[0], s_ref[0]], axis=0)
    g_ref[0] = _dot(m1_ref[...], x).astype(BF16)


def _fft1(a3, s3, m1):
    bsz, _, cols = a3.shape
    blk = pl.BlockSpec((1, FFT_N1, FFT1_LANES), lambda b, j: (b, 0, j))
    return pl.pallas_call(
        _fft1_kernel,
        grid=(bsz, cols // FFT1_LANES),
        in_specs=[blk, blk, pl.BlockSpec((2 * FFT_N1, 2 * FFT_N1), lambda b, j: (0, 0))],
        out_specs=pl.BlockSpec((1, 2 * FFT_N1, FFT1_LANES), lambda b, j: (b, 0, j)),
        out_shape=jax.ShapeDtypeStruct((bsz, 2 * FFT_N1, cols), BF16),
        compiler_params=pltpu.CompilerParams(dimension_semantics=("parallel", "parallel"),
                                             vmem_limit_bytes=VMEM_LIMIT),
        name="fft1",
    )(a3, s3, m1)


def _fft2_kernel(g_ref, m2_ref, y_ref):
    for j in range(FFT2_K1):
        x = jnp.concatenate([g_ref[0, 0, j], g_ref[0, 1, j]], axis=0)
        y_ref[0, :, j * FFT_WIDTH:(j + 1) * FFT_WIDTH] = _dot(m2_ref[j], x).astype(BF16)


def _fft2(g5, m2):
    bsz = g5.shape[0]
    return pl.pallas_call(
        _fft2_kernel,
        grid=(bsz, FFT_N1 // FFT2_K1),
        in_specs=[pl.BlockSpec((1, 2, FFT2_K1, FFT_N2, FFT_WIDTH), lambda b, j: (b, 0, j, 0, 0)),
                  pl.BlockSpec((FFT2_K1, FFT_N2, 2 * FFT_N2), lambda b, j: (j, 0, 0))],
        out_specs=pl.BlockSpec((1, FFT_N2, FFT2_K1 * FFT_WIDTH), lambda b, j: (b, 0, j)),
        out_shape=jax.ShapeDtypeStruct((bsz, FFT_N2, FFT_N1 * FFT_WIDTH), BF16),
        compiler_params=pltpu.CompilerParams(dimension_semantics=("parallel", "parallel"),
                                             vmem_limit_bytes=VMEM_LIMIT),
        name="fft2",
    )(g5, m2)


def _post_kernel(x_ref, bg_ref, cv_ref, cvp_ref, cvn_ref, yf_ref, convw_ref, gc_ref, gf_ref,
                 wot_ref, wob_ref, nx_ref, wq_ref, k_ref, v_ref, wo_ref, nf_ref,
                 wrh_ref, wrl_ref, br_ref, tri_ref, tri32_ref,
                 x2_ref, h3_ref, w_ref, q_ref, cnt_ref):
    tiles = [_post_tile(sub, x_ref, bg_ref, cv_ref, cvp_ref, cvn_ref, yf_ref, convw_ref, gc_ref, gf_ref,
                        wot_ref, wob_ref, nx_ref, wq_ref, k_ref, v_ref, wo_ref, nf_ref,
                        wrh_ref, wrl_ref, br_ref, tri_ref, tri32_ref,
                        x2_ref, h3_ref, w_ref, q_ref, cnt_ref) for sub in range(POST_SUB)]
    running = []
    while tiles or running:
        if tiles:
            running.append(tiles.pop(0))
        for tile in list(running):
            if next(tile, "done") == "done":
                running.remove(tile)


def _post_tile(sub, x_ref, bg_ref, cv_ref, cvp_ref, cvn_ref, yf_ref, convw_ref, gc_ref, gf_ref,
               wot_ref, wob_ref, nx_ref, wq_ref, k_ref, v_ref, wo_ref, nf_ref,
               wrh_ref, wrl_ref, br_ref, tri_ref, tri32_ref,
               x2_ref, h3_ref, w_ref, q_ref, cnt_ref):
    i = pl.program_id(1)
    last = pl.num_programs(1) - 1
    r0 = sub * TOK_TILE
    tile = slice(r0, r0 + TOK_TILE)

    cv = cv_ref[0, tile].astype(F32)
    if sub == 0:
        prev_row = jnp.where(i > 0, cvp_ref[0].astype(F32)[BF16_SUBLANES - 1:BF16_SUBLANES, :], 0.0)
    else:
        prev_row = cv_ref[0, r0 - BF16_SUBLANES:r0].astype(F32)[BF16_SUBLANES - 1:BF16_SUBLANES, :]
    if sub == POST_SUB - 1:
        next_row = jnp.where(i < last, cvn_ref[0].astype(F32)[0:1, :], 0.0)
    else:
        next_row = cv_ref[0, r0 + TOK_TILE:r0 + TOK_TILE + BF16_SUBLANES].astype(F32)[0:1, :]
    rows = lax.broadcasted_iota(I32, cv.shape, 0)
    cvm1 = jnp.where(rows == 0, prev_row, pltpu.roll(cv, 1, axis=0))
    cvp1 = jnp.where(rows == TOK_TILE - 1, next_row, pltpu.roll(cv, TOK_TILE - 1, axis=0))
    cw = convw_ref[...]
    y_conv = bg_ref[0, tile].astype(F32) * (cw[0:1] * cvm1 + cw[1:2] * cv + cw[2:3] * cvp1)
    yc_n = _rms(y_conv, gc_ref[...]).astype(BF16)
    yf_n = _rms(yf_ref[0, tile].astype(F32), gf_ref[...]).astype(BF16)
    yield
    x1 = x_ref[0, tile] + _dot(yc_n, wot_ref[...]) + _dot(yf_n, wob_ref[...])
    yield

    h2 = _rms(x1, nx_ref[...]).astype(BF16)
    yield
    q = _dot(h2, wq_ref[...]).astype(BF16)
    yield
    heads = []
    for hd in range(XATTN_HEADS):
        sl = slice(hd * XATTN_HEAD_DIM, (hd + 1) * XATTN_HEAD_DIM)
        s = _dot_nt(q[:, sl], k_ref[0, :, sl]) * (XATTN_HEAD_DIM ** -0.5)
        yield
        e = jnp.exp(s - jnp.max(s, axis=-1, keepdims=True))
        p = e * (1.0 / jnp.sum(e, axis=-1, keepdims=True))
        yield
        heads.append(_dot(p.astype(BF16), v_ref[0, :, sl]).astype(BF16))
    yield
    x2 = x1 + _dot(jnp.concatenate(heads, axis=-1), wo_ref[...])
    x2_ref[0, tile] = x2
    yield

    h3 = _rms(x2, nf_ref[...])
    _store_token_tiles(h3_ref, h3, TOK_TILE, base=r0 * TILE_SUBLANES)
    h3h = h3.astype(BF16)
    h3l = (h3 - h3h.astype(F32)).astype(BF16)
    yield
    logits = (_dot_nt(wrh_ref[...], h3h) + _dot_nt(wrh_ref[...], h3l) + _dot_nt(wrl_ref[...], h3h)
              + br_ref[:, 0:1])
    yield
    eidx = lax.broadcasted_iota(I32, logits.shape, 0)
    work = logits
    vals, idxs = [], []
    for _ in range(TOP_K):
        m = jnp.max(work, axis=0, keepdims=True)
        ik = jnp.min(jnp.where(work == m, eidx, N_EXPERTS), axis=0, keepdims=True)
        vals.append(m)
        idxs.append(ik)
        work = jnp.where(eidx == ik, -jnp.inf, work)
    ex = [jnp.exp(v - vals[0]) for v in vals]
    inv_den = 1.0 / (ex[0] + ex[1] + ex[2] + ex[3])
    w_ref[sub] = jnp.concatenate([e * inv_den for e in ex], axis=0)

    sel = jnp.zeros(logits.shape, F32)
    for ik in idxs:
        sel = sel + jnp.where(eidx == ik, 1.0, 0.0)
    cnt = jnp.broadcast_to(jnp.sum(sel, axis=1, keepdims=True), (N_EXPERTS, LANES))
    cnt_ref[sub] = cnt
    seg_rows = jnp.floor((cnt + (OCT_ROWS - 1)) * (1.0 / OCT_ROWS)) * OCT_ROWS
    seg_start = _dot(tri32_ref[...], seg_rows.astype(BF16))
    pos = seg_start[:, 0:1] + _dot(sel.astype(BF16), tri_ref[...])
    q_ref[sub] = jnp.concatenate(
        [jnp.sum(jnp.where(eidx == ik, pos, 0.0), axis=0, keepdims=True) for ik in idxs],
        axis=0).astype(I32) * TILE_SUBLANES


def _post(x3, bg, cv, yf, conv_w, gc, gf, wo_top, wo_bot, nx, wq, kmem, vmem_, wo, nf, wrh, wrl, br, tri, tri32):
    bsz, seq, _ = x3.shape
    step_rows = POST_SUB * TOK_TILE
    nt = seq // step_rows
    t = bsz * seq
    n_tiles = t // TOK_TILE
    halo_per_tile = step_rows // BF16_SUBLANES
    n_halo = seq // BF16_SUBLANES
    full = lambda shape: pl.BlockSpec(shape, lambda b, i: (0,) * len(shape))
    tile_d = pl.BlockSpec((1, step_rows, D_MODEL), lambda b, i: (b, i, 0))
    tile_h = pl.BlockSpec((1, step_rows, CONV_WIDTH), lambda b, i: (b, i, 0))
    halo_prev = pl.BlockSpec((1, BF16_SUBLANES, CONV_WIDTH),
                             lambda b, i: (b, jnp.maximum(i * halo_per_tile - 1, 0), 0))
    halo_next = pl.BlockSpec((1, BF16_SUBLANES, CONV_WIDTH),
                             lambda b, i: (b, jnp.minimum((i + 1) * halo_per_tile, n_halo - 1), 0))
    mem_blk = pl.BlockSpec((1, MEM_LEN, D_MODEL), lambda b, i: (b, 0, 0))
    tok4 = pl.BlockSpec((POST_SUB, TOP_K, TOK_TILE), lambda b, i: (b * nt + i, 0, 0))
    per_tile = pl.BlockSpec((POST_SUB, N_EXPERTS, LANES), lambda b, i: (b * nt + i, 0, 0))
    tok_tiles = pl.BlockSpec((step_rows * TILE_SUBLANES, LANES), lambda b, i: (b * nt + i, 0))
    return pl.pallas_call(
        _post_kernel,
        grid=(bsz, nt),
        in_specs=[tile_d, tile_h, tile_h, halo_prev, halo_next, tile_h,
                  full((3, CONV_WIDTH)), full((1, CONV_WIDTH)), full((1, FFT_WIDTH)),
                  full((CONV_WIDTH, D_MODEL)), full((FFT_WIDTH, D_MODEL)), full((1, D_MODEL)),
                  full((D_MODEL, D_MODEL)), mem_blk, mem_blk, full((D_MODEL, D_MODEL)), full((1, D_MODEL)),
                  full((N_EXPERTS, D_MODEL)), full((N_EXPERTS, D_MODEL)), full((N_EXPERTS, 128)),
                  full((TOK_TILE, TOK_TILE)), full((N_EXPERTS, N_EXPERTS))],
        out_specs=[tile_d, tok_tiles, tok4, tok4, per_tile],
        out_shape=[jax.ShapeDtypeStruct((bsz, seq, D_MODEL), F32),
                   jax.ShapeDtypeStruct((t * TILE_SUBLANES, LANES), F32),
                   jax.ShapeDtypeStruct((n_tiles, TOP_K, TOK_TILE), F32),
                   jax.ShapeDtypeStruct((n_tiles, TOP_K, TOK_TILE), I32),
                   jax.ShapeDtypeStruct((n_tiles, N_EXPERTS, LANES), F32)],
        compiler_params=pltpu.CompilerParams(dimension_semantics=("parallel", "parallel"),
                                             vmem_limit_bytes=VMEM_LIMIT),
        name="post",
    )(x3, bg, cv, cv, cv, yf, conv_w, gc, gf, wo_top, wo_bot, nx, wq, kmem, vmem_, wo, nf, wrh, wrl, br, tri, tri32)


def _plan_kernel(cnt_ref, src_ref, dst_ref, be_ref, nu_ref, wnext_ref, wpar_ref, half_ref, off_ref, nxt_ref, *,
                 n_tiles):
    n_sched = be_ref.shape[0]
    dump0 = n_tiles * TILE_OCT

    def fill_pad(lo, hi):
        def body(p, c):
            o = p & (OCT_PER_BLOCK - 1)
            parity = lax.shift_right_logical(p, OCT_PER_BLOCK.bit_length() - 1) & 1
            src_ref[p] = dump0 + 2 * OCT_PER_BLOCK + o
            dst_ref[p] = dump0 + parity * OCT_PER_BLOCK + o
            return c
        lax.fori_loop(lo, hi, body, 0)

    def clear(tile, c):
        off_ref[tile] = 0
        return c

    lax.fori_loop(0, n_tiles, clear, 0)

    def find_next(i, nxt):
        e = N_EXPERTS - 1 - i
        nxt_ref[e] = nxt
        total = lax.fori_loop(0, n_tiles, lambda tile, s: s + cnt_ref[tile * N_EXPERTS + e], jnp.int32(0))
        return jnp.where(total > 0, e, nxt)

    lax.fori_loop(0, N_EXPERTS, find_next, jnp.int32(-1))
    fill_pad(0, OCT_PER_BLOCK)

    def per_expert(e, carry):
        pos0, blk0, group = carry

        def per_tile(tile, pos):
            n_oct = lax.shift_right_logical(cnt_ref[tile * N_EXPERTS + e] + (OCT_ROWS - 1), OCT_ROWS.bit_length() - 1)
            base = tile * TILE_OCT + off_ref[tile]
            off_ref[tile] = off_ref[tile] + n_oct

            def per_group(grp, c):
                for u in range(PLAN_UNROLL):
                    o = grp * PLAN_UNROLL + u
                    src_ref[pos + o] = base + o
                    dst_ref[pos + o] = base + o
                return c

            per_group(0, 0)
            per_group(1, 0)
            lax.fori_loop(2, lax.shift_right_logical(n_oct + (PLAN_UNROLL - 1), PLAN_UNROLL.bit_length() - 1),
                          per_group, 0)
            return pos + n_oct

        pos1 = lax.fori_loop(0, n_tiles, per_tile, pos0)
        nb = lax.shift_right_logical(pos1 - pos0 + (OCT_PER_BLOCK - 1), OCT_PER_BLOCK.bit_length() - 1)
        pos2 = pos0 + nb * OCT_PER_BLOCK
        fill_pad(pos1, pos2)

        last_octets = pos1 - (pos2 - OCT_PER_BLOCK)
        last_is_half = jnp.where(last_octets <= OCT_PER_BLOCK // 2, 1, 0)

        def fill(j, c):
            be_ref[blk0 + j] = e
            wnext_ref[blk0 + j] = nxt_ref[e]
            wpar_ref[blk0 + j] = group & 1
            half_ref[blk0 + j] = jnp.where(j == nb - 1, last_is_half, 0)
            return c

        lax.fori_loop(0, nb, fill, 0)
        return pos2, blk0 + nb, group + jnp.where(nb > 0, 1, 0)

    pos, n_used, _ = lax.fori_loop(0, N_EXPERTS, per_expert,
                                   (jnp.int32(OCT_PER_BLOCK), jnp.int32(0), jnp.int32(0)))
    nu_ref[0] = n_used
    tail_e = be_ref[n_used - 1]

    def tail(j, c):
        be_ref[j] = tail_e
        wnext_ref[j] = -1
        wpar_ref[j] = 0
        half_ref[j] = 0
        return c

    lax.fori_loop(n_used, n_sched, tail, 0)
    fill_pad(pos, src_ref.shape[0])


def _plan(counts, n_tiles, n_blocks):
    smem = pl.BlockSpec(memory_space=pltpu.SMEM)
    grid_spec = pltpu.PrefetchScalarGridSpec(
        num_scalar_prefetch=1,
        grid=(1,),
        in_specs=[],
        out_specs=[smem] * 7,
        scratch_shapes=[pltpu.SMEM((n_tiles,), I32), pltpu.SMEM((N_EXPERTS,), I32)],
    )
    octs = jax.ShapeDtypeStruct(((n_blocks + 2) * OCT_PER_BLOCK,), I32)
    sched = jax.ShapeDtypeStruct((n_blocks + 1,), I32)
    return pl.pallas_call(
        functools.partial(_plan_kernel, n_tiles=n_tiles),
        grid_spec=grid_spec,
        out_shape=[octs, octs, sched, jax.ShapeDtypeStruct((1,), I32), sched, sched, sched],
        compiler_params=pltpu.CompilerParams(dimension_semantics=("arbitrary",), vmem_limit_bytes=VMEM_LIMIT),
        name="plan",
    )(counts)


def _localsort_kernel(q_ref, h3_ref, xs_ref):
    xs_ref[...] = jnp.zeros_like(xs_ref)

    @pl.when(pl.program_id(0) < pl.num_programs(0) - 1)
    def _():
        def group(g, c):
            for u in range(TILE_SUBLANES):
                t = g * TILE_SUBLANES + u
                row = _tile_rows(h3_ref, t)[...]
                for k in range(TOP_K):
                    _tile_at(xs_ref, q_ref[0, t * TOP_K + k])[...] = row
            return c

        lax.fori_loop(0, TOK_TILE // TILE_SUBLANES, group, 0)


def _localsort(q_tiles, h3t):
    n_tiles = q_tiles.shape[0]
    last = n_tiles - 1
    region = TILE_OCT * OCT_ROWS * TILE_SUBLANES
    return pl.pallas_call(
        _localsort_kernel,
        grid=(n_tiles + 1,),
        in_specs=[pl.BlockSpec((None, 1, TOP_K * TOK_TILE), lambda i: (jnp.minimum(i, last), 0, 0),
                               memory_space=pltpu.SMEM),
                  pl.BlockSpec((TOK_TILE * TILE_SUBLANES, LANES), lambda i: (jnp.minimum(i, last), 0))],
        out_specs=pl.BlockSpec((region, LANES), lambda i: (i, 0)),
        out_shape=jax.ShapeDtypeStruct(((n_tiles + 1) * region, LANES), F32),
        compiler_params=pltpu.CompilerParams(dimension_semantics=("parallel",), vmem_limit_bytes=VMEM_LIMIT),
        name="localsort",
    )(q_tiles, h3t)


def _tile_at(ref, sublane_row):
    return ref.at[pl.ds(pl.multiple_of(sublane_row, TILE_SUBLANES), TILE_SUBLANES), :]


def _tile_rows(ref, row, n_rows=1):
    return ref.at[pl.ds(pl.multiple_of(row * TILE_SUBLANES, TILE_SUBLANES), n_rows * TILE_SUBLANES), :]


def _expert_kernel(be_ref, nu_ref, wnext_ref, wpar_ref, half_ref,
                   src_cur_ref, src_next_ref, dst_prev_ref, bgu_ref, bd_ref, xs_ref, wgu_hbm, wd_hbm,
                   y_ref,
                   xbuf0, xbuf1, obuf0, obuf1, wgu_f32, wd_f32, wgu_bf, wd_bf, gsem, ssem, wsem):
    j = pl.program_id(0)
    nu = nu_ref[0]
    xbuf = (xbuf0, xbuf1)
    obuf = (obuf0, obuf1)
    octet = lambda ref, o: _tile_rows(ref, o * OCT_ROWS, OCT_ROWS)

    def gather_octet(src_ref, o, dst):
        return pltpu.make_async_copy(octet(xs_ref, src_ref[0, o]), octet(dst, o), gsem)

    def scatter_octet(dst_ref, o, src):
        return pltpu.make_async_copy(octet(src, o), octet(y_ref, dst_ref[0, o]), ssem)

    def weight_copies(e, p):
        return (pltpu.make_async_copy(wgu_hbm.at[e], wgu_f32.at[p], wsem.at[p, 0]),
                pltpu.make_async_copy(wd_hbm.at[e], wd_f32.at[p], wsem.at[p, 1]))

    def wait_gather(dst):
        for _ in range(OCT_PER_BLOCK):
            pltpu.make_async_copy(octet(xs_ref, 0), octet(dst, 0), gsem).wait()

    def wait_scatter(src):
        for _ in range(OCT_PER_BLOCK):
            pltpu.make_async_copy(octet(src, 0), octet(y_ref, 0), ssem).wait()

    @pl.when(j == 0)
    def _():
        obuf0[...] = jnp.zeros_like(obuf0)
        obuf1[...] = jnp.zeros_like(obuf1)
        for c in weight_copies(be_ref[0], 0):
            c.start(priority=WEIGHT_DMA_PRIORITY)
        for o in range(OCT_PER_BLOCK):
            gather_octet(src_cur_ref, o, xbuf0).start(priority=GATHER_DMA_PRIORITY)

    first_of_group = jnp.logical_or(j == 0, be_ref[j] != be_ref[jnp.maximum(j - 1, 0)])

    @pl.when(jnp.logical_and(j < nu, first_of_group))
    def _():
        p = wpar_ref[j]
        for c in weight_copies(be_ref[j], p):
            c.wait()
        wgu_bf[...] = wgu_f32[p].astype(BF16)
        wd_bf[...] = wd_f32[p].astype(BF16)

        @pl.when(wnext_ref[j] >= 0)
        def _():
            for c in weight_copies(wnext_ref[j], 1 - p):
                c.start(priority=WEIGHT_DMA_PRIORITY)

    def step(s, rows):
        compute = rows > 0
        wait_gather(xbuf[s])

        @pl.when(j >= 1)
        def _():
            wait_scatter(obuf[s])

        def start_scatter():
            for o in range(OCT_PER_BLOCK):
                scatter_octet(dst_prev_ref, o, obuf[1 - s]).start(priority=SCATTER_DMA_PRIORITY)

        if compute:
            for o in range(OCT_PER_BLOCK):
                gather_octet(src_next_ref, o, xbuf[1 - s]).start(priority=GATHER_DMA_PRIORITY)
            x = _load_token_tiles(xbuf[s], rows).astype(BF16)
            gu = _dot(x, wgu_bf[...]) + bgu_ref[0]
            gate = jnp.minimum(gu[:, :D_EXPERT], SWIGLU_LIMIT)
            up = jnp.clip(gu[:, D_EXPERT:], -SWIGLU_LIMIT, SWIGLU_LIMIT)
            glu = gate * (1.0 / (1.0 + jnp.exp(-SWIGLU_ALPHA * gate)))
            h = ((up + 1.0) * glu).astype(BF16)
            start_scatter()
            _store_token_tiles(obuf[s], _dot(h, wd_bf[...]) + bd_ref[0], rows)
        else:
            start_scatter()
            wait_scatter(obuf[1 - s])

    is_half = half_ref[j] == 1
    for s in range(2):
        parity = (j & 1) == s
        used = jnp.logical_and(j < nu, parity)
        pl.when(jnp.logical_and(used, jnp.logical_not(is_half)))(functools.partial(step, s, ROW_BLOCK))
        pl.when(jnp.logical_and(used, is_half))(functools.partial(step, s, ROW_BLOCK // 2))
        pl.when(jnp.logical_and(j == nu, parity))(functools.partial(step, s, 0))


def _experts(block_e, n_used, w_next, w_par, half, src3, dst3, xs, wgu, bgu, wd, bd):
    n_steps = block_e.shape[0]
    blk_rows = ROW_BLOCK * TILE_SUBLANES
    last_blk = src3.shape[0] - 1
    exp_map = lambda j, be, *sched: (be[j], 0, 0)
    oct_spec = lambda off: pl.BlockSpec((None, 1, OCT_PER_BLOCK),
                                        lambda j, *sched: (jnp.minimum(j + off, last_blk), 0, 0),
                                        memory_space=pltpu.SMEM)
    hbm = pl.BlockSpec(memory_space=pl.ANY)
    grid_spec = pltpu.PrefetchScalarGridSpec(
        num_scalar_prefetch=5,
        grid=(n_steps,),
        in_specs=[oct_spec(1), oct_spec(2), oct_spec(0),
                  pl.BlockSpec((1, 1, 2 * D_EXPERT), exp_map), pl.BlockSpec((1, 1, D_MODEL), exp_map),
                  hbm, hbm, hbm],
        out_specs=hbm,
        scratch_shapes=[pltpu.VMEM((blk_rows, LANES), F32)] * 4
        + [pltpu.VMEM((2, D_MODEL, 2 * D_EXPERT), F32), pltpu.VMEM((2, D_EXPERT, D_MODEL), F32),
           pltpu.VMEM((D_MODEL, 2 * D_EXPERT), BF16), pltpu.VMEM((D_EXPERT, D_MODEL), BF16),
           pltpu.SemaphoreType.DMA, pltpu.SemaphoreType.DMA, pltpu.SemaphoreType.DMA((2, 2))],
    )
    return pl.pallas_call(
        _expert_kernel,
        grid_spec=grid_spec,
        out_shape=jax.ShapeDtypeStruct(xs.shape, F32),
        input_output_aliases={10: 0},
        compiler_params=pltpu.CompilerParams(dimension_semantics=("arbitrary",), vmem_limit_bytes=VMEM_LIMIT),
        name="experts",
    )(block_e, n_used, w_next, w_par, half, src3, src3, dst3, bgu, bd, xs, wgu, wd)


def _combine_kernel(q_ref, w_ref, y_ref, x2_ref, g_ref, out_ref, acc_ref):
    def group(grp, c):
        for u in range(TILE_SUBLANES):
            t = grp * TILE_SUBLANES + u
            acc = w_ref[0, t * TOP_K] * _tile_at(y_ref, q_ref[0, t * TOP_K])[...]
            for k in range(1, TOP_K):
                acc = acc + w_ref[0, t * TOP_K + k] * _tile_at(y_ref, q_ref[0, t * TOP_K + k])[...]
            _tile_rows(acc_ref, t)[...] = acc
        return c

    lax.fori_loop(0, TOK_TILE // TILE_SUBLANES, group, 0)
    out_ref[...] = _rms(x2_ref[...] + _load_token_tiles(acc_ref, TOK_TILE), g_ref[...])


def _combine(q_tiles, w_tiles, ybuf, x2, g):
    n_tiles = q_tiles.shape[0]
    region = TILE_OCT * OCT_ROWS * TILE_SUBLANES
    tok4 = pl.BlockSpec((None, 1, TOP_K * TOK_TILE), lambda i: (i, 0, 0), memory_space=pltpu.SMEM)
    return pl.pallas_call(
        _combine_kernel,
        grid=(n_tiles,),
        in_specs=[tok4, tok4,
                  pl.BlockSpec((region, LANES), lambda i: (i, 0)),
                  pl.BlockSpec((TOK_TILE, D_MODEL), lambda i: (i, 0)),
                  pl.BlockSpec((1, D_MODEL), lambda i: (0, 0))],
        out_specs=pl.BlockSpec((TOK_TILE, D_MODEL), lambda i: (i, 0)),
        out_shape=jax.ShapeDtypeStruct(x2.shape, F32),
        scratch_shapes=[pltpu.VMEM((TOK_TILE * TILE_SUBLANES, LANES), F32)],
        compiler_params=pltpu.CompilerParams(dimension_semantics=("parallel",), vmem_limit_bytes=VMEM_LIMIT),
        name="combine",
    )(q_tiles, w_tiles, ybuf, x2, g)


def _layer(x, mem, norm_mix, w_in, conv_w, g_conv_out, g_fft_out, w_out, norm_xattn, norm_mem,
           w_q, w_k, w_v, w_o, norm_ffn, w_router, b_router, w_gate_up, b_gate_up, w_down, b_down, tables):
    bsz, seq, d = x.shape
    t = bsz * seq
    cs, m1, m2 = tables
    row = lambda v: v.reshape(1, -1)

    kmem, vmem_ = _memkv(mem, row(norm_mem), w_k, w_v)

    bg, cv, a, s = _inproj(x.reshape(t, d), row(norm_mix), w_in, cs)
    cols = FFT_N2 * FFT_WIDTH
    g = _fft1(a.reshape(bsz, FFT_N1, cols), s.reshape(bsz, FFT_N1, cols), m1)
    yf = _fft2(g.reshape(bsz, 2, FFT_N1, FFT_N2, FFT_WIDTH), m2).reshape(bsz, seq, FFT_WIDTH)

    w_out_b = w_out.astype(BF16)
    wr_t = w_router.T
    wr_hi = wr_t.astype(BF16)
    wr_lo = (wr_t - wr_hi.astype(F32)).astype(BF16)
    tri = (jnp.arange(TOK_TILE)[:, None] < jnp.arange(TOK_TILE)[None, :]).astype(BF16)
    tri32 = (jnp.arange(N_EXPERTS)[None, :] < jnp.arange(N_EXPERTS)[:, None]).astype(BF16)
    x2, h3, w_tiles, q_tiles, cnt = _post(
        x, bg.reshape(bsz, seq, CONV_WIDTH), cv.reshape(bsz, seq, CONV_WIDTH), yf,
        conv_w, row(g_conv_out), row(g_fft_out), w_out_b[:CONV_WIDTH], w_out_b[CONV_WIDTH:],
        row(norm_xattn), w_q.astype(BF16), kmem, vmem_, w_o.astype(BF16), row(norm_ffn),
        wr_hi, wr_lo, jnp.broadcast_to(b_router[:, None], (N_EXPERTS, 128)), tri, tri32)

    n_tiles = t // TOK_TILE
    n_blocks = (t * TOP_K + n_tiles * N_EXPERTS * (OCT_ROWS - 1)) // ROW_BLOCK + N_EXPERTS
    counts = cnt[:, :, 0].astype(I32).reshape(-1)
    src, dst, block_e, n_used, w_next, w_par, half = _plan(counts, n_tiles, n_blocks)
    per_block = lambda v: v.reshape(n_blocks + 2, 1, OCT_PER_BLOCK)
    by_token = lambda v: v.transpose(0, 2, 1).reshape(n_tiles, 1, TOK_TILE * TOP_K)
    q_tiles, w_tiles = by_token(q_tiles), by_token(w_tiles)
    xs = _localsort(q_tiles, h3)
    ybuf = _experts(block_e, n_used, w_next, w_par, half, per_block(src), per_block(dst), xs,
                    w_gate_up, b_gate_up[:, None, :], w_down, b_down[:, None, :])
    return x2.reshape(t, d), ybuf, q_tiles, w_tiles


def kernel(x, mem, norm_mix, w_in, conv_w, g_conv_out, g_fft_out, w_out, norm_xattn, norm_mem, w_q, w_k, w_v, w_o,
           norm_ffn, w_router, b_router, w_gate_up, b_gate_up, w_down, b_down, norm_final):
    bsz, seq, d = x.shape
    depth = norm_mix.shape[0]
    assert depth == 1, "final norm is fused into the combine step of the single layer"
    tables = _dft_tables(seq)
    x2, ybuf, q_tiles, w_tiles = _layer(
        x, mem, norm_mix[0], w_in[0], conv_w[0], g_conv_out[0], g_fft_out[0], w_out[0], norm_xattn[0],
        norm_mem[0], w_q[0], w_k[0], w_v[0], w_o[0], norm_ffn[0], w_router[0], b_router[0],
        w_gate_up[0], b_gate_up[0], w_down[0], b_down[0], tables)
    out = _combine(q_tiles, w_tiles, ybuf, x2, norm_final.reshape(1, -1))
    return out.reshape(bsz, seq, d)
```
